```python
import math
import jax
import jax.numpy as jnp
from jax import lax
import numpy as np

D_MODEL = 1024
BATCH = 8
SEQ = 2048
DEPTH = 4

CTX_LEN = 256
GRID_W = 64
EPS = 1e-6
CHUNK = 64
CONV_W = 5

S5_WIDTH = D_MODEL // 2
S5_GROUP = 16
S5_GROUPS = S5_WIDTH // S5_GROUP
S5_STATE = 64
GDN_DK = 128
GDN_DV = 128
GDN_WIDTH = D_MODEL // 2
GDN_HEADS = GDN_WIDTH // GDN_DV
GDN_CONV_CH = 2 * GDN_HEADS * GDN_DK + GDN_WIDTH
M2_WIDTH = D_MODEL
M2_HEADDIM = 64
M2_HEADS = M2_WIDTH // M2_HEADDIM
M2_GROUPS = 2
M2_HPG = M2_HEADS // M2_GROUPS
M2_STATE = 128
M2_CONV_CH = M2_WIDTH + 2 * M2_GROUPS * M2_STATE

MIX_WIDTH = S5_WIDTH + GDN_WIDTH + M2_WIDTH
IN_SIZES = (S5_WIDTH, GDN_CONV_CH, GDN_WIDTH, 2 * GDN_HEADS, 2 * GDN_HEADS,
            M2_WIDTH, M2_CONV_CH, 2 * M2_HEADS)
D_IN = sum(IN_SIZES)

D_FF = 256 * ((8 * D_MODEL // 3 + 255) // 256)
N_EXPERTS = 8
TOP_K = 2

F32 = jnp.float32

kernel_name = "hybrid_s5_gdn_ssd_moe_prefix_dit"


def rmsnorm(x, g):
    xf = x.astype(F32)
    y = xf * lax.rsqrt(jnp.mean(xf * xf, axis=-1, keepdims=True) + EPS)
    return (y * g.astype(F32)).astype(x.dtype)


def modulate(x, g, shift, scale):
    return rmsnorm(x, g) * (1 + scale) + shift


def l2norm(t):
    return t * lax.rsqrt(jnp.sum(t * t, axis=-1, keepdims=True) + EPS)


def grid_transpose(u, rows, cols):
    b, l, ch = u.shape
    return u.reshape(b, rows, cols, ch).transpose(0, 2, 1, 3).reshape(b, l, ch)


def split_cols(u, sizes):
    parts, start = [], 0
    for s in sizes:
        parts.append(u[..., start:start + s])
        start += s
    return parts


def rev(t, flag):
    return jnp.flip(t, axis=1) if flag else t


def dwconv(u, w):
    pad = CONV_W // 2
    return lax.conv_general_dilated(
        u, w[:, None, :].astype(u.dtype), window_strides=(1,), padding=[(pad, pad)],
        dimension_numbers=("NWC", "WIO", "NWC"), feature_group_count=u.shape[-1])


def seg_decay(cs):
    t = cs.shape[-1]
    mask = jnp.tril(jnp.ones((t, t), dtype=bool))
    diff = cs[..., :, None] - cs[..., None, :]
    return jnp.where(mask, jnp.exp(jnp.where(mask, diff, 0.0)), 0.0)


def _affine_combine(left, right):
    a_l, b_l = left
    a_r, b_r = right
    return a_l * a_r, a_r * b_l + b_r


def linear_scan(a, b, h0, reverse):
    a_cum, h = lax.associative_scan(_affine_combine, (a, b), axis=1, reverse=reverse)
    if h0 is not None:
        h = h + a_cum * h0[:, None]
    return h


def s5_direction(ug, lam_re, lam_im, b_re, b_im, log_dt, h0, reverse):
    lam = lax.complex(jnp.minimum(lam_re.astype(F32), -1e-4), lam_im.astype(F32))
    dt = jnp.exp(log_dt.astype(F32))[:, None]
    lam_bar = jnp.exp(lam * dt)
    gamma = (lam_bar - 1.0) / lam
    bu = lax.complex(jnp.einsum("blgi,gpi->blgp", ug, b_re.astype(F32)),
                     jnp.einsum("blgi,gpi->blgp", ug, b_im.astype(F32))) * gamma
    return linear_scan(jnp.broadcast_to(lam_bar, bu.shape), bu, h0, reverse)


def s5_readout(states, c_re, c_im):
    return (jnp.einsum("blgp,gip->blgi", states.real, c_re.astype(F32))
            - jnp.einsum("blgp,gip->blgi", states.imag, c_im.astype(F32)))


def s5_mixer(u_ctx, u_lat, lam_re, lam_im, b_re, b_im, c_re, c_im, log_dt, d_skip, w_glu, ctx_out):
    uc = u_ctx.reshape(*u_ctx.shape[:2], S5_GROUPS, S5_GROUP)
    ul = u_lat.reshape(*u_lat.shape[:2], S5_GROUPS, S5_GROUP)
    y_ctx, y_lat = [], []
    for d in range(2):
        reverse = d == 1
        p = (lam_re[d], lam_im[d], b_re[d], b_im[d], log_dt[d])
        s_ctx = s5_direction(uc, *p, None, reverse)
        h_end = s_ctx[:, 0] if reverse else s_ctx[:, -1]
        s_lat = s5_direction(ul, *p, h_end, reverse)
        y_lat.append(s5_readout(s_lat, c_re[d], c_im[d]))
        if ctx_out:
            y_ctx.append(s5_readout(s_ctx, c_re[d], c_im[d]))

    def finish(ys, u):
        y = (ys[0] + ys[1]).reshape(u.shape) + d_skip.astype(F32) * u
        y = jax.nn.gelu(y)
        return y * jax.nn.sigmoid(y @ w_glu.astype(F32))

    return (finish(y_ctx, u_ctx) if ctx_out else None), finish(y_lat, u_lat)


def gdn_prep(qkv, a_raw, b_raw, conv_w, a_log, dt_bias):
    qkv = jax.nn.silu(dwconv(qkv, conv_w))
    bsz, l, _ = qkv.shape
    q, k, v = split_cols(qkv, (GDN_HEADS * GDN_DK, GDN_HEADS * GDN_DK, GDN_WIDTH))
    q = l2norm(q.reshape(bsz, l, GDN_HEADS, GDN_DK))
    k = l2norm(k.reshape(bsz, l, GDN_HEADS, GDN_DK))
    v = v.reshape(bsz, l, GDN_HEADS, GDN_DV)
    g = -jnp.exp(a_log.astype(F32)) * jax.nn.softplus(
        a_raw.reshape(bsz, l, 2, GDN_HEADS) + dt_bias.astype(F32))
    beta = jax.nn.sigmoid(b_raw.reshape(bsz, l, 2, GDN_HEADS))
    return q, k, v, g, beta


def gdn_chunked(q, k, v, g, beta, s0, with_output):
    b, l, h, dk = q.shape
    n = l // CHUNK

    def blk(t):
        return jnp.moveaxis(t.reshape(b, n, CHUNK, h, *t.shape[3:]), 3, 1)

    q, k, v, beta = blk(q) * dk ** -0.5, blk(k), blk(v), blk(beta)
    gc = jnp.cumsum(blk(g), axis=-1)
    decay = seg_decay(gc)
    strict = jnp.tril(jnp.ones((CHUNK, CHUNK), dtype=bool), -1)
    kb = k * beta[..., None]
    lower = jnp.where(strict, jnp.einsum("bhnid,bhnjd->bhnij", kb, k) * decay, 0.0)
    rhs = jnp.concatenate([v * beta[..., None], kb * jnp.exp(gc)[..., None]], axis=-1)
    sol = lax.linalg.triangular_solve(jnp.eye(CHUNK, dtype=F32) + lower, rhs,
                                      left_side=True, lower=True, unit_diagonal=True)
    dv = v.shape[-1]
    u_c, w_c = sol[..., :dv], sol[..., dv:]
    k_dec = k * jnp.exp(gc[..., -1:] - gc)[..., None]
    g_last = jnp.exp(gc[..., -1])
    xs = [u_c, w_c, k_dec, g_last]
    if with_output:
        xs += [q * jnp.exp(gc)[..., None], jnp.einsum("bhnid,bhnjd->bhnij", q, k) * decay]

    def step(s, inp):
        u_i, w_i, kd_i, gl_i = inp[:4]
        v_new = u_i - jnp.einsum("bhcd,bhde->bhce", w_i, s)
        s_next = s * gl_i[..., None, None] + jnp.einsum("bhcd,bhce->bhde", kd_i, v_new)
        if not with_output:
            return s_next, None
        qd_i, a_i = inp[4:]
        o = jnp.einsum("bhcd,bhde->bhce", qd_i, s) + jnp.einsum("bhij,bhje->bhie", a_i, v_new)
        return s_next, o

    s_fin, o = lax.scan(step, s0, [jnp.moveaxis(t, 2, 0) for t in xs])
    if with_output:
        o = jnp.moveaxis(jnp.moveaxis(o, 0, 2), 1, 3).reshape(b, l, h, dv)
    return o, s_fin


def gdn_direction(p, d, s_init, with_output):
    q, k, v, g, beta = p
    r = d == 1
    o, s_fin = gdn_chunked(rev(q, r), rev(k, r), rev(v, r), rev(g[:, :, d], r), rev(beta[:, :, d], r),
                           s_init, with_output)
    return (rev(o, r) if with_output else None), s_fin


def gdn_mixer(ctx_in, lat_in, conv_w, a_log, dt_bias, norm_g, ctx_out):
    pc = gdn_prep(ctx_in[0], ctx_in[2], ctx_in[3], conv_w, a_log, dt_bias)
    pl = gdn_prep(lat_in[0], lat_in[2], lat_in[3], conv_w, a_log, dt_bias)
    s0 = jnp.zeros((ctx_in[0].shape[0], GDN_HEADS, GDN_DK, GDN_DV), F32)
    o_ctx, o_lat = [], []
    for d in range(2):
        oc, s_ctx = gdn_direction(pc, d, s0, ctx_out)
        ol, _ = gdn_direction(pl, d, s_ctx, True)
        o_lat.append(ol)
        if ctx_out:
            o_ctx.append(oc)

    def finish(os, z):
        o = rmsnorm(os[0] + os[1], norm_g)
        return (o * jax.nn.silu(z).reshape(o.shape)).reshape(z.shape)

    return (finish(o_ctx, ctx_in[1]) if ctx_out else None), finish(o_lat, lat_in[1])


def m2_prep(xbc, dt_raw, conv_w, conv_b, dt_bias):
    xbc = jax.nn.silu(dwconv(xbc, conv_w) + conv_b.astype(F32))
    bsz, l, _ = xbc.shape
    xs, bm, cm = split_cols(xbc, (M2_WIDTH, M2_GROUPS * M2_STATE, M2_GROUPS * M2_STATE))
    xs = xs.reshape(bsz, l, M2_GROUPS, M2_HPG, M2_HEADDIM)
    bm = bm.reshape(bsz, l, M2_GROUPS, M2_STATE)
    cm = cm.reshape(bsz, l, M2_GROUPS, M2_STATE)
    dt = jax.nn.softplus(dt_raw.reshape(bsz, l, 2, M2_GROUPS, M2_HPG)
                         + dt_bias.astype(F32).reshape(2, M2_GROUPS, M2_HPG))
    return xs, bm, cm, dt


def ssd_chunked(xdt, log_a, bm, cm, h0, with_output):
    bsz, l = xdt.shape[:2]
    n = l // CHUNK
    xc = xdt.reshape(bsz, n, CHUNK, *xdt.shape[2:])
    bc = bm.reshape(bsz, n, CHUNK, *bm.shape[2:])
    cc = cm.reshape(bsz, n, CHUNK, *cm.shape[2:])
    a_cs = jnp.cumsum(jnp.moveaxis(log_a.reshape(bsz, n, CHUNK, M2_GROUPS, M2_HPG), (3, 4), (1, 2)),
                      axis=-1)
    decay_to_end = jnp.exp(a_cs[..., -1:] - a_cs)
    chunk_states = jnp.einsum("bncge,bgjnc,bncgjp->bngjpe", bc, decay_to_end, xc)
    if h0 is None:
        h0 = jnp.zeros_like(chunk_states[:, 0])
    chunk_states = jnp.concatenate([h0[:, None], chunk_states], axis=1)
    chunk_cs = jnp.cumsum(jnp.pad(a_cs[..., -1], [(0, 0)] * 3 + [(1, 0)]), axis=-1)
    states = jnp.einsum("bgjzy,bygjpe->bzgjpe", seg_decay(chunk_cs), chunk_states)
    if not with_output:
        return None, states[:, -1]
    y_diag = jnp.einsum("bncge,bnsge,bgjncs,bnsgjp->bncgjp", cc, bc, seg_decay(a_cs), xc)
    y_off = jnp.einsum("bncge,bngjpe,bgjnc->bncgjp", cc, states[:, :-1], jnp.exp(a_cs))
    return (y_diag + y_off).reshape(xdt.shape), states[:, -1]


def m2_direction(p, a, d, h0, with_output):
    xs, bm, cm, dt = p
    r = d == 1
    dtd = dt[:, :, d]
    y, h = ssd_chunked(rev(xs * dtd[..., None], r), rev(dtd * a[d], r), rev(bm, r), rev(cm, r),
                       h0, with_output)
    return (rev(y, r) if with_output else None), h


def m2_mixer(ctx_in, lat_in, conv_w, conv_b, a_log, dt_bias, d_skip, norm_g, ctx_out):
    pc = m2_prep(ctx_in[1], ctx_in[2], conv_w, conv_b, dt_bias)
    pl = m2_prep(lat_in[1], lat_in[2], conv_w, conv_b, dt_bias)
    a = -jnp.exp(a_log.astype(F32)).reshape(2, M2_GROUPS, M2_HPG)
    y_ctx, y_lat = [], []
    for d in range(2):
        yc, h_ctx = m2_direction(pc, a, d, None, ctx_out)
        yl, _ = m2_direction(pl, a, d, h_ctx, True)
        y_lat.append(yl)
        if ctx_out:
            y_ctx.append(yc)
    d_skip = d_skip.astype(F32).reshape(M2_GROUPS, M2_HPG, 1)

    def finish(ys, xs, z):
        bsz, l = z.shape[:2]
        y = (ys[0] + ys[1] + d_skip * xs).reshape(bsz, l, M2_GROUPS, -1)
        y = y * jax.nn.silu(z).reshape(bsz, l, M2_GROUPS, -1)
        return rmsnorm(y, norm_g.reshape(M2_GROUPS, -1)).reshape(bsz, l, M2_WIDTH)

    return (finish(y_ctx, pc[0], ctx_in[0]) if ctx_out else None), finish(y_lat, pl[0], lat_in[0])


def token_mixer(h_ctx, h_lat, w_in, s5_p, gdn_p, m2_p, ctx_out):
    pc = split_cols((h_ctx @ w_in).astype(F32), IN_SIZES)
    pl = split_cols((h_lat @ w_in).astype(F32), IN_SIZES)
    a_ctx, a_lat = s5_mixer(pc[0], pl[0], *s5_p, ctx_out)
    b_ctx, b_lat = gdn_mixer(pc[1:5], pl[1:5], *gdn_p, ctx_out)
    m_ctx, m_lat = m2_mixer(pc[5:8], pl[5:8], *m2_p, ctx_out)
    y_lat = jnp.concatenate([a_lat, b_lat, m_lat], axis=-1)
    y_ctx = jnp.concatenate([a_ctx, b_ctx, m_ctx], axis=-1) if ctx_out else None
    return y_ctx, y_lat


def swiglu(h, w_gate, w_up, w_down):
    return (jax.nn.silu(h @ w_gate) * (h @ w_up)) @ w_down


def moe_swiglu(h, router_w, w_gate, w_up, w_down):
    logits = (h @ router_w).astype(F32)
    top_val, top_idx = lax.top_k(logits, TOP_K)
    top_w = jax.nn.softmax(top_val, axis=-1)
    gates = jnp.einsum("...k,...ke->...e", top_w,
                       jax.nn.one_hot(top_idx, N_EXPERTS, dtype=F32)).astype(h.dtype)
    out = jnp.zeros_like(h)
    for e in range(N_EXPERTS):
        out = out + gates[..., e:e + 1] * swiglu(h, w_gate[e], w_up[e], w_down[e])
    return out


def setup_inputs(seed: int = 0) -> dict:
    key = jax.random.key(seed)
    keys = iter(jax.random.split(key, 48))

    def nrm(shape, scale):
        return scale * jax.random.normal(next(keys), shape, F32)

    def unif(shape, lo, hi):
        return jax.random.uniform(next(keys), shape, F32, lo, hi)

    def dt_bias(shape):
        dt = jnp.exp(unif(shape, math.log(1e-3), math.log(1e-1)))
        return dt + jnp.log(-jnp.expm1(-dt))

    n_dense, n_moe = (DEPTH + 1) // 2, DEPTH // 2
    s5_shape = (DEPTH, 2, S5_GROUPS, S5_STATE)
    return {
        "x": nrm((BATCH, SEQ, D_MODEL), 1.0),
        "c": nrm((BATCH, D_MODEL), 1.0),
        "ctx": nrm((BATCH, CTX_LEN, D_MODEL), 1.0),
        "c_ctx": nrm((D_MODEL,), 1.0),
        "ada_w": nrm((DEPTH, D_MODEL, 6 * D_MODEL), 0.5 * D_MODEL ** -0.5),
        "ada_b": nrm((DEPTH, 6 * D_MODEL), 0.01),
        "norm1_g": 1.0 + nrm((DEPTH, D_MODEL), 0.02),
        "norm2_g": 1.0 + nrm((DEPTH, D_MODEL), 0.02),
        "w_in": nrm((DEPTH, D_MODEL, D_IN), D_MODEL ** -0.5),
        "w_out": nrm((DEPTH, MIX_WIDTH, D_MODEL), MIX_WIDTH ** -0.5),
        "s5_lam_re": -0.5 + nrm(s5_shape, 0.01),
        "s5_lam_im": math.pi * jnp.arange(S5_STATE, dtype=F32) + nrm(s5_shape, 0.01),
        "s5_b_re": nrm((DEPTH, 2, S5_GROUPS, S5_STATE, S5_GROUP), (2 * S5_GROUP) ** -0.5),
        "s5_b_im": nrm((DEPTH, 2, S5_GROUPS, S5_STATE, S5_GROUP), (2 * S5_GROUP) ** -0.5),
        "s5_c_re": nrm((DEPTH, 2, S5_GROUPS, S5_GROUP, S5_STATE), S5_STATE ** -0.5),
        "s5_c_im": nrm((DEPTH, 2, S5_GROUPS, S5_GROUP, S5_STATE), S5_STATE ** -0.5),
        "s5_log_dt": unif((DEPTH, 2, S5_GROUPS), math.log(1e-3), math.log(1e-1)),
        "s5_d": nrm((DEPTH, S5_WIDTH), 1.0),
        "s5_w_glu": nrm((DEPTH, S5_WIDTH, S5_WIDTH), S5_WIDTH ** -0.5),
        "gdn_conv_w": nrm((DEPTH, CONV_W, GDN_CONV_CH), CONV_W ** -0.5),
        "gdn_a_log": jnp.log(unif((DEPTH, 2, GDN_HEADS), 1.0, 16.0)),
        "gdn_dt_bias": dt_bias((DEPTH, 2, GDN_HEADS)),
        "gdn_norm_g": 1.0 + nrm((DEPTH, GDN_DV), 0.02),
        "m2_conv_w": nrm((DEPTH, CONV_W, M2_CONV_CH), CONV_W ** -0.5),
        "m2_conv_b": nrm((DEPTH, M2_CONV_CH), 0.02),
        "m2_a_log": jnp.log(unif((DEPTH, 2, M2_HEADS), 1.0, 16.0)),
        "m2_dt_bias": dt_bias((DEPTH, 2, M2_HEADS)),
        "m2_d": 1.0 + nrm((DEPTH, M2_HEADS), 0.02),
        "m2_norm_g": 1.0 + nrm((DEPTH, M2_WIDTH), 0.02),
        "ffn_w_gate": nrm((n_dense, D_MODEL, D_FF), D_MODEL ** -0.5),
        "ffn_w_up": nrm((n_dense, D_MODEL, D_FF), D_MODEL ** -0.5),
        "ffn_w_down": nrm((n_dense, D_FF, D_MODEL), D_FF ** -0.5),
        "moe_router": nrm((n_moe, D_MODEL, N_EXPERTS), D_MODEL ** -0.5),
        "moe_w_gate": nrm((n_moe, N_EXPERTS, D_MODEL, D_FF), D_MODEL ** -0.5),
        "moe_w_up": nrm((n_moe, N_EXPERTS, D_MODEL, D_FF), D_MODEL ** -0.5),
        "moe_w_down": nrm((n_moe, N_EXPERTS, D_FF, D_MODEL), D_FF ** -0.5),
        "final_norm_g": 1.0 + nrm((D_MODEL,), 0.02),
    }


def reference(x, c, ctx, c_ctx, ada_w, ada_b, norm1_g, norm2_g, w_in, w_out,
              s5_lam_re, s5_lam_im, s5_b_re, s5_b_im, s5_c_re, s5_c_im, s5_log_dt, s5_d, s5_w_glu,
              gdn_conv_w, gdn_a_log, gdn_dt_bias, gdn_norm_g,
              m2_conv_w, m2_conv_b, m2_a_log, m2_dt_bias, m2_d, m2_norm_g,
              ffn_w_gate, ffn_w_up, ffn_w_down,
              moe_router, moe_w_gate, moe_w_up, moe_w_down, final_norm_g):
    rows = x.shape[1] // GRID_W
    sc = jax.nn.silu(c)
    sx = jax.nn.silu(c_ctx)
    x_lat, x_ctx = x, ctx
    for i in range(DEPTH):
        last = i == DEPTH - 1
        col_major = i % 2 == 1
        ml = jnp.split((sc @ ada_w[i] + ada_b[i])[:, None, :], 6, axis=-1)
        mc = jnp.split(sx @ ada_w[i] + ada_b[i], 6, axis=-1)
        s5_p = (s5_lam_re[i], s5_lam_im[i], s5_b_re[i], s5_b_im[i], s5_c_re[i], s5_c_im[i],
                s5_log_dt[i], s5_d[i], s5_w_glu[i])
        gdn_p = (gdn_conv_w[i], gdn_a_log[i], gdn_dt_bias[i], gdn_norm_g[i])
        m2_p = (m2_conv_w[i], m2_conv_b[i], m2_a_log[i], m2_dt_bias[i], m2_d[i], m2_norm_g[i])

        h_lat = modulate(x_lat, norm1_g[i], ml[0], ml[1])
        h_ctx = modulate(x_ctx, norm1_g[i], mc[0], mc[1])
        if col_major:
            h_lat = grid_transpose(h_lat, rows, GRID_W)
        y_ctx, y_lat = token_mixer(h_ctx, h_lat, w_in[i], s5_p, gdn_p, m2_p, not last)
        if col_major:
            y_lat = grid_transpose(y_lat, GRID_W, rows)
        x_lat = x_lat + ml[2] * (y_lat.astype(x_lat.dtype) @ w_out[i])

        j = i // 2
        if i % 2 == 0:
            ffn = lambda h: swiglu(h, ffn_w_gate[j], ffn_w_up[j], ffn_w_down[j])
        else:
            ffn = lambda h: moe_swiglu(h, moe_router[j], moe_w_gate[j], moe_w_up[j], moe_w_down[j])
        x_lat = x_lat + ml[5] * ffn(modulate(x_lat, norm2_g[i], ml[3], ml[4]))
        if not last:
            x_ctx = x_ctx + mc[2] * (y_ctx.astype(x_ctx.dtype) @ w_out[i])
            x_ctx = x_ctx + mc[5] * ffn(modulate(x_ctx, norm2_g[i], mc[3], mc[4]))
    return rmsnorm(x_lat, final_norm_g)
```

```python
import functools
import math

import jax
import jax.numpy as jnp
from jax import lax
from jax.experimental import pallas as pl
from jax.experimental.pallas import tpu as pltpu

F32 = jnp.float32
BF16 = jnp.bfloat16

D_MODEL = 1024
GRID_W = 64
EPS = 1e-6
CHUNK = 64
CONV_W = 5
S5_WIDTH = D_MODEL // 2
S5_GROUP = 16
S5_GROUPS = S5_WIDTH // S5_GROUP
S5_STATE = 64
GDN_DK = 128
GDN_DV = 128
GDN_WIDTH = D_MODEL // 2
GDN_HEADS = GDN_WIDTH // GDN_DV
GDN_CONV_CH = 2 * GDN_HEADS * GDN_DK + GDN_WIDTH
M2_WIDTH = D_MODEL
M2_HEADDIM = 64
M2_HEADS = M2_WIDTH // M2_HEADDIM
M2_GROUPS = 2
M2_HPG = M2_HEADS // M2_GROUPS
M2_STATE = 128
M2_CONV_CH = M2_WIDTH + 2 * M2_GROUPS * M2_STATE
MIX_WIDTH = S5_WIDTH + GDN_WIDTH + M2_WIDTH
IN_SIZES = (S5_WIDTH, GDN_CONV_CH, GDN_WIDTH, 2 * GDN_HEADS, 2 * GDN_HEADS,
            M2_WIDTH, M2_CONV_CH, 2 * M2_HEADS)
D_IN = sum(IN_SIZES)
D_FF = 256 * ((8 * D_MODEL // 3 + 255) // 256)
N_EXPERTS = 8
TOP_K = 2

LANES = 128
SUBLANES = 8
VMEM_LIMIT = 56 * 1024 * 1024

P_U = 0
P_GZ = P_U + S5_WIDTH
P_QKV = P_GZ + GDN_WIDTH
P_MZ = P_QKV + GDN_CONV_CH
P_XBC = P_MZ + M2_WIDTH
P_SMALL = P_XBC + M2_CONV_CH
IN_TN = 896
NP = 6 * IN_TN
assert P_SMALL + LANES <= NP

FF_TF = 256
FF_NF = D_FF // FF_TF
MOE_TM = 512


def _cparams(sem, vmem=VMEM_LIMIT):
    return pltpu.CompilerParams(dimension_semantics=sem, vmem_limit_bytes=vmem)


def _silu(x):
    return x * jax.nn.sigmoid(x)


def _rms(x):
    return x * lax.rsqrt(jnp.mean(x * x, axis=-1, keepdims=True) + EPS)


def _mod_vec(mod_ref, k, is_ctx):
    return jnp.where(is_ctx, mod_ref[0, k:k + 1, :], mod_ref[1, k:k + 1, :])


def _ada_kernel(c_ref, w_ref, b_ref, o_ref):
    s = _silu(c_ref[...])
    o_ref[...] = jnp.dot(s.astype(BF16), w_ref[...].astype(BF16),
                         preferred_element_type=F32) + b_ref[...]


def _ada_all(cpad, ada_w, ada_b):
    depth, d, n6 = ada_w.shape
    tn = 1536
    rows = cpad.shape[0]
    return pl.pallas_call(
        _ada_kernel,
        grid=(depth, n6 // tn),
        in_specs=[pl.BlockSpec((rows, d), lambda i, j: (0, 0)),
                  pl.BlockSpec((None, d, tn), lambda i, j: (i, 0, j)),
                  pl.BlockSpec((None, 1, tn), lambda i, j: (i, 0, j))],
        out_specs=pl.BlockSpec((None, rows, tn), lambda i, j: (i, 0, j)),
        out_shape=jax.ShapeDtypeStruct((depth, rows, n6), F32),
        compiler_params=_cparams(("parallel", "parallel")),
        name="ada_mod",
    )(cpad, ada_w, ada_b.reshape(depth, 1, n6))


def _in_proj_kernel(x_ref, mod_ref, g_ref, w_ref, o_ref, h_scr, *, lc, tm, rc):
    r = pl.program_id(1)

    @pl.when(pl.program_id(2) == 0)
    def _():
        def body(c, carry):
            r0 = pl.multiple_of(c * rc, rc)
            x = x_ref[pl.ds(r0, rc), :]
            row = r * tm + r0 + lax.broadcasted_iota(jnp.int32, (rc, 1), 0)
            is_ctx = row < lc
            h = (_rms(x) * g_ref[...]) * (1.0 + _mod_vec(mod_ref, 1, is_ctx)) + _mod_vec(mod_ref, 0, is_ctx)
            h_scr[pl.ds(r0, rc), :] = h.astype(BF16)
            return carry
        lax.fori_loop(0, tm // rc, body, 0)

    o_ref[...] = jnp.dot(h_scr[...], w_ref[...], preferred_element_type=F32)


def _in_proj(x, mod, g, w_packed, lc):
    b, lt, d = x.shape
    tm = lt // 2
    rc = math.gcd(tm, 128)
    assert lc <= tm and tm % 16 == 0
    kern = functools.partial(_in_proj_kernel, lc=lc, tm=tm, rc=rc)
    return pl.pallas_call(
        kern,
        grid=(b, 2, NP // IN_TN),
        in_specs=[pl.BlockSpec((None, tm, d), lambda i, r, j: (i, r, 0)),
                  pl.BlockSpec((None, 2, 6, d), lambda i, r, j: (i, 0, 0, 0)),
                  pl.BlockSpec((1, d), lambda i, r, j: (0, 0)),
                  pl.BlockSpec((d, IN_TN), lambda i, r, j: (0, j))],
        out_specs=pl.BlockSpec((None, tm, IN_TN), lambda i, r, j: (i, r, j)),
        out_shape=jax.ShapeDtypeStruct((b, lt, NP), F32),
        scratch_shapes=[pltpu.VMEM((tm, d), BF16)],
        compiler_params=_cparams(("parallel", "parallel", "arbitrary")),
        name="in_proj",
    )(x, mod, g.reshape(1, d), w_packed)


def _pack_w_in(w):
    o = [0]
    for s in IN_SIZES:
        o.append(o[-1] + s)
    u, qkv, gz, al, be, mz, xbc, dt = [w[:, o[i]:o[i + 1]] for i in range(8)]
    small_pad = jnp.zeros((w.shape[0], LANES - 2 * 2 * GDN_HEADS - 2 * M2_HEADS), w.dtype)
    tail = jnp.zeros((w.shape[0], NP - P_SMALL - LANES), w.dtype)
    return jnp.concatenate([u, gz, qkv, mz, xbc, al, be, dt, small_pad, tail], axis=1).astype(BF16)


def _out_proj_kernel(a_ref, b_ref, m_ref, x_ref, mod_ref, w_ref, o_ref, *, lc, tm):
    r = pl.program_id(1)
    acc = jnp.dot(a_ref[...].astype(BF16), w_ref[0:S5_WIDTH, :], preferred_element_type=F32)
    acc += jnp.dot(b_ref[...].astype(BF16), w_ref[S5_WIDTH:S5_WIDTH + GDN_WIDTH, :],
                   preferred_element_type=F32)
    acc += jnp.dot(m_ref[...].astype(BF16), w_ref[S5_WIDTH + GDN_WIDTH:, :],
                   preferred_element_type=F32)
    row = r * tm + lax.broadcasted_iota(jnp.int32, (tm, 1), 0)
    o_ref[...] = x_ref[...] + _mod_vec(mod_ref, 2, row < lc) * acc


def _out_proj(a, bm, m, x, mod, w_out_bf, lc):
    b, lt, d = x.shape
    tm = lt // 4 if (lt // 4) % 8 == 0 and lc <= lt // 4 else lt // 2
    kern = functools.partial(_out_proj_kernel, lc=lc, tm=tm)
    tok = lambda w: pl.BlockSpec((None, tm, w), lambda i, r: (i, r, 0))
    return pl.pallas_call(
        kern,
        grid=(b, lt // tm),
        in_specs=[tok(S5_WIDTH), tok(GDN_WIDTH), tok(M2_WIDTH), tok(d),
                  pl.BlockSpec((None, 2, 6, d), lambda i, r: (i, 0, 0, 0)),
                  pl.BlockSpec((MIX_WIDTH, d), lambda i, r: (0, 0))],
        out_specs=tok(d),
        out_shape=jax.ShapeDtypeStruct((b, lt, d), F32),
        compiler_params=_cparams(("parallel", "parallel")),
        name="out_proj",
    )(a, bm, m, x, mod, w_out_bf)


def _swiglu_acc(h, wg_ref, wu_ref, wd_ref):
    def body(f, acc):
        g = jnp.dot(h, wg_ref[f], preferred_element_type=F32)
        u = jnp.dot(h, wu_ref[f], preferred_element_type=F32)
        a = (_silu(g) * u).astype(BF16)
        return acc + jnp.dot(a, wd_ref[f], preferred_element_type=F32)
    return lax.fori_loop(0, FF_NF, body, jnp.zeros((h.shape[0], D_MODEL), F32))


def _ffn_dense_kernel(x_ref, mod_ref, g_ref, wg_ref, wu_ref, wd_ref, o_ref, *, lc, tm):
    r = pl.program_id(1)
    x = x_ref[...]
    row = r * tm + lax.broadcasted_iota(jnp.int32, (tm, 1), 0)
    is_ctx = row < lc
    h = (_rms(x) * g_ref[...]) * (1.0 + _mod_vec(mod_ref, 4, is_ctx)) + _mod_vec(mod_ref, 3, is_ctx)
    acc = _swiglu_acc(h.astype(BF16), wg_ref, wu_ref, wd_ref)
    o_ref[...] = x + _mod_vec(mod_ref, 5, is_ctx) * acc


def _split_ff(wg, wu, wd):
    lead = wg.shape[:-2]
    nl = len(lead)
    perm = tuple(range(nl)) + (nl + 1, nl, nl + 2)
    wg = wg.astype(BF16).reshape(*lead, D_MODEL, FF_NF, FF_TF).transpose(perm)
    wu = wu.astype(BF16).reshape(*lead, D_MODEL, FF_NF, FF_TF).transpose(perm)
    wd = wd.astype(BF16).reshape(*lead, FF_NF, FF_TF, D_MODEL)
    return wg, wu, wd


def _tok_tile(lc, l):
    return math.gcd(math.gcd(lc, l), 256)


def _ffn_dense(x, mod, g, wg, wu, wd, lc):
    b, lt, d = x.shape
    tm = _tok_tile(lc, lt - lc)
    kern = functools.partial(_ffn_dense_kernel, lc=lc, tm=tm)
    tok = pl.BlockSpec((None, tm, d), lambda i, r: (i, r, 0))
    wspec = lambda shp: pl.BlockSpec(shp, lambda i, r: (0, 0, 0))
    return pl.pallas_call(
        kern,
        grid=(b, lt // tm),
        in_specs=[tok, pl.BlockSpec((None, 2, 6, d), lambda i, r: (i, 0, 0, 0)),
                  pl.BlockSpec((1, d), lambda i, r: (0, 0)),
                  wspec((FF_NF, d, FF_TF)), wspec((FF_NF, d, FF_TF)), wspec((FF_NF, FF_TF, d))],
        out_specs=tok,
        out_shape=jax.ShapeDtypeStruct((b, lt, d), F32),
        compiler_params=_cparams(("parallel", "parallel")),
        name="ffn_dense",
    )(x, mod, g.reshape(1, d), wg, wu, wd)


def _router_kernel(x_ref, mod_ref, g_ref, rw_ref, h_ref, rt_ref, *, lc, tm):
    r = pl.program_id(1)
    x = x_ref[...]
    row = r * tm + lax.broadcasted_iota(jnp.int32, (tm, 1), 0)
    is_ctx = row < lc
    h = (_rms(x) * g_ref[...]) * (1.0 + _mod_vec(mod_ref, 4, is_ctx)) + _mod_vec(mod_ref, 3, is_ctx)
    h_ref[...] = h
    logits = jnp.dot(h, rw_ref[...], preferred_element_type=F32, precision=lax.Precision.HIGHEST)
    lane = lax.broadcasted_iota(jnp.int32, logits.shape, 1)
    neg = jnp.float32(-jnp.inf)
    lg = jnp.where(lane < N_EXPERTS, logits, neg)
    m1 = jnp.max(lg, axis=-1, keepdims=True)
    i1 = jnp.min(jnp.where(lg == m1, lane, LANES), axis=-1, keepdims=True)
    lg2 = jnp.where(lane == i1, neg, lg)
    m2 = jnp.max(lg2, axis=-1, keepdims=True)
    i2 = jnp.min(jnp.where(lg2 == m2, lane, LANES), axis=-1, keepdims=True)
    e2 = jnp.exp(m2 - m1)
    den = 1.0 + e2
    w1 = 1.0 / den
    w2 = e2 / den
    out = jnp.where(lane == 0, i1.astype(F32), 0.0)
    out = jnp.where(lane == 1, i2.astype(F32), out)
    out = jnp.where(lane == 2, w1, out)
    out = jnp.where(lane == 3, w2, out)
    rt_ref[...] = out


def _router(x, mod, g, router_w, lc):
    b, lt, d = x.shape
    tm = _tok_tile(lc, lt - lc)
    rw = jnp.concatenate([router_w, jnp.zeros((d, LANES - N_EXPERTS), F32)], axis=1)
    kern = functools.partial(_router_kernel, lc=lc, tm=tm)
    tok = lambda w: pl.BlockSpec((None, tm, w), lambda i, r: (i, r, 0))
    return pl.pallas_call(
        kern,
        grid=(b, lt // tm),
        in_specs=[tok(d), pl.BlockSpec((None, 2, 6, d), lambda i, r: (i, 0, 0, 0)),
                  pl.BlockSpec((1, d), lambda i, r: (0, 0)),
                  pl.BlockSpec((d, LANES), lambda i, r: (0, 0))],
        out_specs=[tok(d), tok(LANES)],
        out_shape=[jax.ShapeDtypeStruct((b, lt, d), F32), jax.ShapeDtypeStruct((b, lt, LANES), F32)],
        compiler_params=_cparams(("parallel", "parallel")),
        name="moe_router",
    )(x, mod, g.reshape(1, d), rw)


def _row_copy(src_hbm, dst, idx, r, sem):
    return pltpu.make_async_copy(src_hbm.at[pl.ds(idx, 1), :], dst.at[pl.ds(r, 1), :], sem)


def _gather_rows_kernel(idx_ref, src_hbm, o_ref, sem, *, tg):
    def start(r, c):
        _row_copy(src_hbm, o_ref, idx_ref[r], r, sem).start()
        return c
    lax.fori_loop(0, tg, start, 0)

    def wait(r, c):
        _row_copy(src_hbm, o_ref, 0, r, sem).wait()
        return c
    lax.fori_loop(0, tg, wait, 0)


def _gather_rows(src, idx, tg):
    n_out = idx.shape[0]
    d = src.shape[1]
    kern = functools.partial(_gather_rows_kernel, tg=tg)
    return pl.pallas_call(
        kern,
        grid=(n_out // tg,),
        in_specs=[pl.BlockSpec((tg,), lambda i: (i,), memory_space=pltpu.SMEM),
                  pl.BlockSpec(memory_space=pl.ANY)],
        out_specs=pl.BlockSpec((tg, d), lambda i: (i, 0)),
        out_shape=jax.ShapeDtypeStruct((n_out, d), F32),
        scratch_shapes=[pltpu.SemaphoreType.DMA(())],
        compiler_params=_cparams(("arbitrary",)),
        name="moe_gather",
    )(idx, src)


def _expert_kernel(te_ref, nt_ref, x_ref, wg_ref, wu_ref, wd_ref, o_ref):
    @pl.when(pl.program_id(0) < nt_ref[0])
    def _():
        o_ref[...] = _swiglu_acc(x_ref[...].astype(BF16), wg_ref, wu_ref, wd_ref)

    @pl.when(pl.program_id(0) >= nt_ref[0])
    def _():
        o_ref[...] = jnp.zeros_like(o_ref)


def _experts(xs, tile_expert, n_tiles_used, wg, wu, wd):
    rp, d = xs.shape
    wspec = lambda shp: pl.BlockSpec((None,) + shp, lambda t, te, nt: (te[t], 0, 0, 0))
    grid_spec = pltpu.PrefetchScalarGridSpec(
        num_scalar_prefetch=2,
        grid=(rp // MOE_TM,),
        in_specs=[pl.BlockSpec((MOE_TM, d), lambda t, te, nt: (t, 0)),
                  wspec((FF_NF, d, FF_TF)), wspec((FF_NF, d, FF_TF)), wspec((FF_NF, FF_TF, d))],
        out_specs=pl.BlockSpec((MOE_TM, d), lambda t, te, nt: (t, 0)),
    )
    return pl.pallas_call(
        _expert_kernel,
        grid_spec=grid_spec,
        out_shape=jax.ShapeDtypeStruct((rp, d), F32),
        compiler_params=_cparams(("arbitrary",)),
        name="moe_experts",
    )(tile_expert, n_tiles_used, xs, wg, wu, wd)


def _combine_kernel(p1_ref, p2_ref, y_hbm, x_ref, rt_ref, mod_ref, o_ref, ya, yb, sem, *, lc, tm):
    r = pl.program_id(1)

    def start(k, c):
        _row_copy(y_hbm, ya, p1_ref[k], k, sem).start()
        _row_copy(y_hbm, yb, p2_ref[k], k, sem).start()
        return c
    lax.fori_loop(0, tm, start, 0)

    def wait(k, c):
        _row_copy(y_hbm, ya, 0, k, sem).wait()
        _row_copy(y_hbm, yb, 0, k, sem).wait()
        return c
    lax.fori_loop(0, tm, wait, 0)

    row = r * tm + lax.broadcasted_iota(jnp.int32, (tm, 1), 0)
    rt = rt_ref[...]
    y = rt[:, 2:3] * ya[...] + rt[:, 3:4] * yb[...]
    o_ref[...] = x_ref[...] + _mod_vec(mod_ref, 5, row < lc) * y


def _combine(p1, p2, y, x, rt, mod, lc):
    b, lt, d = x.shape
    tm = _tok_tile(lc, lt - lc)
    nt = lt // tm
    kern = functools.partial(_combine_kernel, lc=lc, tm=tm)
    tok = lambda w: pl.BlockSpec((None, tm, w), lambda i, r: (i, r, 0))
    ispec = pl.BlockSpec((tm,), lambda i, r: (i * nt + r,), memory_space=pltpu.SMEM)
    return pl.pallas_call(
        kern,
        grid=(b, nt),
        in_specs=[ispec, ispec, pl.BlockSpec(memory_space=pl.ANY), tok(d), tok(LANES),
                  pl.BlockSpec((None, 2, 6, d), lambda i, r: (i, 0, 0, 0))],
        out_specs=tok(d),
        out_shape=jax.ShapeDtypeStruct((b, lt, d), F32),
        scratch_shapes=[pltpu.VMEM((tm, d), F32), pltpu.VMEM((tm, d), F32),
                        pltpu.SemaphoreType.DMA(())],
        compiler_params=_cparams(("arbitrary", "arbitrary")),
        name="moe_combine",
    )(p1, p2, y, x, rt, mod)


def _moe(x, mod, g, router_w, wg, wu, wd, lc):
    b, lt, d = x.shape
    n = b * lt
    h2, rt = _router(x, mod, g, router_w, lc)
    rt2 = rt.reshape(n, LANES)
    e_flat = jnp.concatenate([rt2[:, 0], rt2[:, 1]]).astype(jnp.int32)
    onehot = (e_flat[:, None] == jnp.arange(N_EXPERTS, dtype=jnp.int32)[None, :]).astype(jnp.int32)
    counts = jnp.sum(onehot, axis=0)
    rank = jnp.sum((jnp.cumsum(onehot, axis=0) - 1) * onehot, axis=1)
    padded = ((counts + MOE_TM - 1) // MOE_TM) * MOE_TM
    ends = jnp.cumsum(padded)
    starts = ends - padded
    pos = starts[e_flat] + rank
    rp = ((2 * n + MOE_TM - 1) // MOE_TM + N_EXPERTS) * MOE_TM
    tok_id = jnp.concatenate([jnp.arange(n, dtype=jnp.int32)] * 2)
    src_row = jnp.zeros((rp,), jnp.int32).at[pos].set(tok_id)
    n_tiles = rp // MOE_TM
    tile_start = jnp.arange(n_tiles, dtype=jnp.int32) * MOE_TM
    tile_expert = jnp.minimum(jnp.sum(tile_start[:, None] >= ends[None, :], axis=1),
                              N_EXPERTS - 1).astype(jnp.int32)
    n_used = (ends[-1] // MOE_TM).astype(jnp.int32).reshape(1)
    xs = _gather_rows(h2.reshape(n, d), src_row, MOE_TM)
    ys = _experts(xs, tile_expert, n_used, wg, wu, wd)
    return _combine(pos[:n].astype(jnp.int32), pos[n:].astype(jnp.int32), ys, x, rt, mod, lc)


def _final_norm_kernel(x_ref, g_ref, o_ref):
    o_ref[...] = _rms(x_ref[...]) * g_ref[...]


def _final_norm(x, g):
    b, l, d = x.shape
    tm = math.gcd(l, 512)
    tok = pl.BlockSpec((None, tm, d), lambda i, r: (i, r, 0))
    return pl.pallas_call(
        _final_norm_kernel,
        grid=(b, l // tm),
        in_specs=[tok, pl.BlockSpec((1, d), lambda i, r: (0, 0))],
        out_specs=tok,
        out_shape=jax.ShapeDtypeStruct((b, l, d), F32),
        compiler_params=_cparams(("parallel", "parallel")),
        name="final_norm",
    )(x, g.reshape(1, d))


def _t_rmsnorm(x, g):
    return (x * lax.rsqrt(jnp.mean(x * x, axis=-1, keepdims=True) + EPS)) * g


def _t_l2norm(t):
    return t * lax.rsqrt(jnp.sum(t * t, axis=-1, keepdims=True) + EPS)


def _t_rev(t, flag):
    return jnp.flip(t, axis=1) if flag else t


def _t_dwconv(u, w):
    pad = CONV_W // 2
    return lax.conv_general_dilated(
        u, w[:, None, :].astype(u.dtype), window_strides=(1,), padding=[(pad, pad)],
        dimension_numbers=("NWC", "WIO", "NWC"), feature_group_count=u.shape[-1])


def _t_seg_decay(cs):
    t = cs.shape[-1]
    mask = jnp.tril(jnp.ones((t, t), dtype=bool))
    diff = cs[..., :, None] - cs[..., None, :]
    return jnp.where(mask, jnp.exp(jnp.where(mask, diff, 0.0)), 0.0)


def _t_affine(left, right):
    a_l, b_l = left
    a_r, b_r = right
    return a_l * a_r, a_r * b_l + b_r


def _t_linear_scan(a, b, h0, reverse):
    a_cum, h = lax.associative_scan(_t_affine, (a, b), axis=1, reverse=reverse)
    if h0 is not None:
        h = h + a_cum * h0[:, None]
    return h


def _t_s5_direction(ug, lam_re, lam_im, b_re, b_im, log_dt, h0, reverse):
    lam = lax.complex(jnp.minimum(lam_re, -1e-4), lam_im)
    dt = jnp.exp(log_dt)[:, None]
    lam_bar = jnp.exp(lam * dt)
    gamma = (lam_bar - 1.0) / lam
    bu = lax.complex(jnp.einsum("blgi,gpi->blgp", ug, b_re),
                     jnp.einsum("blgi,gpi->blgp", ug, b_im)) * gamma
    return _t_linear_scan(jnp.broadcast_to(lam_bar, bu.shape), bu, h0, reverse)


def _t_s5_readout(states, c_re, c_im):
    return (jnp.einsum("blgp,gip->blgi", states.real, c_re)
            - jnp.einsum("blgp,gip->blgi", states.imag, c_im))


def _t_s5_mixer(u_ctx, u_lat, lam_re, lam_im, b_re, b_im, c_re, c_im, log_dt, d_skip, w_glu):
    uc = u_ctx.reshape(*u_ctx.shape[:2], S5_GROUPS, S5_GROUP)
    ul = u_lat.reshape(*u_lat.shape[:2], S5_GROUPS, S5_GROUP)
    y_ctx, y_lat = [], []
    for d in range(2):
        reverse = d == 1
        p = (lam_re[d], lam_im[d], b_re[d], b_im[d], log_dt[d])
        s_ctx = _t_s5_direction(uc, *p, None, reverse)
        h_end = s_ctx[:, 0] if reverse else s_ctx[:, -1]
        s_lat = _t_s5_direction(ul, *p, h_end, reverse)
        y_lat.append(_t_s5_readout(s_lat, c_re[d], c_im[d]))
        y_ctx.append(_t_s5_readout(s_ctx, c_re[d], c_im[d]))

    def finish(ys, u):
        y = (ys[0] + ys[1]).reshape(u.shape) + d_skip * u
        y = jax.nn.gelu(y)
        return y * jax.nn.sigmoid(y @ w_glu)

    return finish(y_ctx, u_ctx), finish(y_lat, u_lat)


def _t_gdn_prep(qkv, a_raw, b_raw, conv_w, a_log, dt_bias):
    qkv = jax.nn.silu(_t_dwconv(qkv, conv_w))
    bsz, l, _ = qkv.shape
    q = qkv[..., :GDN_HEADS * GDN_DK]
    k = qkv[..., GDN_HEADS * GDN_DK:2 * GDN_HEADS * GDN_DK]
    v = qkv[..., 2 * GDN_HEADS * GDN_DK:]
    q = _t_l2norm(q.reshape(bsz, l, GDN_HEADS, GDN_DK))
    k = _t_l2norm(k.reshape(bsz, l, GDN_HEADS, GDN_DK))
    v = v.reshape(bsz, l, GDN_HEADS, GDN_DV)
    g = -jnp.exp(a_log) * jax.nn.softplus(a_raw.reshape(bsz, l, 2, GDN_HEADS) + dt_bias)
    beta = jax.nn.sigmoid(b_raw.reshape(bsz, l, 2, GDN_HEADS))
    return q, k, v, g, beta


def _t_gdn_chunked(q, k, v, g, beta, s0):
    b, l, h, dk = q.shape
    n = l // CHUNK

    def blk(t):
        return jnp.moveaxis(t.reshape(b, n, CHUNK, h, *t.shape[3:]), 3, 1)

    q, k, v, beta = blk(q) * dk ** -0.5, blk(k), blk(v), blk(beta)
    gc = jnp.cumsum(blk(g), axis=-1)
    decay = _t_seg_decay(gc)
    strict = jnp.tril(jnp.ones((CHUNK, CHUNK), dtype=bool), -1)
    kb = k * beta[..., None]
    lower = jnp.where(strict, jnp.einsum("bhnid,bhnjd->bhnij", kb, k) * decay, 0.0)
    rhs = jnp.concatenate([v * beta[..., None], kb * jnp.exp(gc)[..., None]], axis=-1)
    sol = lax.linalg.triangular_solve(jnp.eye(CHUNK, dtype=F32) + lower, rhs,
                                      left_side=True, lower=True, unit_diagonal=True)
    dv = v.shape[-1]
    u_c, w_c = sol[..., :dv], sol[..., dv:]
    k_dec = k * jnp.exp(gc[..., -1:] - gc)[..., None]
    g_last = jnp.exp(gc[..., -1])
    xs = [u_c, w_c, k_dec, g_last, q * jnp.exp(gc)[..., None],
          jnp.einsum("bhnid,bhnjd->bhnij", q, k) * decay]

    def step(s, inp):
        u_i, w_i, kd_i, gl_i, qd_i, a_i = inp
        v_new = u_i - jnp.einsum("bhcd,bhde->bhce", w_i, s)
        s_next = s * gl_i[..., None, None] + jnp.einsum("bhcd,bhce->bhde", kd_i, v_new)
        o = jnp.einsum("bhcd,bhde->bhce", qd_i, s) + jnp.einsum("bhij,bhje->bhie", a_i, v_new)
        return s_next, o

    s_fin, o = lax.scan(step, s0, [jnp.moveaxis(t, 2, 0) for t in xs])
    o = jnp.moveaxis(jnp.moveaxis(o, 0, 2), 1, 3).reshape(b, l, h, dv)
    return o, s_fin


def _t_gdn_direction(p, d, s_init):
    q, k, v, g, beta = p
    r = d == 1
    o, s_fin = _t_gdn_chunked(_t_rev(q, r), _t_rev(k, r), _t_rev(v, r), _t_rev(g[:, :, d], r),
                              _t_rev(beta[:, :, d], r), s_init)
    return _t_rev(o, r), s_fin


def _t_gdn_mixer(ctx_in, lat_in, conv_w, a_log, dt_bias, norm_g):
    pc = _t_gdn_prep(ctx_in[0], ctx_in[2], ctx_in[3], conv_w, a_log, dt_bias)
    pl_ = _t_gdn_prep(lat_in[0], lat_in[2], lat_in[3], conv_w, a_log, dt_bias)
    s0 = jnp.zeros((ctx_in[0].shape[0], GDN_HEADS, GDN_DK, GDN_DV), F32)
    o_ctx, o_lat = [], []
    for d in range(2):
        oc, s_ctx = _t_gdn_direction(pc, d, s0)
        ol, _ = _t_gdn_direction(pl_, d, s_ctx)
        o_lat.append(ol)
        o_ctx.append(oc)

    def finish(os, z):
        o = _t_rmsnorm(os[0] + os[1], norm_g)
        return (o * jax.nn.silu(z).reshape(o.shape)).reshape(z.shape)

    return finish(o_ctx, ctx_in[1]), finish(o_lat, lat_in[1])


def _t_m2_prep(xbc, dt_raw, conv_w, conv_b, dt_bias):
    xbc = jax.nn.silu(_t_dwconv(xbc, conv_w) + conv_b)
    bsz, l, _ = xbc.shape
    xs = xbc[..., :M2_WIDTH]
    bm = xbc[..., M2_WIDTH:M2_WIDTH + M2_GROUPS * M2_STATE]
    cm = xbc[..., M2_WIDTH + M2_GROUPS * M2_STATE:]
    xs = xs.reshape(bsz, l, M2_GROUPS, M2_HPG, M2_HEADDIM)
    bm = bm.reshape(bsz, l, M2_GROUPS, M2_STATE)
    cm = cm.reshape(bsz, l, M2_GROUPS, M2_STATE)
    dt = jax.nn.softplus(dt_raw.reshape(bsz, l, 2, M2_GROUPS, M2_HPG)
                         + dt_bias.reshape(2, M2_GROUPS, M2_HPG))
    return xs, bm, cm, dt


def _t_ssd_chunked(xdt, log_a, bm, cm, h0):
    bsz, l = xdt.shape[:2]
    n = l // CHUNK
    xc = xdt.reshape(bsz, n, CHUNK, *xdt.shape[2:])
    bc = bm.reshape(bsz, n, CHUNK, *bm.shape[2:])
    cc = cm.reshape(bsz, n, CHUNK, *cm.shape[2:])
    a_cs = jnp.cumsum(jnp.moveaxis(log_a.reshape(bsz, n, CHUNK, M2_GROUPS, M2_HPG), (3, 4), (1, 2)),
                      axis=-1)
    decay_to_end = jnp.exp(a_cs[..., -1:] - a_cs)
    chunk_states = jnp.einsum("bncge,bgjnc,bncgjp->bngjpe", bc, decay_to_end, xc)
    if h0 is None:
        h0 = jnp.zeros_like(chunk_states[:, 0])
    chunk_states = jnp.concatenate([h0[:, None], chunk_states], axis=1)
    chunk_cs = jnp.cumsum(jnp.pad(a_cs[..., -1], [(0, 0)] * 3 + [(1, 0)]), axis=-1)
    states = jnp.einsum("bgjzy,bygjpe->bzgjpe", _t_seg_decay(chunk_cs), chunk_states)
    y_diag = jnp.einsum("bncge,bnsge,bgjncs,bnsgjp->bncgjp", cc, bc, _t_seg_decay(a_cs), xc)
    y_off = jnp.einsum("bncge,bngjpe,bgjnc->bncgjp", cc, states[:, :-1], jnp.exp(a_cs))
    return (y_diag + y_off).reshape(xdt.shape), states[:, -1]


def _t_m2_direction(p, a, d, h0):
    xs, bm, cm, dt = p
    r = d == 1
    dtd = dt[:, :, d]
    y, h = _t_ssd_chunked(_t_rev(xs * dtd[..., None], r), _t_rev(dtd * a[d], r), _t_rev(bm, r),
                          _t_rev(cm, r), h0)
    return _t_rev(y, r), h


def _t_m2_mixer(ctx_in, lat_in, conv_w, conv_b, a_log, dt_bias, d_skip, norm_g):
    pc = _t_m2_prep(ctx_in[1], ctx_in[2], conv_w, conv_b, dt_bias)
    pl_ = _t_m2_prep(lat_in[1], lat_in[2], conv_w, conv_b, dt_bias)
    a = -jnp.exp(a_log).reshape(2, M2_GROUPS, M2_HPG)
    y_ctx, y_lat = [], []
    for d in range(2):
        yc, h_ctx = _t_m2_direction(pc, a, d, None)
        yl, _ = _t_m2_direction(pl_, a, d, h_ctx)
        y_lat.append(yl)
        y_ctx.append(yc)
    d_skip = d_skip.reshape(M2_GROUPS, M2_HPG, 1)

    def finish(ys, xs, z):
        bsz, l = z.shape[:2]
        y = (ys[0] + ys[1] + d_skip * xs).reshape(bsz, l, M2_GROUPS, -1)
        y = y * jax.nn.silu(z).reshape(bsz, l, M2_GROUPS, -1)
        return _t_rmsnorm(y, norm_g.reshape(M2_GROUPS, -1)).reshape(bsz, l, M2_WIDTH)

    return finish(y_ctx, pc[0], ctx_in[0]), finish(y_lat, pl_[0], lat_in[0])


def _t_token_mixer(p, lc, s5_p, gdn_p, m2_p):
    def seg(t):
        return dict(u=t[..., P_U:P_U + S5_WIDTH], gz=t[..., P_GZ:P_GZ + GDN_WIDTH],
                    qkv=t[..., P_QKV:P_QKV + GDN_CONV_CH], mz=t[..., P_MZ:P_MZ + M2_WIDTH],
                    xbc=t[..., P_XBC:P_XBC + M2_CONV_CH],
                    al=t[..., P_SMALL:P_SMALL + 8], be=t[..., P_SMALL + 8:P_SMALL + 16],
                    dt=t[..., P_SMALL + 16:P_SMALL + 48])
    c, l = seg(p[:, :lc]), seg(p[:, lc:])
    a_c, a_l = _t_s5_mixer(c["u"], l["u"], *s5_p)
    b_c, b_l = _t_gdn_mixer((c["qkv"], c["gz"], c["al"], c["be"]), (l["qkv"], l["gz"], l["al"], l["be"]),
                            *gdn_p)
    m_c, m_l = _t_m2_mixer((c["mz"], c["xbc"], c["dt"]), (l["mz"], l["xbc"], l["dt"]), *m2_p)
    cat = lambda x, y: jnp.concatenate([x, y], axis=1)
    return cat(a_c, a_l), cat(b_c, b_l), cat(m_c, m_l)


def _grid_t(xl, rows, cols):
    b, l, ch = xl.shape
    return xl.reshape(b, rows, cols, ch).transpose(0, 2, 1, 3).reshape(b, l, ch)


def kernel(x, c, ctx, c_ctx, ada_w, ada_b, norm1_g, norm2_g, w_in, w_out,
           s5_lam_re, s5_lam_im, s5_b_re, s5_b_im, s5_c_re, s5_c_im, s5_log_dt, s5_d, s5_w_glu,
           gdn_conv_w, gdn_a_log, gdn_dt_bias, gdn_norm_g,
           m2_conv_w, m2_conv_b, m2_a_log, m2_dt_bias, m2_d, m2_norm_g,
           ffn_w_gate, ffn_w_up, ffn_w_down,
           moe_router, moe_w_gate, moe_w_up, moe_w_down, final_norm_g):
    b, l, d = x.shape
    lc = ctx.shape[1]
    depth = ada_w.shape[0]
    rows = l // GRID_W

    cpad = jnp.zeros((2 * SUBLANES, d), F32).at[:b].set(c).at[b].set(c_ctx)
    mods = _ada_all(cpad, ada_w, ada_b).reshape(depth, 2 * SUBLANES, 6, d)

    xt = jnp.concatenate([ctx, x], axis=1)
    col_major = False
    for i in range(depth):
        want_cm = i % 2 == 1
        if want_cm != col_major:
            lat = xt[:, lc:]
            lat = _grid_t(lat, rows, GRID_W) if want_cm else _grid_t(lat, GRID_W, rows)
            xt = jnp.concatenate([xt[:, :lc], lat], axis=1)
            col_major = want_cm
        mod = jnp.stack([jnp.broadcast_to(mods[i, b][None], (b, 6, d)), mods[i, :b]], axis=1)

        p = _in_proj(xt, mod, norm1_g[i], _pack_w_in(w_in[i]), lc)
        s5_p = (s5_lam_re[i], s5_lam_im[i], s5_b_re[i], s5_b_im[i], s5_c_re[i], s5_c_im[i],
                s5_log_dt[i], s5_d[i], s5_w_glu[i])
        gdn_p = (gdn_conv_w[i], gdn_a_log[i], gdn_dt_bias[i], gdn_norm_g[i])
        m2_p = (m2_conv_w[i], m2_conv_b[i], m2_a_log[i], m2_dt_bias[i], m2_d[i], m2_norm_g[i])
        ya, yb, ym = _t_token_mixer(p, lc, s5_p, gdn_p, m2_p)
        xt = _out_proj(ya, yb, ym, xt, mod, w_out[i].astype(BF16), lc)

        j = i // 2
        if i % 2 == 0:
            wg, wu, wd = _split_ff(ffn_w_gate[j], ffn_w_up[j], ffn_w_down[j])
            xt = _ffn_dense(xt, mod, norm2_g[i], wg, wu, wd, lc)
        else:
            wg, wu, wd = _split_ff(moe_w_gate[j], moe_w_up[j], moe_w_down[j])
            xt = _moe(xt, mod, norm2_g[i], moe_router[j], wg, wu, wd, lc)

    lat = xt[:, lc:]
    if col_major:
        lat = _grid_t(lat, GRID_W, rows)
    return _final_norm(lat, final_norm_g)
```

```python
import functools
import math

import jax
import jax.numpy as jnp
from jax import lax
from jax.experimental import pallas as pl
from jax.experimental.pallas import tpu as pltpu

F32 = jnp.float32
BF16 = jnp.bfloat16

D_MODEL = 1024
GRID_W = 64
EPS = 1e-6
CHUNK = 64
CONV_W = 5
S5_WIDTH = D_MODEL // 2
S5_GROUP = 16
S5_GROUPS = S5_WIDTH // S5_GROUP
S5_STATE = 64
GDN_DK = 128
GDN_DV = 128
GDN_WIDTH = D_MODEL // 2
GDN_HEADS = GDN_WIDTH // GDN_DV
GDN_CONV_CH = 2 * GDN_HEADS * GDN_DK + GDN_WIDTH
M2_WIDTH = D_MODEL
M2_HEADDIM = 64
M2_HEADS = M2_WIDTH // M2_HEADDIM
M2_GROUPS = 2
M2_HPG = M2_HEADS // M2_GROUPS
M2_STATE = 128
M2_CONV_CH = M2_WIDTH + 2 * M2_GROUPS * M2_STATE
MIX_WIDTH = S5_WIDTH + GDN_WIDTH + M2_WIDTH
IN_SIZES = (S5_WIDTH, GDN_CONV_CH, GDN_WIDTH, 2 * GDN_HEADS, 2 * GDN_HEADS,
            M2_WIDTH, M2_CONV_CH, 2 * M2_HEADS)
D_IN = sum(IN_SIZES)
D_FF = 256 * ((8 * D_MODEL // 3 + 255) // 256)
N_EXPERTS = 8
TOP_K = 2

LANES = 128
SUBLANES = 8
VMEM_LIMIT = 56 * 1024 * 1024

P_U = 0
P_GZ = P_U + S5_WIDTH
P_QKV = P_GZ + GDN_WIDTH
P_MZ = P_QKV + GDN_CONV_CH
P_XBC = P_MZ + M2_WIDTH
P_SMALL = P_XBC + M2_CONV_CH
IN_TN = 896
NP = 6 * IN_TN
assert P_SMALL + LANES <= NP

FF_TF = 256
FF_NF = D_FF // FF_TF
MOE_TM = 512


def _cparams(sem, vmem=VMEM_LIMIT):
    return pltpu.CompilerParams(dimension_semantics=sem, vmem_limit_bytes=vmem)


def _silu(x):
    return x * jax.nn.sigmoid(x)


def _rms(x):
    return x * lax.rsqrt(jnp.mean(x * x, axis=-1, keepdims=True) + EPS)


def _mod_vec(mod_ref, k, is_ctx):
    return jnp.where(is_ctx, mod_ref[0, k:k + 1, :], mod_ref[1, k:k + 1, :])


def _ada_kernel(c_ref, w_ref, b_ref, o_ref):
    s = _silu(c_ref[...])
    o_ref[...] = jnp.dot(s.astype(BF16), w_ref[...].astype(BF16),
                         preferred_element_type=F32) + b_ref[...]


def _ada_all(cpad, ada_w, ada_b):
    depth, d, n6 = ada_w.shape
    tn = 1536
    rows = cpad.shape[0]
    return pl.pallas_call(
        _ada_kernel,
        grid=(depth, n6 // tn),
        in_specs=[pl.BlockSpec((rows, d), lambda i, j: (0, 0)),
                  pl.BlockSpec((None, d, tn), lambda i, j: (i, 0, j)),
                  pl.BlockSpec((None, 1, tn), lambda i, j: (i, 0, j))],
        out_specs=pl.BlockSpec((None, rows, tn), lambda i, j: (i, 0, j)),
        out_shape=jax.ShapeDtypeStruct((depth, rows, n6), F32),
        compiler_params=_cparams(("parallel", "parallel")),
        name="ada_mod",
    )(cpad, ada_w, ada_b.reshape(depth, 1, n6))


def _in_proj_kernel(x_ref, mod_ref, g_ref, w_ref, o_ref, h_scr, *, lc, tm, rc):
    r = pl.program_id(1)

    @pl.when(pl.program_id(2) == 0)
    def _():
        def body(c, carry):
            r0 = pl.multiple_of(c * rc, rc)
            x = x_ref[pl.ds(r0, rc), :]
            row = r * tm + r0 + lax.broadcasted_iota(jnp.int32, (rc, 1), 0)
            is_ctx = row < lc
            h = (_rms(x) * g_ref[...]) * (1.0 + _mod_vec(mod_ref, 1, is_ctx)) + _mod_vec(mod_ref, 0, is_ctx)
            h_scr[pl.ds(r0, rc), :] = h.astype(BF16)
            return carry
        lax.fori_loop(0, tm // rc, body, 0)

    o_ref[...] = jnp.dot(h_scr[...], w_ref[...], preferred_element_type=F32)


def _in_proj(x, mod, g, w_packed, lc):
    b, lt, d = x.shape
    tm = lt // 2
    rc = math.gcd(tm, 128)
    assert lc <= tm and tm % 16 == 0
    kern = functools.partial(_in_proj_kernel, lc=lc, tm=tm, rc=rc)
    return pl.pallas_call(
        kern,
        grid=(b, 2, NP // IN_TN),
        in_specs=[pl.BlockSpec((None, tm, d), lambda i, r, j: (i, r, 0)),
                  pl.BlockSpec((None, 2, 6, d), lambda i, r, j: (i, 0, 0, 0)),
                  pl.BlockSpec((1, d), lambda i, r, j: (0, 0)),
                  pl.BlockSpec((d, IN_TN), lambda i, r, j: (0, j))],
        out_specs=pl.BlockSpec((None, tm, IN_TN), lambda i, r, j: (i, r, j)),
        out_shape=jax.ShapeDtypeStruct((b, lt, NP), F32),
        scratch_shapes=[pltpu.VMEM((tm, d), BF16)],
        compiler_params=_cparams(("parallel", "parallel", "arbitrary")),
        name="in_proj",
    )(x, mod, g.reshape(1, d), w_packed)


def _pack_w_in(w):
    o = [0]
    for s in IN_SIZES:
        o.append(o[-1] + s)
    u, qkv, gz, al, be, mz, xbc, dt = [w[:, o[i]:o[i + 1]] for i in range(8)]
    small_pad = jnp.zeros((w.shape[0], LANES - 2 * 2 * GDN_HEADS - 2 * M2_HEADS), w.dtype)
    tail = jnp.zeros((w.shape[0], NP - P_SMALL - LANES), w.dtype)
    return jnp.concatenate([u, gz, qkv, mz, xbc, al, be, dt, small_pad, tail], axis=1).astype(BF16)


def _out_proj_kernel(a_ref, b_ref, m_ref, x_ref, mod_ref, w_ref, o_ref, *, lc, tm):
    r = pl.program_id(1)
    acc = jnp.dot(a_ref[...].astype(BF16), w_ref[0:S5_WIDTH, :], preferred_element_type=F32)
    acc += jnp.dot(b_ref[...].astype(BF16), w_ref[S5_WIDTH:S5_WIDTH + GDN_WIDTH, :],
                   preferred_element_type=F32)
    acc += jnp.dot(m_ref[...].astype(BF16), w_ref[S5_WIDTH + GDN_WIDTH:, :],
                   preferred_element_type=F32)
    row = r * tm + lax.broadcasted_iota(jnp.int32, (tm, 1), 0)
    o_ref[...] = x_ref[...] + _mod_vec(mod_ref, 2, row < lc) * acc


def _out_proj(a, bm, m, x, mod, w_out_bf, lc):
    b, lt, d = x.shape
    tm = lt // 4 if (lt // 4) % 8 == 0 and lc <= lt // 4 else lt // 2
    kern = functools.partial(_out_proj_kernel, lc=lc, tm=tm)
    tok = lambda w: pl.BlockSpec((None, tm, w), lambda i, r: (i, r, 0))
    return pl.pallas_call(
        kern,
        grid=(b, lt // tm),
        in_specs=[tok(S5_WIDTH), tok(GDN_WIDTH), tok(M2_WIDTH), tok(d),
                  pl.BlockSpec((None, 2, 6, d), lambda i, r: (i, 0, 0, 0)),
                  pl.BlockSpec((MIX_WIDTH, d), lambda i, r: (0, 0))],
        out_specs=tok(d),
        out_shape=jax.ShapeDtypeStruct((b, lt, d), F32),
        compiler_params=_cparams(("parallel", "parallel")),
        name="out_proj",
    )(a, bm, m, x, mod, w_out_bf)


def _swiglu_acc(h, wg_ref, wu_ref, wd_ref):
    def body(f, acc):
        g = jnp.dot(h, wg_ref[f], preferred_element_type=F32)
        u = jnp.dot(h, wu_ref[f], preferred_element_type=F32)
        a = (_silu(g) * u).astype(BF16)
        return acc + jnp.dot(a, wd_ref[f], preferred_element_type=F32)
    return lax.fori_loop(0, FF_NF, body, jnp.zeros((h.shape[0], D_MODEL), F32))


def _ffn_dense_kernel(x_ref, mod_ref, g_ref, wg_ref, wu_ref, wd_ref, o_ref, *, lc, tm):
    r = pl.program_id(1)
    x = x_ref[...]
    row = r * tm + lax.broadcasted_iota(jnp.int32, (tm, 1), 0)
    is_ctx = row < lc
    h = (_rms(x) * g_ref[...]) * (1.0 + _mod_vec(mod_ref, 4, is_ctx)) + _mod_vec(mod_ref, 3, is_ctx)
    acc = _swiglu_acc(h.astype(BF16), wg_ref, wu_ref, wd_ref)
    o_ref[...] = x + _mod_vec(mod_ref, 5, is_ctx) * acc


def _split_ff(wg, wu, wd):
    lead = wg.shape[:-2]
    nl = len(lead)
    perm = tuple(range(nl)) + (nl + 1, nl, nl + 2)
    wg = wg.astype(BF16).reshape(*lead, D_MODEL, FF_NF, FF_TF).transpose(perm)
    wu = wu.astype(BF16).reshape(*lead, D_MODEL, FF_NF, FF_TF).transpose(perm)
    wd = wd.astype(BF16).reshape(*lead, FF_NF, FF_TF, D_MODEL)
    return wg, wu, wd


def _tok_tile(lc, l):
    return math.gcd(math.gcd(lc, l), 256)


def _ffn_dense(x, mod, g, wg, wu, wd, lc):
    b, lt, d = x.shape
    tm = _tok_tile(lc, lt - lc)
    kern = functools.partial(_ffn_dense_kernel, lc=lc, tm=tm)
    tok = pl.BlockSpec((None, tm, d), lambda i, r: (i, r, 0))
    wspec = lambda shp: pl.BlockSpec(shp, lambda i, r: (0, 0, 0))
    return pl.pallas_call(
        kern,
        grid=(b, lt // tm),
        in_specs=[tok, pl.BlockSpec((None, 2, 6, d), lambda i, r: (i, 0, 0, 0)),
                  pl.BlockSpec((1, d), lambda i, r: (0, 0)),
                  wspec((FF_NF, d, FF_TF)), wspec((FF_NF, d, FF_TF)), wspec((FF_NF, FF_TF, d))],
        out_specs=tok,
        out_shape=jax.ShapeDtypeStruct((b, lt, d), F32),
        compiler_params=_cparams(("parallel", "parallel")),
        name="ffn_dense",
    )(x, mod, g.reshape(1, d), wg, wu, wd)


def _router_kernel(x_ref, mod_ref, g_ref, rw_ref, h_ref, rt_ref, *, lc, tm):
    r = pl.program_id(1)
    x = x_ref[...]
    row = r * tm + lax.broadcasted_iota(jnp.int32, (tm, 1), 0)
    is_ctx = row < lc
    h = (_rms(x) * g_ref[...]) * (1.0 + _mod_vec(mod_ref, 4, is_ctx)) + _mod_vec(mod_ref, 3, is_ctx)
    h_ref[...] = h
    logits = jnp.dot(h, rw_ref[...], preferred_element_type=F32, precision=lax.Precision.HIGHEST)
    lane = lax.broadcasted_iota(jnp.int32, logits.shape, 1)
    neg = jnp.float32(-jnp.inf)
    lg = jnp.where(lane < N_EXPERTS, logits, neg)
    m1 = jnp.max(lg, axis=-1, keepdims=True)
    i1 = jnp.min(jnp.where(lg == m1, lane, LANES), axis=-1, keepdims=True)
    lg2 = jnp.where(lane == i1, neg, lg)
    m2 = jnp.max(lg2, axis=-1, keepdims=True)
    i2 = jnp.min(jnp.where(lg2 == m2, lane, LANES), axis=-1, keepdims=True)
    e2 = jnp.exp(m2 - m1)
    den = 1.0 + e2
    w1 = 1.0 / den
    w2 = e2 / den
    out = jnp.where(lane == 0, i1.astype(F32), 0.0)
    out = jnp.where(lane == 1, i2.astype(F32), out)
    out = jnp.where(lane == 2, w1, out)
    out = jnp.where(lane == 3, w2, out)
    rt_ref[...] = out


def _router(x, mod, g, router_w, lc):
    b, lt, d = x.shape
    tm = _tok_tile(lc, lt - lc)
    rw = jnp.concatenate([router_w, jnp.zeros((d, LANES - N_EXPERTS), F32)], axis=1)
    kern = functools.partial(_router_kernel, lc=lc, tm=tm)
    tok = lambda w: pl.BlockSpec((None, tm, w), lambda i, r: (i, r, 0))
    return pl.pallas_call(
        kern,
        grid=(b, lt // tm),
        in_specs=[tok(d), pl.BlockSpec((None, 2, 6, d), lambda i, r: (i, 0, 0, 0)),
                  pl.BlockSpec((1, d), lambda i, r: (0, 0)),
                  pl.BlockSpec((d, LANES), lambda i, r: (0, 0))],
        out_specs=[tok(d), tok(LANES)],
        out_shape=[jax.ShapeDtypeStruct((b, lt, d), F32), jax.ShapeDtypeStruct((b, lt, LANES), F32)],
        compiler_params=_cparams(("parallel", "parallel")),
        name="moe_router",
    )(x, mod, g.reshape(1, d), rw)


def _row_copy(src_hbm, dst, idx, r, sem):
    return pltpu.make_async_copy(src_hbm.at[pl.ds(idx, 1), :], dst.at[pl.ds(r, 1), :], sem)


def _gather_rows_kernel(idx_ref, src_hbm, o_ref, sem, *, tg):
    def start(r, c):
        _row_copy(src_hbm, o_ref, idx_ref[r], r, sem).start()
        return c
    lax.fori_loop(0, tg, start, 0)

    def wait(r, c):
        _row_copy(src_hbm, o_ref, 0, r, sem).wait()
        return c
    lax.fori_loop(0, tg, wait, 0)


def _gather_rows(src, idx, tg):
    n_out = idx.shape[0]
    d = src.shape[1]
    kern = functools.partial(_gather_rows_kernel, tg=tg)
    return pl.pallas_call(
        kern,
        grid=(n_out // tg,),
        in_specs=[pl.BlockSpec((tg,), lambda i: (i,), memory_space=pltpu.SMEM),
                  pl.BlockSpec(memory_space=pl.ANY)],
        out_specs=pl.BlockSpec((tg, d), lambda i: (i, 0)),
        out_shape=jax.ShapeDtypeStruct((n_out, d), F32),
        scratch_shapes=[pltpu.SemaphoreType.DMA(())],
        compiler_params=_cparams(("arbitrary",)),
        name="moe_gather",
    )(idx, src)


def _expert_kernel(te_ref, nt_ref, x_ref, wg_ref, wu_ref, wd_ref, o_ref):
    @pl.when(pl.program_id(0) < nt_ref[0])
    def _():
        o_ref[...] = _swiglu_acc(x_ref[...].astype(BF16), wg_ref, wu_ref, wd_ref)

    @pl.when(pl.program_id(0) >= nt_ref[0])
    def _():
        o_ref[...] = jnp.zeros_like(o_ref)


def _experts(xs, tile_expert, n_tiles_used, wg, wu, wd):
    rp, d = xs.shape
    wspec = lambda shp: pl.BlockSpec((None,) + shp, lambda t, te, nt: (te[t], 0, 0, 0))
    grid_spec = pltpu.PrefetchScalarGridSpec(
        num_scalar_prefetch=2,
        grid=(rp // MOE_TM,),
        in_specs=[pl.BlockSpec((MOE_TM, d), lambda t, te, nt: (t, 0)),
                  wspec((FF_NF, d, FF_TF)), wspec((FF_NF, d, FF_TF)), wspec((FF_NF, FF_TF, d))],
        out_specs=pl.BlockSpec((MOE_TM, d), lambda t, te, nt: (t, 0)),
    )
    return pl.pallas_call(
        _expert_kernel,
        grid_spec=grid_spec,
        out_shape=jax.ShapeDtypeStruct((rp, d), F32),
        compiler_params=_cparams(("arbitrary",)),
        name="moe_experts",
    )(tile_expert, n_tiles_used, xs, wg, wu, wd)


def _combine_kernel(p1_ref, p2_ref, y_hbm, x_ref, rt_ref, mod_ref, o_ref, ya, yb, sem, *, lc, tm):
    r = pl.program_id(1)

    def start(k, c):
        _row_copy(y_hbm, ya, p1_ref[k], k, sem).start()
        _row_copy(y_hbm, yb, p2_ref[k], k, sem).start()
        return c
    lax.fori_loop(0, tm, start, 0)

    def wait(k, c):
        _row_copy(y_hbm, ya, 0, k, sem).wait()
        _row_copy(y_hbm, yb, 0, k, sem).wait()
        return c
    lax.fori_loop(0, tm, wait, 0)

    row = r * tm + lax.broadcasted_iota(jnp.int32, (tm, 1), 0)
    rt = rt_ref[...]
    y = rt[:, 2:3] * ya[...] + rt[:, 3:4] * yb[...]
    o_ref[...] = x_ref[...] + _mod_vec(mod_ref, 5, row < lc) * y


def _combine(p1, p2, y, x, rt, mod, lc):
    b, lt, d = x.shape
    tm = _tok_tile(lc, lt - lc)
    nt = lt // tm
    kern = functools.partial(_combine_kernel, lc=lc, tm=tm)
    tok = lambda w: pl.BlockSpec((None, tm, w), lambda i, r: (i, r, 0))
    ispec = pl.BlockSpec((tm,), lambda i, r: (i * nt + r,), memory_space=pltpu.SMEM)
    return pl.pallas_call(
        kern,
        grid=(b, nt),
        in_specs=[ispec, ispec, pl.BlockSpec(memory_space=pl.ANY), tok(d), tok(LANES),
                  pl.BlockSpec((None, 2, 6, d), lambda i, r: (i, 0, 0, 0))],
        out_specs=tok(d),
        out_shape=jax.ShapeDtypeStruct((b, lt, d), F32),
        scratch_shapes=[pltpu.VMEM((tm, d), F32), pltpu.VMEM((tm, d), F32),
                        pltpu.SemaphoreType.DMA(())],
        compiler_params=_cparams(("arbitrary", "arbitrary")),
        name="moe_combine",
    )(p1, p2, y, x, rt, mod)


def _moe(x, mod, g, router_w, wg, wu, wd, lc):
    b, lt, d = x.shape
    n = b * lt
    h2, rt = _router(x, mod, g, router_w, lc)
    rt2 = rt.reshape(n, LANES)
    e_flat = jnp.concatenate([rt2[:, 0], rt2[:, 1]]).astype(jnp.int32)
    onehot = (e_flat[:, None] == jnp.arange(N_EXPERTS, dtype=jnp.int32)[None, :]).astype(jnp.int32)
    counts = jnp.sum(onehot, axis=0)
    rank = jnp.sum((jnp.cumsum(onehot, axis=0) - 1) * onehot, axis=1)
    padded = ((counts + MOE_TM - 1) // MOE_TM) * MOE_TM
    ends = jnp.cumsum(padded)
    starts = ends - padded
    pos = starts[e_flat] + rank
    rp = ((2 * n + MOE_TM - 1) // MOE_TM + N_EXPERTS) * MOE_TM
    tok_id = jnp.concatenate([jnp.arange(n, dtype=jnp.int32)] * 2)
    src_row = jnp.zeros((rp,), jnp.int32).at[pos].set(tok_id)
    n_tiles = rp // MOE_TM
    tile_start = jnp.arange(n_tiles, dtype=jnp.int32) * MOE_TM
    tile_expert = jnp.minimum(jnp.sum(tile_start[:, None] >= ends[None, :], axis=1),
                              N_EXPERTS - 1).astype(jnp.int32)
    n_used = (ends[-1] // MOE_TM).astype(jnp.int32).reshape(1)
    xs = _gather_rows(h2.reshape(n, d), src_row, MOE_TM)
    ys = _experts(xs, tile_expert, n_used, wg, wu, wd)
    return _combine(pos[:n].astype(jnp.int32), pos[n:].astype(jnp.int32), ys, x, rt, mod, lc)


def _final_norm_kernel(x_ref, g_ref, o_ref):
    o_ref[...] = _rms(x_ref[...]) * g_ref[...]


def _final_norm(x, g):
    b, l, d = x.shape
    tm = math.gcd(l, 512)
    tok = pl.BlockSpec((None, tm, d), lambda i, r: (i, r, 0))
    return pl.pallas_call(
        _final_norm_kernel,
        grid=(b, l // tm),
        in_specs=[tok, pl.BlockSpec((1, d), lambda i, r: (0, 0))],
        out_specs=tok,
        out_shape=jax.ShapeDtypeStruct((b, l, d), F32),
        compiler_params=_cparams(("parallel", "parallel")),
        name="final_norm",
    )(x, g.reshape(1, d))


def _t_rmsnorm(x, g):
    return (x * lax.rsqrt(jnp.mean(x * x, axis=-1, keepdims=True) + EPS)) * g


def _t_l2norm(t):
    return t * lax.rsqrt(jnp.sum(t * t, axis=-1, keepdims=True) + EPS)


def _t_rev(t, flag):
    return jnp.flip(t, axis=1) if flag else t


def _t_dwconv(u, w):
    pad = CONV_W // 2
    return lax.conv_general_dilated(
        u, w[:, None, :].astype(u.dtype), window_strides=(1,), padding=[(pad, pad)],
        dimension_numbers=("NWC", "WIO", "NWC"), feature_group_count=u.shape[-1])


def _t_seg_decay(cs):
    t = cs.shape[-1]
    mask = jnp.tril(jnp.ones((t, t), dtype=bool))
    diff = cs[..., :, None] - cs[..., None, :]
    return jnp.where(mask, jnp.exp(jnp.where(mask, diff, 0.0)), 0.0)


def _t_affine(left, right):
    a_l, b_l = left
    a_r, b_r = right
    return a_l * a_r, a_r * b_l + b_r


def _t_linear_scan(a, b, h0, reverse):
    a_cum, h = lax.associative_scan(_t_affine, (a, b), axis=1, reverse=reverse)
    if h0 is not None:
        h = h + a_cum * h0[:, None]
    return h


def _t_s5_direction(ug, lam_re, lam_im, b_re, b_im, log_dt, h0, reverse):
    lam = lax.complex(jnp.minimum(lam_re, -1e-4), lam_im)
    dt = jnp.exp(log_dt)[:, None]
    lam_bar = jnp.exp(lam * dt)
    gamma = (lam_bar - 1.0) / lam
    bu = lax.complex(jnp.einsum("blgi,gpi->blgp", ug, b_re),
                     jnp.einsum("blgi,gpi->blgp", ug, b_im)) * gamma
    return _t_linear_scan(jnp.broadcast_to(lam_bar, bu.shape), bu, h0, reverse)


def _t_s5_readout(states, c_re, c_im):
    return (jnp.einsum("blgp,gip->blgi", states.real, c_re)
            - jnp.einsum("blgp,gip->blgi", states.imag, c_im))


def _t_s5_mixer(u_ctx, u_lat, lam_re, lam_im, b_re, b_im, c_re, c_im, log_dt, d_skip, w_glu):
    uc = u_ctx.reshape(*u_ctx.shape[:2], S5_GROUPS, S5_GROUP)
    ul = u_lat.reshape(*u_lat.shape[:2], S5_GROUPS, S5_GROUP)
    y_ctx, y_lat = [], []
    for d in range(2):
        reverse = d == 1
        p = (lam_re[d], lam_im[d], b_re[d], b_im[d], log_dt[d])
        s_ctx = _t_s5_direction(uc, *p, None, reverse)
        h_end = s_ctx[:, 0] if reverse else s_ctx[:, -1]
        s_lat = _t_s5_direction(ul, *p, h_end, reverse)
        y_lat.append(_t_s5_readout(s_lat, c_re[d], c_im[d]))
        y_ctx.append(_t_s5_readout(s_ctx, c_re[d], c_im[d]))

    def finish(ys, u):
        y = (ys[0] + ys[1]).reshape(u.shape) + d_skip * u
        y = jax.nn.gelu(y)
        return y * jax.nn.sigmoid(y @ w_glu)

    return finish(y_ctx, u_ctx), finish(y_lat, u_lat)


def _t_gdn_prep(qkv, a_raw, b_raw, conv_w, a_log, dt_bias):
    qkv = jax.nn.silu(_t_dwconv(qkv, conv_w))
    bsz, l, _ = qkv.shape
    q = qkv[..., :GDN_HEADS * GDN_DK]
    k = qkv[..., GDN_HEADS * GDN_DK:2 * GDN_HEADS * GDN_DK]
    v = qkv[..., 2 * GDN_HEADS * GDN_DK:]
    q = _t_l2norm(q.reshape(bsz, l, GDN_HEADS, GDN_DK))
    k = _t_l2norm(k.reshape(bsz, l, GDN_HEADS, GDN_DK))
    v = v.reshape(bsz, l, GDN_HEADS, GDN_DV)
    g = -jnp.exp(a_log) * jax.nn.softplus(a_raw.reshape(bsz, l, 2, GDN_HEADS) + dt_bias)
    beta = jax.nn.sigmoid(b_raw.reshape(bsz, l, 2, GDN_HEADS))
    return q, k, v, g, beta


def _t_gdn_chunked(q, k, v, g, beta, s0):
    b, l, h, dk = q.shape
    n = l // CHUNK

    def blk(t):
        return jnp.moveaxis(t.reshape(b, n, CHUNK, h, *t.shape[3:]), 3, 1)

    q, k, v, beta = blk(q) * dk ** -0.5, blk(k), blk(v), blk(beta)
    gc = jnp.cumsum(blk(g), axis=-1)
    decay = _t_seg_decay(gc)
    strict = jnp.tril(jnp.ones((CHUNK, CHUNK), dtype=bool), -1)
    kb = k * beta[..., None]
    lower = jnp.where(strict, jnp.einsum("bhnid,bhnjd->bhnij", kb, k) * decay, 0.0)
    rhs = jnp.concatenate([v * beta[..., None], kb * jnp.exp(gc)[..., None]], axis=-1)
    sol = lax.linalg.triangular_solve(jnp.eye(CHUNK, dtype=F32) + lower, rhs,
                                      left_side=True, lower=True, unit_diagonal=True)
    dv = v.shape[-1]
    u_c, w_c = sol[..., :dv], sol[..., dv:]
    k_dec = k * jnp.exp(gc[..., -1:] - gc)[..., None]
    g_last = jnp.exp(gc[..., -1])
    xs = [u_c, w_c, k_dec, g_last, q * jnp.exp(gc)[..., None],
          jnp.einsum("bhnid,bhnjd->bhnij", q, k) * decay]

    def step(s, inp):
        u_i, w_i, kd_i, gl_i, qd_i, a_i = inp
        v_new = u_i - jnp.einsum("bhcd,bhde->bhce", w_i, s)
        s_next = s * gl_i[..., None, None] + jnp.einsum("bhcd,bhce->bhde", kd_i, v_new)
        o = jnp.einsum("bhcd,bhde->bhce", qd_i, s) + jnp.einsum("bhij,bhje->bhie", a_i, v_new)
        return s_next, o

    s_fin, o = lax.scan(step, s0, [jnp.moveaxis(t, 2, 0) for t in xs])
    o = jnp.moveaxis(jnp.moveaxis(o, 0, 2), 1, 3).reshape(b, l, h, dv)
    return o, s_fin


def _t_gdn_direction(p, d, s_init):
    q, k, v, g, beta = p
    r = d == 1
    o, s_fin = _t_gdn_chunked(_t_rev(q, r), _t_rev(k, r), _t_rev(v, r), _t_rev(g[:, :, d], r),
                              _t_rev(beta[:, :, d], r), s_init)
    return _t_rev(o, r), s_fin


def _t_gdn_mixer(ctx_in, lat_in, conv_w, a_log, dt_bias, norm_g):
    pc = _t_gdn_prep(ctx_in[0], ctx_in[2], ctx_in[3], conv_w, a_log, dt_bias)
    pl_ = _t_gdn_prep(lat_in[0], lat_in[2], lat_in[3], conv_w, a_log, dt_bias)
    s0 = jnp.zeros((ctx_in[0].shape[0], GDN_HEADS, GDN_DK, GDN_DV), F32)
    o_ctx, o_lat = [], []
    for d in range(2):
        oc, s_ctx = _t_gdn_direction(pc, d, s0)
        ol, _ = _t_gdn_direction(pl_, d, s_ctx)
        o_lat.append(ol)
        o_ctx.append(oc)

    def finish(os, z):
        o = _t_rmsnorm(os[0] + os[1], norm_g)
        return (o * jax.nn.silu(z).reshape(o.shape)).reshape(z.shape)

    return finish(o_ctx, ctx_in[1]), finish(o_lat, lat_in[1])


def _t_m2_prep(xbc, dt_raw, conv_w, conv_b, dt_bias):
    xbc = jax.nn.silu(_t_dwconv(xbc, conv_w) + conv_b)
    bsz, l, _ = xbc.shape
    xs = xbc[..., :M2_WIDTH]
    bm = xbc[..., M2_WIDTH:M2_WIDTH + M2_GROUPS * M2_STATE]
    cm = xbc[..., M2_WIDTH + M2_GROUPS * M2_STATE:]
    xs = xs.reshape(bsz, l, M2_GROUPS, M2_HPG, M2_HEADDIM)
    bm = bm.reshape(bsz, l, M2_GROUPS, M2_STATE)
    cm = cm.reshape(bsz, l, M2_GROUPS, M2_STATE)
    dt = jax.nn.softplus(dt_raw.reshape(bsz, l, 2, M2_GROUPS, M2_HPG)
                         + dt_bias.reshape(2, M2_GROUPS, M2_HPG))
    return xs, bm, cm, dt


def _t_ssd_chunked(xdt, log_a, bm, cm, h0):
    bsz, l = xdt.shape[:2]
    n = l // CHUNK
    xc = xdt.reshape(bsz, n, CHUNK, *xdt.shape[2:])
    bc = bm.reshape(bsz, n, CHUNK, *bm.shape[2:])
    cc = cm.reshape(bsz, n, CHUNK, *cm.shape[2:])
    a_cs = jnp.cumsum(jnp.moveaxis(log_a.reshape(bsz, n, CHUNK, M2_GROUPS, M2_HPG), (3, 4), (1, 2)),
                      axis=-1)
    decay_to_end = jnp.exp(a_cs[..., -1:] - a_cs)
    chunk_states = jnp.einsum("bncge,bgjnc,bncgjp->bngjpe", bc, decay_to_end, xc)
    if h0 is None:
        h0 = jnp.zeros_like(chunk_states[:, 0])
    chunk_states = jnp.concatenate([h0[:, None], chunk_states], axis=1)
    chunk_cs = jnp.cumsum(jnp.pad(a_cs[..., -1], [(0, 0)] * 3 + [(1, 0)]), axis=-1)
    states = jnp.einsum("bgjzy,bygjpe->bzgjpe", _t_seg_decay(chunk_cs), chunk_states)
    y_diag = jnp.einsum("bncge,bnsge,bgjncs,bnsgjp->bncgjp", cc, bc, _t_seg_decay(a_cs), xc)
    y_off = jnp.einsum("bncge,bngjpe,bgjnc->bncgjp", cc, states[:, :-1], jnp.exp(a_cs))
    return (y_diag + y_off).reshape(xdt.shape), states[:, -1]


def _t_m2_direction(p, a, d, h0):
    xs, bm, cm, dt = p
    r = d == 1
    dtd = dt[:, :, d]
    y, h = _t_ssd_chunked(_t_rev(xs * dtd[..., None], r), _t_rev(dtd * a[d], r), _t_rev(bm, r),
                          _t_rev(cm, r), h0)
    return _t_rev(y, r), h


def _t_m2_mixer(ctx_in, lat_in, conv_w, conv_b, a_log, dt_bias, d_skip, norm_g):
    pc = _t_m2_prep(ctx_in[1], ctx_in[2], conv_w, conv_b, dt_bias)
    pl_ = _t_m2_prep(lat_in[1], lat_in[2], conv_w, conv_b, dt_bias)
    a = -jnp.exp(a_log).reshape(2, M2_GROUPS, M2_HPG)
    y_ctx, y_lat = [], []
    for d in range(2):
        yc, h_ctx = _t_m2_direction(pc, a, d, None)
        yl, _ = _t_m2_direction(pl_, a, d, h_ctx)
        y_lat.append(yl)
        y_ctx.append(yc)
    d_skip = d_skip.reshape(M2_GROUPS, M2_HPG, 1)

    def finish(ys, xs, z):
        bsz, l = z.shape[:2]
        y = (ys[0] + ys[1] + d_skip * xs).reshape(bsz, l, M2_GROUPS, -1)
        y = y * jax.nn.silu(z).reshape(bsz, l, M2_GROUPS, -1)
        return _t_rmsnorm(y, norm_g.reshape(M2_GROUPS, -1)).reshape(bsz, l, M2_WIDTH)

    return finish(y_ctx, pc[0], ctx_in[0]), finish(y_lat, pl_[0], lat_in[0])


def _t_token_mixer(p, lc, s5_p, gdn_p, m2_p):
    def seg(t):
        return dict(u=t[..., P_U:P_U + S5_WIDTH], gz=t[..., P_GZ:P_GZ + GDN_WIDTH],
                    qkv=t[..., P_QKV:P_QKV + GDN_CONV_CH], mz=t[..., P_MZ:P_MZ + M2_WIDTH],
                    xbc=t[..., P_XBC:P_XBC + M2_CONV_CH],
                    al=t[..., P_SMALL:P_SMALL + 8], be=t[..., P_SMALL + 8:P_SMALL + 16],
                    dt=t[..., P_SMALL + 16:P_SMALL + 48])
    c, l = seg(p[:, :lc]), seg(p[:, lc:])
    a_c, a_l = _t_s5_mixer(c["u"], l["u"], *s5_p)
    b_c, b_l = _t_gdn_mixer((c["qkv"], c["gz"], c["al"], c["be"]), (l["qkv"], l["gz"], l["al"], l["be"]),
                            *gdn_p)
    m_c, m_l = _t_m2_mixer((c["mz"], c["xbc"], c["dt"]), (l["mz"], l["xbc"], l["dt"]), *m2_p)
    cat = lambda x, y: jnp.concatenate([x, y], axis=1)
    return cat(a_c, a_l), cat(b_c, b_l), cat(m_c, m_l)


def _grid_t(xl, rows, cols):
    b, l, ch = xl.shape
    return xl.reshape(b, rows, cols, ch).transpose(0, 2, 1, 3).reshape(b, l, ch)


def kernel(x, c, ctx, c_ctx, ada_w, ada_b, norm1_g, norm2_g, w_in, w_out,
           s5_lam_re, s5_lam_im, s5_b_re, s5_b_im, s5_c_re, s5_c_im, s5_log_dt, s5_d, s5_w_glu,
           gdn_conv_w, gdn_a_log, gdn_dt_bias, gdn_norm_g,
           m2_conv_w, m2_conv_b, m2_a_log, m2_dt_bias, m2_d, m2_norm_g,
           ffn_w_gate, ffn_w_up, ffn_w_down,
           moe_router, moe_w_gate, moe_w_up, moe_w_down, final_norm_g):
    b, l, d = x.shape
    lc = ctx.shape[1]
    depth = ada_w.shape[0]
    rows = l // GRID_W

    cpad = jnp.zeros((2 * SUBLANES, d), F32).at[:b].set(c).at[b].set(c_ctx)
    mods = _ada_all(cpad, ada_w, ada_b).reshape(depth, 2 * SUBLANES, 6, d)

    xt = jnp.concatenate([ctx, x], axis=1)
    col_major = False
    for i in range(depth):
        want_cm = i % 2 == 1
        if want_cm != col_major:
            lat = xt[:, lc:]
            lat = _grid_t(lat, rows, GRID_W) if want_cm else _grid_t(lat, GRID_W, rows)
            xt = jnp.concatenate([xt[:, :lc], lat], axis=1)
            col_major = want_cm
        mod = jnp.stack([jnp.broadcast_to(mods[i, b][None], (b, 6, d)), mods[i, :b]], axis=1)

        p = _in_proj(xt, mod, norm1_g[i], _pack_w_in(w_in[i]), lc)
        u_tb = jnp.transpose(p[:, :, P_U:P_U + S5_WIDTH], (1, 0, 2)).reshape((lc + l) * b, S5_WIDTH)
        s5_par = _s5_params(s5_lam_re[i], s5_lam_im[i], s5_b_re[i], s5_b_im[i], s5_c_re[i], s5_c_im[i],
                            s5_log_dt[i])
        ya = _s5_mixer(u_tb, lc, b, s5_par, s5_d[i], s5_w_glu[i])
        ya = jnp.transpose(ya.reshape(lc + l, b, S5_WIDTH), (1, 0, 2))
        yb = _gdn_mixer(p, lc, gdn_conv_w[i], gdn_a_log[i], gdn_dt_bias[i], gdn_norm_g[i])
        ym = _ssd_mixer(p, lc, m2_conv_w[i], m2_conv_b[i], m2_a_log[i], m2_dt_bias[i], m2_d[i],
                        m2_norm_g[i])
        xt = _out_proj(ya, yb, ym, xt, mod, w_out[i].astype(BF16), lc)

        j = i // 2
        if i % 2 == 0:
            wg, wu, wd = _split_ff(ffn_w_gate[j], ffn_w_up[j], ffn_w_down[j])
            xt = _ffn_dense(xt, mod, norm2_g[i], wg, wu, wd, lc)
        else:
            wg, wu, wd = _split_ff(moe_w_gate[j], moe_w_up[j], moe_w_down[j])
            xt = _moe(xt, mod, norm2_g[i], moe_router[j], wg, wu, wd, lc)

    lat = xt[:, lc:]
    if col_major:
        lat = _grid_t(lat, GRID_W, rows)
    return _final_norm(lat, final_norm_g)


S5_HALF = S5_WIDTH // 2
S5_HS = (S5_GROUPS // 2) * S5_STATE
S5_LQ = 512


def _s5_scan_kernel(u_ref, bk_ref, ck_ref, a_ref, y_ref, s_scr, h_scr, *, tt, nb):
    d = pl.program_id(0)

    @pl.when(pl.program_id(1) == 0)
    def _():
        h_scr[...] = jnp.zeros_like(h_scr)

    u = u_ref[...].astype(BF16)
    for k in range(2):
        s_scr[:, k * 2 * S5_HS:(k + 1) * 2 * S5_HS] = jnp.dot(
            u[:, k * S5_HALF:(k + 1) * S5_HALF], bk_ref[k], preferred_element_type=F32)

    for k in range(2):
        for q in range(S5_HS // S5_LQ):
            re0 = k * 2 * S5_HS + q * S5_LQ
            im0 = re0 + S5_HS
            a_re = jnp.broadcast_to(a_ref[2 * k:2 * k + 1, q * S5_LQ:(q + 1) * S5_LQ], (nb, S5_LQ))
            a_im = jnp.broadcast_to(a_ref[2 * k + 1:2 * k + 2, q * S5_LQ:(q + 1) * S5_LQ], (nb, S5_LQ))

            def step(i, carry, re0=re0, im0=im0, a_re=a_re, a_im=a_im):
                hr, hi = carry
                t = i + d * (tt - 1 - 2 * i)
                r = pl.multiple_of(t * nb, nb)
                nr = a_re * hr - a_im * hi + s_scr[pl.ds(r, nb), re0:re0 + S5_LQ]
                ni = a_re * hi + a_im * hr + s_scr[pl.ds(r, nb), im0:im0 + S5_LQ]
                s_scr[pl.ds(r, nb), re0:re0 + S5_LQ] = nr
                s_scr[pl.ds(r, nb), im0:im0 + S5_LQ] = ni
                return nr, ni

            hr, hi = lax.fori_loop(0, tt, step, (h_scr[:, re0:re0 + S5_LQ], h_scr[:, im0:im0 + S5_LQ]),
                                   unroll=4)
            h_scr[:, re0:re0 + S5_LQ] = hr
            h_scr[:, im0:im0 + S5_LQ] = hi

    for k in range(2):
        y_ref[:, k * S5_HALF:(k + 1) * S5_HALF] = jnp.dot(
            s_scr[:, k * 2 * S5_HS:(k + 1) * 2 * S5_HS].astype(BF16), ck_ref[k],
            preferred_element_type=F32)


def _s5_fin_kernel(y_ref, u_ref, d_ref, w_ref, o_ref):
    u = u_ref[...]
    y = y_ref[0] + y_ref[1] + d_ref[...] * u
    y = jax.nn.gelu(y)
    o_ref[...] = y * jax.nn.sigmoid(jnp.dot(y.astype(BF16), w_ref[...], preferred_element_type=F32))


def _s5_params(lam_re, lam_im, b_re, b_im, c_re, c_im, log_dt):
    lam = lax.complex(jnp.minimum(lam_re, -1e-4), lam_im)
    dt = jnp.exp(log_dt)[..., None]
    lam_bar = jnp.exp(lam * dt)
    gamma = (lam_bar - 1.0) / lam
    bt = lax.complex(b_re, b_im) * gamma[..., None]
    gh = S5_GROUPS // 2
    eye = jnp.eye(gh, dtype=F32)

    def bd_in(m):
        m = m.reshape(2, 2, gh, S5_STATE, S5_GROUP)
        return jnp.einsum("gh,dkgpi->dkgihp", eye, m).reshape(2, 2, gh * S5_GROUP, gh * S5_STATE)

    def bd_out(m):
        m = m.reshape(2, 2, gh, S5_GROUP, S5_STATE)
        return jnp.einsum("gh,dkgip->dkgphi", eye, m).reshape(2, 2, gh * S5_STATE, gh * S5_GROUP)

    bk = jnp.concatenate([bd_in(bt.real), bd_in(bt.imag)], axis=-1).astype(BF16)
    ck = jnp.concatenate([bd_out(c_re), -bd_out(c_im)], axis=-2).astype(BF16)
    lr = lam_bar.real.reshape(2, 2, 1, S5_HS)
    li = lam_bar.imag.reshape(2, 2, 1, S5_HS)
    a = jnp.concatenate([lr, li], axis=2).reshape(2, 4, S5_HS)
    return bk, ck, a


def _s5_mixer(u_tb, lc, nb, params, d_skip, w_glu):
    n, w = u_tb.shape
    lt = n // nb
    tt = CHUNK
    n_c, n_t = lc // tt, lt // tt
    bk, ck, a = params

    def tile(dd, s):
        bwd = jnp.where(s < n_c, n_c - 1 - s, n_t - 1 - s + n_c)
        return jnp.where(dd == 0, s, bwd)

    kern = functools.partial(_s5_scan_kernel, tt=tt, nb=nb)
    y = pl.pallas_call(
        kern,
        grid=(2, n_t),
        in_specs=[pl.BlockSpec((tt * nb, w), lambda dd, s: (tile(dd, s), 0)),
                  pl.BlockSpec((None, 2, S5_HALF, 2 * S5_HS), lambda dd, s: (dd, 0, 0, 0)),
                  pl.BlockSpec((None, 2, 2 * S5_HS, S5_HALF), lambda dd, s: (dd, 0, 0, 0)),
                  pl.BlockSpec((None, 4, S5_HS), lambda dd, s: (dd, 0, 0))],
        out_specs=pl.BlockSpec((None, tt * nb, w), lambda dd, s: (dd, tile(dd, s), 0)),
        out_shape=jax.ShapeDtypeStruct((2, n, w), F32),
        scratch_shapes=[pltpu.VMEM((tt * nb, 4 * S5_HS), F32), pltpu.VMEM((nb, 4 * S5_HS), F32)],
        compiler_params=_cparams(("arbitrary", "arbitrary")),
        name="s5_scan",
    )(u_tb, bk, ck, a)

    tm = math.gcd(n, 512)
    return pl.pallas_call(
        _s5_fin_kernel,
        grid=(n // tm,),
        in_specs=[pl.BlockSpec((2, tm, w), lambda i: (0, i, 0)),
                  pl.BlockSpec((tm, w), lambda i: (i, 0)),
                  pl.BlockSpec((1, w), lambda i: (0, 0)),
                  pl.BlockSpec((w, w), lambda i: (0, 0))],
        out_specs=pl.BlockSpec((tm, w), lambda i: (i, 0)),
        out_shape=jax.ShapeDtypeStruct((n, w), F32),
        compiler_params=_cparams(("parallel",)),
        name="s5_finish",
    )(y, u_tb, d_skip.reshape(1, w), w_glu.astype(BF16))


def _seq_shift(x, s, row, lc, lt):
    tgt = row + s
    ok = (tgt >= 0) & (tgt < lt) & ((row < lc) == (tgt < lc))
    return jnp.where(ok, pltpu.roll(x, (-s) % lt, 0), 0.0)


def _dwconv_silu(x, w_ref, bias, row, lc, lt):
    pad = CONV_W // 2
    acc = w_ref[pad:pad + 1, :] * x
    for j in range(CONV_W):
        if j != pad:
            acc = acc + w_ref[j:j + 1, :] * _seq_shift(x, j - pad, row, lc, lt)
    if bias is not None:
        acc = acc + bias
    return _silu(acc)


def _chunk_cumsum(x, row, fwd_lanes):
    lt = x.shape[0]
    pos = row & (CHUNK - 1)
    s = 1
    while s < CHUNK:
        dn = jnp.where(pos >= s, pltpu.roll(x, s, 0), 0.0)
        up = jnp.where(pos < CHUNK - s, pltpu.roll(x, lt - s, 0), 0.0)
        x = x + jnp.where(fwd_lanes, dn, up)
        s *= 2
    return x


def _col(arr, lane, idx):
    return jnp.sum(jnp.where(lane == idx, arr, 0.0), axis=-1, keepdims=True)


def _bwd_chunk(s, n_cc, n_c):
    return jnp.where(s < n_cc, n_cc - 1 - s, n_c - 1 - s + n_cc)


_TDOT = (((0,), (0,)), ((), ()))
_NTDOT = (((1,), (1,)), ((), ()))


def _sdot(a, b):
    return jnp.dot(a.astype(BF16), b.astype(BF16), preferred_element_type=F32)


def _gdn_kernel(q_ref, k_ref, v_ref, z_ref, sm_ref, wq_ref, wk_ref, wv_ref, al_ref, dtb_ref, ng_ref,
                o_ref, qs, ks, vs, gcf_s, gcb_s, bf_s, bb_s, pm_s, n_s, qw_s, au_s, gl_s, of_s, ob_s,
                *, lc, lt):
    h = pl.program_id(1)
    n_c, n_cc = lt // CHUNK, lc // CHUNK
    row = lax.broadcasted_iota(jnp.int32, (lt, 1), 0)
    lane = lax.broadcasted_iota(jnp.int32, (1, LANES), 1)

    def l2n(t):
        return t * lax.rsqrt(jnp.sum(t * t, axis=-1, keepdims=True) + EPS)

    qs[...] = l2n(_dwconv_silu(q_ref[...], wq_ref, None, row, lc, lt)) * (GDN_DK ** -0.5)
    ks[...] = l2n(_dwconv_silu(k_ref[...], wk_ref, None, row, lc, lt))
    vs[...] = _dwconv_silu(v_ref[...], wv_ref, None, row, lc, lt)

    sm = sm_ref[...]
    g_all = -jnp.exp(al_ref[...]) * jax.nn.softplus(sm + dtb_ref[...])
    gc_all = _chunk_cumsum(g_all, row, lane < GDN_HEADS)
    beta_all = jax.nn.sigmoid(sm)
    full = (lt, LANES)
    gcf_s[...] = jnp.broadcast_to(_col(gc_all, lane, h), full)
    gcb_s[...] = jnp.broadcast_to(_col(gc_all, lane, GDN_HEADS + h), full)
    bf_s[...] = jnp.broadcast_to(_col(beta_all, lane, 2 * GDN_HEADS + h), full)
    bb_s[...] = jnp.broadcast_to(_col(beta_all, lane, 3 * GDN_HEADS + h), full)

    ii = lax.broadcasted_iota(jnp.int32, (CHUNK, CHUNK), 0)
    jj = lax.broadcasted_iota(jnp.int32, (CHUNK, CHUNK), 1)
    eye = (ii == jj).astype(F32)

    def prep(c, carry):
        r0 = pl.multiple_of(c * CHUNK, CHUNK)
        rows = pl.ds(r0, CHUNK)
        k = ks[rows, :]
        q = qs[rows, :]
        v = vs[rows, :]
        kbf = k.astype(BF16)
        qk = lax.dot_general(q.astype(BF16), kbf, _NTDOT, preferred_element_type=F32)
        for dirn in range(2):
            gc = (gcf_s if dirn == 0 else gcb_s)[rows, :]
            beta = (bf_s if dirn == 0 else bb_s)[rows, :]
            incl = (ii >= jj) if dirn == 0 else (ii <= jj)
            strict = (ii > jj) if dirn == 0 else (ii < jj)
            gt = jnp.transpose(gc)[:CHUNK, :]
            decay = jnp.where(incl, jnp.exp(jnp.where(incl, gc[:, :CHUNK] - gt, 0.0)), 0.0)
            tot = gc[CHUNK - 1:CHUNK, :] if dirn == 0 else gc[0:1, :]
            kb = k * beta
            akk = lax.dot_general(kb.astype(BF16), kbf, _NTDOT, preferred_element_type=F32)
            x = jnp.where(strict, -(akk * decay), 0.0)
            pinv = eye + x
            xp = x
            m = 2
            while m < CHUNK:
                xp = _sdot(xp, xp)
                pinv = _sdot(pinv, eye + xp)
                m *= 2
            rhs = jnp.concatenate([v * beta, kb * jnp.exp(gc)], axis=1)
            sol = _sdot(pinv, rhs)
            ub = sol[:, :GDN_DV].astype(BF16)
            wb = sol[:, GDN_DV:].astype(BF16)
            kd = (k * jnp.exp(tot - gc)).astype(BF16)
            aqk = jnp.where(incl, qk * decay, 0.0).astype(BF16)
            pm_s[dirn, c] = lax.dot_general(kd, wb, _TDOT, preferred_element_type=F32).astype(BF16)
            n_s[dirn, c] = lax.dot_general(kd, ub, _TDOT, preferred_element_type=F32)
            qw_s[dirn, c] = (q * jnp.exp(gc) - jnp.dot(aqk, wb, preferred_element_type=F32)).astype(BF16)
            au_s[dirn, c] = jnp.dot(aqk, ub, preferred_element_type=F32)
            gl_s[dirn, c] = jnp.broadcast_to(jnp.exp(tot), (SUBLANES, LANES))
        return carry

    lax.fori_loop(0, n_c, prep, 0)

    def step(s, carry):
        new = []
        for dirn, st in enumerate(carry):
            c = s if dirn == 0 else _bwd_chunk(s, n_cc, n_c)
            sbf = st.astype(BF16)
            o = jnp.dot(qw_s[dirn, c], sbf, preferred_element_type=F32) + au_s[dirn, c]
            (of_s if dirn == 0 else ob_s)[pl.ds(pl.multiple_of(c * CHUNK, CHUNK), CHUNK), :] = o
            new.append(gl_s[dirn, c][0:1, :] * st
                       - jnp.dot(pm_s[dirn, c], sbf, preferred_element_type=F32) + n_s[dirn, c])
        return tuple(new)

    zero = jnp.zeros((GDN_DK, GDN_DV), F32)
    lax.fori_loop(0, n_c, step, (zero, zero))

    o = of_s[...] + ob_s[...]
    o = o * lax.rsqrt(jnp.mean(o * o, axis=-1, keepdims=True) + EPS) * ng_ref[...]
    o_ref[...] = o * _silu(z_ref[...])


def _gdn_mixer(p, lc, conv_w, a_log, dt_bias, norm_g):
    b, lt, _ = p.shape
    n_c = lt // CHUNK
    nh = GDN_HEADS
    kern = functools.partial(_gdn_kernel, lc=lc, lt=lt)
    pcol = lambda base: pl.BlockSpec((None, lt, LANES), lambda i, h, base=base: (i, 0, base + h))
    wcol = lambda base: pl.BlockSpec((CONV_W, LANES), lambda i, h, base=base: (0, base + h))
    vec = pl.BlockSpec((1, LANES), lambda i, h: (0, 0))
    pad = jnp.zeros((LANES - 2 * nh,), F32)
    al = jnp.concatenate([a_log.reshape(-1), pad]).reshape(1, LANES)
    dtb = jnp.concatenate([dt_bias.reshape(-1), pad]).reshape(1, LANES)
    seq = lambda dt: pltpu.VMEM((lt, LANES), dt)
    return pl.pallas_call(
        kern,
        grid=(b, nh),
        in_specs=[pcol(P_QKV // LANES), pcol(P_QKV // LANES + nh), pcol(P_QKV // LANES + 2 * nh),
                  pcol(P_GZ // LANES),
                  pl.BlockSpec((None, lt, LANES), lambda i, h: (i, 0, P_SMALL // LANES)),
                  wcol(0), wcol(nh), wcol(2 * nh), vec, vec, vec],
        out_specs=pl.BlockSpec((None, lt, LANES), lambda i, h: (i, 0, h)),
        out_shape=jax.ShapeDtypeStruct((b, lt, GDN_WIDTH), F32),
        scratch_shapes=[seq(F32), seq(F32), seq(F32), seq(F32), seq(F32), seq(F32), seq(F32),
                        pltpu.VMEM((2, n_c, GDN_DK, GDN_DK), BF16),
                        pltpu.VMEM((2, n_c, GDN_DK, GDN_DV), F32),
                        pltpu.VMEM((2, n_c, CHUNK, GDN_DK), BF16),
                        pltpu.VMEM((2, n_c, CHUNK, GDN_DV), F32),
                        pltpu.VMEM((2, n_c, SUBLANES, LANES), F32),
                        seq(F32), seq(F32)],
        compiler_params=_cparams(("parallel", "arbitrary")),
        name="gdn_mixer",
    )(p, p, p, p, p, conv_w, conv_w, conv_w, al, dtb, norm_g.reshape(1, LANES))


M2_GW = M2_HPG * M2_HEADDIM
M2_DT_LANE = 4 * GDN_HEADS


def _ssd_kernel(x_ref, b_ref, c_ref, z_ref, sm_ref, wx_ref, wb_ref, wc_ref, bx_ref, bb_ref, bc_ref,
                al_ref, dtb_ref, dsk_ref, ng_ref, o_ref, xs, bs, cs, dt_s, ac_s, tr_s, lhs_s, h_s,
                *, lc, lt):
    g = pl.program_id(1)
    n_c, n_cc = lt // CHUNK, lc // CHUNK
    row = lax.broadcasted_iota(jnp.int32, (lt, 1), 0)
    lane = lax.broadcasted_iota(jnp.int32, (1, LANES), 1)
    head_of_lane = lax.broadcasted_iota(jnp.int32, (1, M2_GW), 1) // M2_HEADDIM

    xs[...] = _dwconv_silu(x_ref[...], wx_ref, bx_ref[...], row, lc, lt)
    bs[...] = _dwconv_silu(b_ref[...], wb_ref, bb_ref[...], row, lc, lt)
    cs[...] = _dwconv_silu(c_ref[...], wc_ref, bc_ref[...], row, lc, lt)

    dt_all = jax.nn.softplus(sm_ref[...] + dtb_ref[...])
    dt_s[...] = dt_all
    ac_s[...] = _chunk_cumsum(dt_all * (-jnp.exp(al_ref[...])), row, lane < M2_DT_LANE + M2_HEADS)

    o_ref[...] = jnp.zeros_like(o_ref)
    h_s[...] = jnp.zeros_like(h_s)

    ii = lax.broadcasted_iota(jnp.int32, (CHUNK, CHUNK), 0)
    jj = lax.broadcasted_iota(jnp.int32, (CHUNK, CHUNK), 1)

    def step(s, carry):
        for dirn in range(2):
            c = s if dirn == 0 else _bwd_chunk(s, n_cc, n_c)
            rows = pl.ds(pl.multiple_of(c * CHUNK, CHUNK), CHUNK)
            incl = (ii >= jj) if dirn == 0 else (ii <= jj)
            x = xs[rows, :]
            bc = bs[rows, :].astype(BF16)
            cc = cs[rows, :].astype(BF16)
            dtc = dt_s[rows, :]
            acc = ac_s[rows, :]
            tr_s[...] = jnp.transpose(acc)
            cb = lax.dot_general(cc, bc, _NTDOT, preferred_element_type=F32)
            dte = jnp.zeros((CHUNK, M2_GW), F32)
            ace = jnp.zeros((CHUNK, M2_GW), F32)
            for j in range(M2_HPG):
                l_idx = M2_DT_LANE + dirn * M2_HEADS + g * M2_HPG + j
                a_col = _col(acc, lane, l_idx)
                hm = head_of_lane == j
                dte = jnp.where(hm, _col(dtc, lane, l_idx), dte)
                ace = jnp.where(hm, a_col, ace)
                a_row = tr_s[pl.ds(l_idx, 1), :]
                dec = jnp.where(incl, jnp.exp(jnp.where(incl, a_col - a_row, 0.0)), 0.0)
                lhs_s[j * CHUNK:(j + 1) * CHUNK, :] = (cb * dec).astype(BF16)
            tot = ace[CHUNK - 1:CHUNK, :] if dirn == 0 else ace[0:1, :]
            xdt = x * dte
            xdb = xdt.astype(BF16)
            r = jnp.dot(lhs_s[...], xdb, preferred_element_type=F32)
            hprev = h_s[dirn]
            y = jnp.dot(cc, hprev.astype(BF16), preferred_element_type=F32) * jnp.exp(ace)
            for j in range(M2_HPG):
                y = y + jnp.where(head_of_lane == j, r[j * CHUNK:(j + 1) * CHUNK, :], 0.0)
            o_ref[rows, :] += y
            xd = (xdt * jnp.exp(tot - ace)).astype(BF16)
            h_s[dirn] = jnp.exp(tot) * hprev + lax.dot_general(bc, xd, _TDOT, preferred_element_type=F32)
        return carry

    lax.fori_loop(0, n_c, step, 0)

    def fin(c, carry):
        rows = pl.ds(pl.multiple_of(c * CHUNK, CHUNK), CHUNK)
        y = (o_ref[rows, :] + dsk_ref[...] * xs[rows, :]) * _silu(z_ref[rows, :])
        o_ref[rows, :] = y * lax.rsqrt(jnp.mean(y * y, axis=-1, keepdims=True) + EPS) * ng_ref[...]
        return carry

    lax.fori_loop(0, n_c, fin, 0)


def _ssd_mixer(p, lc, conv_w, conv_b, a_log, dt_bias, d_skip, norm_g):
    b, lt, _ = p.shape
    kern = functools.partial(_ssd_kernel, lc=lc, lt=lt)
    gw = M2_GW
    xb0 = P_XBC // LANES
    bcol = xb0 + M2_WIDTH // LANES
    ccol = bcol + M2_GROUPS * M2_STATE // LANES
    wide = lambda base: pl.BlockSpec((None, lt, gw), lambda i, g, base=base: (i, 0, base + g))
    narrow = lambda base: pl.BlockSpec((None, lt, LANES), lambda i, g, base=base: (i, 0, base + g))
    wwide = lambda rows: pl.BlockSpec((rows, gw), lambda i, g: (0, g))
    wnar = lambda rows, base: pl.BlockSpec((rows, LANES), lambda i, g, base=base: (0, base + g))
    vec = pl.BlockSpec((1, LANES), lambda i, g: (0, 0))
    lead = jnp.zeros((M2_DT_LANE,), F32)
    tail = jnp.zeros((LANES - M2_DT_LANE - 2 * M2_HEADS,), F32)
    al = jnp.concatenate([lead, a_log.reshape(-1), tail]).reshape(1, LANES)
    dtb = jnp.concatenate([lead, dt_bias.reshape(-1), tail]).reshape(1, LANES)
    cbias = conv_b.reshape(1, M2_CONV_CH)
    wb0 = M2_WIDTH // LANES
    wc0 = wb0 + M2_GROUPS * M2_STATE // LANES
    return pl.pallas_call(
        kern,
        grid=(b, M2_GROUPS),
        in_specs=[wide(P_XBC // gw), narrow(bcol), narrow(ccol), wide(P_MZ // gw),
                  pl.BlockSpec((None, lt, LANES), lambda i, g: (i, 0, P_SMALL // LANES)),
                  wwide(CONV_W), wnar(CONV_W, wb0), wnar(CONV_W, wc0),
                  wwide(1), wnar(1, wb0), wnar(1, wc0),
                  vec, vec, wwide(1), wwide(1)],
        out_specs=pl.BlockSpec((None, lt, gw), lambda i, g: (i, 0, g)),
        out_shape=jax.ShapeDtypeStruct((b, lt, M2_WIDTH), F32),
        scratch_shapes=[pltpu.VMEM((lt, gw), F32), pltpu.VMEM((lt, LANES), F32), pltpu.VMEM((lt, LANES), F32),
                        pltpu.VMEM((lt, LANES), F32), pltpu.VMEM((lt, LANES), F32),
                        pltpu.VMEM((LANES, CHUNK), F32), pltpu.VMEM((M2_HPG * CHUNK, CHUNK), BF16),
                        pltpu.VMEM((2, M2_STATE, gw), F32)],
        compiler_params=_cparams(("parallel", "arbitrary")),
        name="ssd_mixer",
    )(p, p, p, p, p, conv_w, conv_w, conv_w, cbias, cbias, cbias, al, dtb,
      jnp.repeat(d_skip, M2_HEADDIM).reshape(1, M2_WIDTH), norm_g.reshape(1, M2_WIDTH))
```

```python
import functools
import math

import jax
import jax.numpy as jnp
from jax import lax
from jax.experimental import pallas as pl
from jax.experimental.pallas import tpu as pltpu

F32 = jnp.float32
BF16 = jnp.bfloat16

D_MODEL = 1024
GRID_W = 64
EPS = 1e-6
CHUNK = 64
CONV_W = 5
S5_WIDTH = D_MODEL // 2
S5_GROUP = 16
S5_GROUPS = S5_WIDTH // S5_GROUP
S5_STATE = 64
GDN_DK = 128
GDN_DV = 128
GDN_WIDTH = D_MODEL // 2
GDN_HEADS = GDN_WIDTH // GDN_DV
GDN_CONV_CH = 2 * GDN_HEADS * GDN_DK + GDN_WIDTH
M2_WIDTH = D_MODEL
M2_HEADDIM = 64
M2_HEADS = M2_WIDTH // M2_HEADDIM
M2_GROUPS = 2
M2_HPG = M2_HEADS // M2_GROUPS
M2_STATE = 128
M2_CONV_CH = M2_WIDTH + 2 * M2_GROUPS * M2_STATE
MIX_WIDTH = S5_WIDTH + GDN_WIDTH + M2_WIDTH
IN_SIZES = (S5_WIDTH, GDN_CONV_CH, GDN_WIDTH, 2 * GDN_HEADS, 2 * GDN_HEADS,
            M2_WIDTH, M2_CONV_CH, 2 * M2_HEADS)
D_IN = sum(IN_SIZES)
D_FF = 256 * ((8 * D_MODEL // 3 + 255) // 256)
N_EXPERTS = 8
TOP_K = 2

LANES = 128
SUBLANES = 8
VMEM_LIMIT = 56 * 1024 * 1024

P_U = 0
P_GZ = P_U + S5_WIDTH
P_QKV = P_GZ + GDN_WIDTH
P_MZ = P_QKV + GDN_CONV_CH
P_XBC = P_MZ + M2_WIDTH
P_SMALL = P_XBC + M2_CONV_CH
IN_TN = 896
NP = 6 * IN_TN
assert P_SMALL + LANES <= NP

FF_TF = 256
FF_NF = D_FF // FF_TF
MOE_TM = 512


def _cparams(sem, vmem=VMEM_LIMIT):
    return pltpu.CompilerParams(dimension_semantics=sem, vmem_limit_bytes=vmem)


def _silu(x):
    return x * jax.nn.sigmoid(x)


def _rms(x):
    return x * lax.rsqrt(jnp.mean(x * x, axis=-1, keepdims=True) + EPS)


def _mod_vec(mod_ref, k, is_ctx):
    return jnp.where(is_ctx, mod_ref[0, k:k + 1, :], mod_ref[1, k:k + 1, :])


def _ada_kernel(c_ref, w_ref, b_ref, o_ref):
    s = _silu(c_ref[...])
    o_ref[...] = jnp.dot(s.astype(BF16), w_ref[...].astype(BF16),
                         preferred_element_type=F32) + b_ref[...]


def _ada_all(cpad, ada_w, ada_b):
    depth, d, n6 = ada_w.shape
    tn = 1536
    rows = cpad.shape[0]
    return pl.pallas_call(
        _ada_kernel,
        grid=(depth, n6 // tn),
        in_specs=[pl.BlockSpec((rows, d), lambda i, j: (0, 0)),
                  pl.BlockSpec((None, d, tn), lambda i, j: (i, 0, j)),
                  pl.BlockSpec((None, 1, tn), lambda i, j: (i, 0, j))],
        out_specs=pl.BlockSpec((None, rows, tn), lambda i, j: (i, 0, j)),
        out_shape=jax.ShapeDtypeStruct((depth, rows, n6), F32),
        compiler_params=_cparams(("parallel", "parallel")),
        name="ada_mod",
    )(cpad, ada_w, ada_b.reshape(depth, 1, n6))


def _in_proj_kernel(x_ref, mod_ref, g_ref, w_ref, o_ref, h_scr, *, lc, tm, rc):
    r = pl.program_id(1)

    @pl.when(pl.program_id(2) == 0)
    def _():
        def body(c, carry):
            r0 = pl.multiple_of(c * rc, rc)
            x = x_ref[pl.ds(r0, rc), :]
            row = r * tm + r0 + lax.broadcasted_iota(jnp.int32, (rc, 1), 0)
            is_ctx = row < lc
            h = (_rms(x) * g_ref[...]) * (1.0 + _mod_vec(mod_ref, 1, is_ctx)) + _mod_vec(mod_ref, 0, is_ctx)
            h_scr[pl.ds(r0, rc), :] = h.astype(BF16)
            return carry
        lax.fori_loop(0, tm // rc, body, 0)

    o_ref[...] = jnp.dot(h_scr[...], w_ref[...], preferred_element_type=F32)


def _in_proj(x, mod, g, w_packed, lc):
    b, lt, d = x.shape
    tm = lt // 2
    rc = math.gcd(tm, 128)
    assert lc <= tm and tm % 16 == 0
    kern = functools.partial(_in_proj_kernel, lc=lc, tm=tm, rc=rc)
    return pl.pallas_call(
        kern,
        grid=(b, 2, NP // IN_TN),
        in_specs=[pl.BlockSpec((None, tm, d), lambda i, r, j: (i, r, 0)),
                  pl.BlockSpec((None, 2, 6, d), lambda i, r, j: (i, 0, 0, 0)),
                  pl.BlockSpec((1, d), lambda i, r, j: (0, 0)),
                  pl.BlockSpec((d, IN_TN), lambda i, r, j: (0, j))],
        out_specs=pl.BlockSpec((None, tm, IN_TN), lambda i, r, j: (i, r, j)),
        out_shape=jax.ShapeDtypeStruct((b, lt, NP), F32),
        scratch_shapes=[pltpu.VMEM((tm, d), BF16)],
        compiler_params=_cparams(("parallel", "parallel", "arbitrary")),
        name="in_proj",
    )(x, mod, g.reshape(1, d), w_packed)


def _pack_w_in(w):
    o = [0]
    for s in IN_SIZES:
        o.append(o[-1] + s)
    u, qkv, gz, al, be, mz, xbc, dt = [w[:, o[i]:o[i + 1]] for i in range(8)]
    small_pad = jnp.zeros((w.shape[0], LANES - 2 * 2 * GDN_HEADS - 2 * M2_HEADS), w.dtype)
    tail = jnp.zeros((w.shape[0], NP - P_SMALL - LANES), w.dtype)
    return jnp.concatenate([u, gz, qkv, mz, xbc, al, be, dt, small_pad, tail], axis=1).astype(BF16)


def _out_proj_kernel(a_ref, b_ref, m_ref, x_ref, mod_ref, w_ref, o_ref, *, lc, tm):
    r = pl.program_id(1)
    acc = jnp.dot(a_ref[...].astype(BF16), w_ref[0:S5_WIDTH, :], preferred_element_type=F32)
    acc += jnp.dot(b_ref[...].astype(BF16), w_ref[S5_WIDTH:S5_WIDTH + GDN_WIDTH, :],
                   preferred_element_type=F32)
    acc += jnp.dot(m_ref[...].astype(BF16), w_ref[S5_WIDTH + GDN_WIDTH:, :],
                   preferred_element_type=F32)
    row = r * tm + lax.broadcasted_iota(jnp.int32, (tm, 1), 0)
    o_ref[...] = x_ref[...] + _mod_vec(mod_ref, 2, row < lc) * acc


def _out_proj(a, bm, m, x, mod, w_out_bf, lc):
    b, lt, d = x.shape
    tm = lt // 4 if (lt // 4) % 8 == 0 and lc <= lt // 4 else lt // 2
    kern = functools.partial(_out_proj_kernel, lc=lc, tm=tm)
    tok = lambda w: pl.BlockSpec((None, tm, w), lambda i, r: (i, r, 0))
    return pl.pallas_call(
        kern,
        grid=(b, lt // tm),
        in_specs=[tok(S5_WIDTH), tok(GDN_WIDTH), tok(M2_WIDTH), tok(d),
                  pl.BlockSpec((None, 2, 6, d), lambda i, r: (i, 0, 0, 0)),
                  pl.BlockSpec((MIX_WIDTH, d), lambda i, r: (0, 0))],
        out_specs=tok(d),
        out_shape=jax.ShapeDtypeStruct((b, lt, d), F32),
        compiler_params=_cparams(("parallel", "parallel")),
        name="out_proj",
    )(a, bm, m, x, mod, w_out_bf)


def _swiglu_acc(h, wg_ref, wu_ref, wd_ref):
    def body(f, acc):
        g = jnp.dot(h, wg_ref[f], preferred_element_type=F32)
        u = jnp.dot(h, wu_ref[f], preferred_element_type=F32)
        a = (_silu(g) * u).astype(BF16)
        return acc + jnp.dot(a, wd_ref[f], preferred_element_type=F32)
    return lax.fori_loop(0, FF_NF, body, jnp.zeros((h.shape[0], D_MODEL), F32))


def _ffn_dense_kernel(x_ref, mod_ref, g_ref, wg_ref, wu_ref, wd_ref, o_ref, *, lc, tm):
    r = pl.program_id(1)
    x = x_ref[...]
    row = r * tm + lax.broadcasted_iota(jnp.int32, (tm, 1), 0)
    is_ctx = row < lc
    h = (_rms(x) * g_ref[...]) * (1.0 + _mod_vec(mod_ref, 4, is_ctx)) + _mod_vec(mod_ref, 3, is_ctx)
    acc = _swiglu_acc(h.astype(BF16), wg_ref, wu_ref, wd_ref)
    o_ref[...] = x + _mod_vec(mod_ref, 5, is_ctx) * acc


def _split_ff(wg, wu, wd):
    lead = wg.shape[:-2]
    nl = len(lead)
    perm = tuple(range(nl)) + (nl + 1, nl, nl + 2)
    wg = wg.astype(BF16).reshape(*lead, D_MODEL, FF_NF, FF_TF).transpose(perm)
    wu = wu.astype(BF16).reshape(*lead, D_MODEL, FF_NF, FF_TF).transpose(perm)
    wd = wd.astype(BF16).reshape(*lead, FF_NF, FF_TF, D_MODEL)
    return wg, wu, wd


def _tok_tile(lc, l):
    return math.gcd(math.gcd(lc, l), 256)


def _ffn_dense(x, mod, g, wg, wu, wd, lc):
    b, lt, d = x.shape
    tm = lt // 3 if lt % (3 * SUBLANES) == 0 else _tok_tile(lc, lt - lc)
    kern = functools.partial(_ffn_dense_kernel, lc=lc, tm=tm)
    tok = pl.BlockSpec((None, tm, d), lambda i, r: (i, r, 0))
    wspec = lambda shp: pl.BlockSpec(shp, lambda i, r: (0, 0, 0))
    return pl.pallas_call(
        kern,
        grid=(b, lt // tm),
        in_specs=[tok, pl.BlockSpec((None, 2, 6, d), lambda i, r: (i, 0, 0, 0)),
                  pl.BlockSpec((1, d), lambda i, r: (0, 0)),
                  wspec((FF_NF, d, FF_TF)), wspec((FF_NF, d, FF_TF)), wspec((FF_NF, FF_TF, d))],
        out_specs=tok,
        out_shape=jax.ShapeDtypeStruct((b, lt, d), F32),
        compiler_params=_cparams(("parallel", "parallel")),
        name="ffn_dense",
    )(x, mod, g.reshape(1, d), wg, wu, wd)


def _router_kernel(x_ref, mod_ref, g_ref, rw_ref, h_ref, rt_ref, *, lc, tm):
    r = pl.program_id(1)
    x = x_ref[...]
    row = r * tm + lax.broadcasted_iota(jnp.int32, (tm, 1), 0)
    is_ctx = row < lc
    h = (_rms(x) * g_ref[...]) * (1.0 + _mod_vec(mod_ref, 4, is_ctx)) + _mod_vec(mod_ref, 3, is_ctx)
    h_ref[...] = h
    logits = jnp.dot(h, rw_ref[...], preferred_element_type=F32, precision=lax.Precision.HIGHEST)
    lane = lax.broadcasted_iota(jnp.int32, logits.shape, 1)
    neg = jnp.float32(-jnp.inf)
    lg = jnp.where(lane < N_EXPERTS, logits, neg)
    m1 = jnp.max(lg, axis=-1, keepdims=True)
    i1 = jnp.min(jnp.where(lg == m1, lane, LANES), axis=-1, keepdims=True)
    lg2 = jnp.where(lane == i1, neg, lg)
    m2 = jnp.max(lg2, axis=-1, keepdims=True)
    i2 = jnp.min(jnp.where(lg2 == m2, lane, LANES), axis=-1, keepdims=True)
    e2 = jnp.exp(m2 - m1)
    den = 1.0 + e2
    w1 = 1.0 / den
    w2 = e2 / den
    out = jnp.where(lane == 0, i1.astype(F32), 0.0)
    out = jnp.where(lane == 1, i2.astype(F32), out)
    out = jnp.where(lane == 2, w1, out)
    out = jnp.where(lane == 3, w2, out)
    rt_ref[...] = out


def _router(x, mod, g, router_w, lc):
    b, lt, d = x.shape
    tm = _tok_tile(lc, lt - lc)
    rw = jnp.concatenate([router_w, jnp.zeros((d, LANES - N_EXPERTS), F32)], axis=1)
    kern = functools.partial(_router_kernel, lc=lc, tm=tm)
    tok = lambda w: pl.BlockSpec((None, tm, w), lambda i, r: (i, r, 0))
    return pl.pallas_call(
        kern,
        grid=(b, lt // tm),
        in_specs=[tok(d), pl.BlockSpec((None, 2, 6, d), lambda i, r: (i, 0, 0, 0)),
                  pl.BlockSpec((1, d), lambda i, r: (0, 0)),
                  pl.BlockSpec((d, LANES), lambda i, r: (0, 0))],
        out_specs=[tok(d), tok(LANES)],
        out_shape=[jax.ShapeDtypeStruct((b, lt, d), F32), jax.ShapeDtypeStruct((b, lt, LANES), F32)],
        compiler_params=_cparams(("parallel", "parallel")),
        name="moe_router",
    )(x, mod, g.reshape(1, d), rw)


def _row_copy(src_hbm, dst, idx, r, sem):
    return pltpu.make_async_copy(src_hbm.at[pl.ds(idx, 1), :], dst.at[pl.ds(r, 1), :], sem)


def _gather_rows_kernel(idx_ref, src_hbm, o_ref, sem, *, tg):
    def start(r, c):
        _row_copy(src_hbm, o_ref, idx_ref[r], r, sem).start()
        return c
    lax.fori_loop(0, tg, start, 0)

    def wait(r, c):
        _row_copy(src_hbm, o_ref, 0, r, sem).wait()
        return c
    lax.fori_loop(0, tg, wait, 0)


def _gather_rows(src, idx, tg):
    n_out = idx.shape[0]
    d = src.shape[1]
    kern = functools.partial(_gather_rows_kernel, tg=tg)
    return pl.pallas_call(
        kern,
        grid=(n_out // tg,),
        in_specs=[pl.BlockSpec((tg,), lambda i: (i,), memory_space=pltpu.SMEM),
                  pl.BlockSpec(memory_space=pl.ANY)],
        out_specs=pl.BlockSpec((tg, d), lambda i: (i, 0)),
        out_shape=jax.ShapeDtypeStruct((n_out, d), F32),
        scratch_shapes=[pltpu.SemaphoreType.DMA(())],
        compiler_params=_cparams(("arbitrary",)),
        name="moe_gather",
    )(idx, src)


def _expert_kernel(te_ref, nt_ref, x_ref, wg_ref, wu_ref, wd_ref, o_ref):
    @pl.when(pl.program_id(0) < nt_ref[0])
    def _():
        o_ref[...] = _swiglu_acc(x_ref[...].astype(BF16), wg_ref, wu_ref, wd_ref)

    @pl.when(pl.program_id(0) >= nt_ref[0])
    def _():
        o_ref[...] = jnp.zeros_like(o_ref)


def _experts(xs, tile_expert, n_tiles_used, wg, wu, wd):
    rp, d = xs.shape
    wspec = lambda shp: pl.BlockSpec((None,) + shp, lambda t, te, nt: (te[t], 0, 0, 0))
    grid_spec = pltpu.PrefetchScalarGridSpec(
        num_scalar_prefetch=2,
        grid=(rp // MOE_TM,),
        in_specs=[pl.BlockSpec((MOE_TM, d), lambda t, te, nt: (t, 0)),
                  wspec((FF_NF, d, FF_TF)), wspec((FF_NF, d, FF_TF)), wspec((FF_NF, FF_TF, d))],
        out_specs=pl.BlockSpec((MOE_TM, d), lambda t, te, nt: (t, 0)),
    )
    return pl.pallas_call(
        _expert_kernel,
        grid_spec=grid_spec,
        out_shape=jax.ShapeDtypeStruct((rp, d), F32),
        compiler_params=_cparams(("arbitrary",)),
        name="moe_experts",
    )(tile_expert, n_tiles_used, xs, wg, wu, wd)


def _combine_kernel(p1_ref, p2_ref, y_hbm, x_ref, rt_ref, mod_ref, o_ref, ya, yb, sem, *, lc, tm):
    r = pl.program_id(1)

    def start(k, c):
        _row_copy(y_hbm, ya, p1_ref[k], k, sem).start()
        _row_copy(y_hbm, yb, p2_ref[k], k, sem).start()
        return c
    lax.fori_loop(0, tm, start, 0)

    def wait(k, c):
        _row_copy(y_hbm, ya, 0, k, sem).wait()
        _row_copy(y_hbm, yb, 0, k, sem).wait()
        return c
    lax.fori_loop(0, tm, wait, 0)

    row = r * tm + lax.broadcasted_iota(jnp.int32, (tm, 1), 0)
    rt = rt_ref[...]
    y = rt[:, 2:3] * ya[...] + rt[:, 3:4] * yb[...]
    o_ref[...] = x_ref[...] + _mod_vec(mod_ref, 5, row < lc) * y


def _combine(p1, p2, y, x, rt, mod, lc):
    b, lt, d = x.shape
    tm = _tok_tile(lc, lt - lc)
    nt = lt // tm
    kern = functools.partial(_combine_kernel, lc=lc, tm=tm)
    tok = lambda w: pl.BlockSpec((None, tm, w), lambda i, r: (i, r, 0))
    ispec = pl.BlockSpec((tm,), lambda i, r: (i * nt + r,), memory_space=pltpu.SMEM)
    return pl.pallas_call(
        kern,
        grid=(b, nt),
        in_specs=[ispec, ispec, pl.BlockSpec(memory_space=pl.ANY), tok(d), tok(LANES),
                  pl.BlockSpec((None, 2, 6, d), lambda i, r: (i, 0, 0, 0))],
        out_specs=tok(d),
        out_shape=jax.ShapeDtypeStruct((b, lt, d), F32),
        scratch_shapes=[pltpu.VMEM((tm, d), F32), pltpu.VMEM((tm, d), F32),
                        pltpu.SemaphoreType.DMA(())],
        compiler_params=_cparams(("arbitrary", "arbitrary")),
        name="moe_combine",
    )(p1, p2, y, x, rt, mod)


def _moe(x, mod, g, router_w, wg, wu, wd, lc):
    b, lt, d = x.shape
    n = b * lt
    h2, rt = _router(x, mod, g, router_w, lc)
    rt2 = rt.reshape(n, LANES)
    e_flat = jnp.concatenate([rt2[:, 0], rt2[:, 1]]).astype(jnp.int32)
    onehot = (e_flat[:, None] == jnp.arange(N_EXPERTS, dtype=jnp.int32)[None, :]).astype(jnp.int32)
    counts = jnp.sum(onehot, axis=0)
    rank = jnp.sum((jnp.cumsum(onehot, axis=0) - 1) * onehot, axis=1)
    padded = ((counts + MOE_TM - 1) // MOE_TM) * MOE_TM
    ends = jnp.cumsum(padded)
    starts = ends - padded
    pos = starts[e_flat] + rank
    rp = ((2 * n + MOE_TM - 1) // MOE_TM + N_EXPERTS) * MOE_TM
    tok_id = jnp.concatenate([jnp.arange(n, dtype=jnp.int32)] * 2)
    src_row = jnp.zeros((rp,), jnp.int32).at[pos].set(tok_id)
    n_tiles = rp // MOE_TM
    tile_start = jnp.arange(n_tiles, dtype=jnp.int32) * MOE_TM
    tile_expert = jnp.minimum(jnp.sum(tile_start[:, None] >= ends[None, :], axis=1),
                              N_EXPERTS - 1).astype(jnp.int32)
    n_used = (ends[-1] // MOE_TM).astype(jnp.int32).reshape(1)
    xs = _gather_rows(h2.reshape(n, d), src_row, MOE_TM)
    ys = _experts(xs, tile_expert, n_used, wg, wu, wd)
    return _combine(pos[:n].astype(jnp.int32), pos[n:].astype(jnp.int32), ys, x, rt, mod, lc)


def _final_norm_kernel(x_ref, g_ref, o_ref):
    o_ref[...] = _rms(x_ref[...]) * g_ref[...]


def _final_norm(x, g):
    b, l, d = x.shape
    tm = math.gcd(l, 512)
    tok = pl.BlockSpec((None, tm, d), lambda i, r: (i, r, 0))
    return pl.pallas_call(
        _final_norm_kernel,
        grid=(b, l // tm),
        in_specs=[tok, pl.BlockSpec((1, d), lambda i, r: (0, 0))],
        out_specs=tok,
        out_shape=jax.ShapeDtypeStruct((b, l, d), F32),
        compiler_params=_cparams(("parallel", "parallel")),
        name="final_norm",
    )(x, g.reshape(1, d))


def _t_rmsnorm(x, g):
    return (x * lax.rsqrt(jnp.mean(x * x, axis=-1, keepdims=True) + EPS)) * g


def _t_l2norm(t):
    return t * lax.rsqrt(jnp.sum(t * t, axis=-1, keepdims=True) + EPS)


def _t_rev(t, flag):
    return jnp.flip(t, axis=1) if flag else t


def _t_dwconv(u, w):
    pad = CONV_W // 2
    return lax.conv_general_dilated(
        u, w[:, None, :].astype(u.dtype), window_strides=(1,), padding=[(pad, pad)],
        dimension_numbers=("NWC", "WIO", "NWC"), feature_group_count=u.shape[-1])


def _t_seg_decay(cs):
    t = cs.shape[-1]
    mask = jnp.tril(jnp.ones((t, t), dtype=bool))
    diff = cs[..., :, None] - cs[..., None, :]
    return jnp.where(mask, jnp.exp(jnp.where(mask, diff, 0.0)), 0.0)


def _t_affine(left, right):
    a_l, b_l = left
    a_r, b_r = right
    return a_l * a_r, a_r * b_l + b_r


def _t_linear_scan(a, b, h0, reverse):
    a_cum, h = lax.associative_scan(_t_affine, (a, b), axis=1, reverse=reverse)
    if h0 is not None:
        h = h + a_cum * h0[:, None]
    return h


def _t_s5_direction(ug, lam_re, lam_im, b_re, b_im, log_dt, h0, reverse):
    lam = lax.complex(jnp.minimum(lam_re, -1e-4), lam_im)
    dt = jnp.exp(log_dt)[:, None]
    lam_bar = jnp.exp(lam * dt)
    gamma = (lam_bar - 1.0) / lam
    bu = lax.complex(jnp.einsum("blgi,gpi->blgp", ug, b_re),
                     jnp.einsum("blgi,gpi->blgp", ug, b_im)) * gamma
    return _t_linear_scan(jnp.broadcast_to(lam_bar, bu.shape), bu, h0, reverse)


def _t_s5_readout(states, c_re, c_im):
    return (jnp.einsum("blgp,gip->blgi", states.real, c_re)
            - jnp.einsum("blgp,gip->blgi", states.imag, c_im))


def _t_s5_mixer(u_ctx, u_lat, lam_re, lam_im, b_re, b_im, c_re, c_im, log_dt, d_skip, w_glu):
    uc = u_ctx.reshape(*u_ctx.shape[:2], S5_GROUPS, S5_GROUP)
    ul = u_lat.reshape(*u_lat.shape[:2], S5_GROUPS, S5_GROUP)
    y_ctx, y_lat = [], []
    for d in range(2):
        reverse = d == 1
        p = (lam_re[d], lam_im[d], b_re[d], b_im[d], log_dt[d])
        s_ctx = _t_s5_direction(uc, *p, None, reverse)
        h_end = s_ctx[:, 0] if reverse else s_ctx[:, -1]
        s_lat = _t_s5_direction(ul, *p, h_end, reverse)
        y_lat.append(_t_s5_readout(s_lat, c_re[d], c_im[d]))
        y_ctx.append(_t_s5_readout(s_ctx, c_re[d], c_im[d]))

    def finish(ys, u):
        y = (ys[0] + ys[1]).reshape(u.shape) + d_skip * u
        y = jax.nn.gelu(y)
        return y * jax.nn.sigmoid(y @ w_glu)

    return finish(y_ctx, u_ctx), finish(y_lat, u_lat)


def _t_gdn_prep(qkv, a_raw, b_raw, conv_w, a_log, dt_bias):
    qkv = jax.nn.silu(_t_dwconv(qkv, conv_w))
    bsz, l, _ = qkv.shape
    q = qkv[..., :GDN_HEADS * GDN_DK]
    k = qkv[..., GDN_HEADS * GDN_DK:2 * GDN_HEADS * GDN_DK]
    v = qkv[..., 2 * GDN_HEADS * GDN_DK:]
    q = _t_l2norm(q.reshape(bsz, l, GDN_HEADS, GDN_DK))
    k = _t_l2norm(k.reshape(bsz, l, GDN_HEADS, GDN_DK))
    v = v.reshape(bsz, l, GDN_HEADS, GDN_DV)
    g = -jnp.exp(a_log) * jax.nn.softplus(a_raw.reshape(bsz, l, 2, GDN_HEADS) + dt_bias)
    beta = jax.nn.sigmoid(b_raw.reshape(bsz, l, 2, GDN_HEADS))
    return q, k, v, g, beta


def _t_gdn_chunked(q, k, v, g, beta, s0):
    b, l, h, dk = q.shape
    n = l // CHUNK

    def blk(t):
        return jnp.moveaxis(t.reshape(b, n, CHUNK, h, *t.shape[3:]), 3, 1)

    q, k, v, beta = blk(q) * dk ** -0.5, blk(k), blk(v), blk(beta)
    gc = jnp.cumsum(blk(g), axis=-1)
    decay = _t_seg_decay(gc)
    strict = jnp.tril(jnp.ones((CHUNK, CHUNK), dtype=bool), -1)
    kb = k * beta[..., None]
    lower = jnp.where(strict, jnp.einsum("bhnid,bhnjd->bhnij", kb, k) * decay, 0.0)
    rhs = jnp.concatenate([v * beta[..., None], kb * jnp.exp(gc)[..., None]], axis=-1)
    sol = lax.linalg.triangular_solve(jnp.eye(CHUNK, dtype=F32) + lower, rhs,
                                      left_side=True, lower=True, unit_diagonal=True)
    dv = v.shape[-1]
    u_c, w_c = sol[..., :dv], sol[..., dv:]
    k_dec = k * jnp.exp(gc[..., -1:] - gc)[..., None]
    g_last = jnp.exp(gc[..., -1])
    xs = [u_c, w_c, k_dec, g_last, q * jnp.exp(gc)[..., None],
          jnp.einsum("bhnid,bhnjd->bhnij", q, k) * decay]

    def step(s, inp):
        u_i, w_i, kd_i, gl_i, qd_i, a_i = inp
        v_new = u_i - jnp.einsum("bhcd,bhde->bhce", w_i, s)
        s_next = s * gl_i[..., None, None] + jnp.einsum("bhcd,bhce->bhde", kd_i, v_new)
        o = jnp.einsum("bhcd,bhde->bhce", qd_i, s) + jnp.einsum("bhij,bhje->bhie", a_i, v_new)
        return s_next, o

    s_fin, o = lax.scan(step, s0, [jnp.moveaxis(t, 2, 0) for t in xs])
    o = jnp.moveaxis(jnp.moveaxis(o, 0, 2), 1, 3).reshape(b, l, h, dv)
    return o, s_fin


def _t_gdn_direction(p, d, s_init):
    q, k, v, g, beta = p
    r = d == 1
    o, s_fin = _t_gdn_chunked(_t_rev(q, r), _t_rev(k, r), _t_rev(v, r), _t_rev(g[:, :, d], r),
                              _t_rev(beta[:, :, d], r), s_init)
    return _t_rev(o, r), s_fin


def _t_gdn_mixer(ctx_in, lat_in, conv_w, a_log, dt_bias, norm_g):
    pc = _t_gdn_prep(ctx_in[0], ctx_in[2], ctx_in[3], conv_w, a_log, dt_bias)
    pl_ = _t_gdn_prep(lat_in[0], lat_in[2], lat_in[3], conv_w, a_log, dt_bias)
    s0 = jnp.zeros((ctx_in[0].shape[0], GDN_HEADS, GDN_DK, GDN_DV), F32)
    o_ctx, o_lat = [], []
    for d in range(2):
        oc, s_ctx = _t_gdn_direction(pc, d, s0)
        ol, _ = _t_gdn_direction(pl_, d, s_ctx)
        o_lat.append(ol)
        o_ctx.append(oc)

    def finish(os, z):
        o = _t_rmsnorm(os[0] + os[1], norm_g)
        return (o * jax.nn.silu(z).reshape(o.shape)).reshape(z.shape)

    return finish(o_ctx, ctx_in[1]), finish(o_lat, lat_in[1])


def _t_m2_prep(xbc, dt_raw, conv_w, conv_b, dt_bias):
    xbc = jax.nn.silu(_t_dwconv(xbc, conv_w) + conv_b)
    bsz, l, _ = xbc.shape
    xs = xbc[..., :M2_WIDTH]
    bm = xbc[..., M2_WIDTH:M2_WIDTH + M2_GROUPS * M2_STATE]
    cm = xbc[..., M2_WIDTH + M2_GROUPS * M2_STATE:]
    xs = xs.reshape(bsz, l, M2_GROUPS, M2_HPG, M2_HEADDIM)
    bm = bm.reshape(bsz, l, M2_GROUPS, M2_STATE)
    cm = cm.reshape(bsz, l, M2_GROUPS, M2_STATE)
    dt = jax.nn.softplus(dt_raw.reshape(bsz, l, 2, M2_GROUPS, M2_HPG)
                         + dt_bias.reshape(2, M2_GROUPS, M2_HPG))
    return xs, bm, cm, dt


def _t_ssd_chunked(xdt, log_a, bm, cm, h0):
    bsz, l = xdt.shape[:2]
    n = l // CHUNK
    xc = xdt.reshape(bsz, n, CHUNK, *xdt.shape[2:])
    bc = bm.reshape(bsz, n, CHUNK, *bm.shape[2:])
    cc = cm.reshape(bsz, n, CHUNK, *cm.shape[2:])
    a_cs = jnp.cumsum(jnp.moveaxis(log_a.reshape(bsz, n, CHUNK, M2_GROUPS, M2_HPG), (3, 4), (1, 2)),
                      axis=-1)
    decay_to_end = jnp.exp(a_cs[..., -1:] - a_cs)
    chunk_states = jnp.einsum("bncge,bgjnc,bncgjp->bngjpe", bc, decay_to_end, xc)
    if h0 is None:
        h0 = jnp.zeros_like(chunk_states[:, 0])
    chunk_states = jnp.concatenate([h0[:, None], chunk_states], axis=1)
    chunk_cs = jnp.cumsum(jnp.pad(a_cs[..., -1], [(0, 0)] * 3 + [(1, 0)]), axis=-1)
    states = jnp.einsum("bgjzy,bygjpe->bzgjpe", _t_seg_decay(chunk_cs), chunk_states)
    y_diag = jnp.einsum("bncge,bnsge,bgjncs,bnsgjp->bncgjp", cc, bc, _t_seg_decay(a_cs), xc)
    y_off = jnp.einsum("bncge,bngjpe,bgjnc->bncgjp", cc, states[:, :-1], jnp.exp(a_cs))
    return (y_diag + y_off).reshape(xdt.shape), states[:, -1]


def _t_m2_direction(p, a, d, h0):
    xs, bm, cm, dt = p
    r = d == 1
    dtd = dt[:, :, d]
    y, h = _t_ssd_chunked(_t_rev(xs * dtd[..., None], r), _t_rev(dtd * a[d], r), _t_rev(bm, r),
                          _t_rev(cm, r), h0)
    return _t_rev(y, r), h


def _t_m2_mixer(ctx_in, lat_in, conv_w, conv_b, a_log, dt_bias, d_skip, norm_g):
    pc = _t_m2_prep(ctx_in[1], ctx_in[2], conv_w, conv_b, dt_bias)
    pl_ = _t_m2_prep(lat_in[1], lat_in[2], conv_w, conv_b, dt_bias)
    a = -jnp.exp(a_log).reshape(2, M2_GROUPS, M2_HPG)
    y_ctx, y_lat = [], []
    for d in range(2):
        yc, h_ctx = _t_m2_direction(pc, a, d, None)
        yl, _ = _t_m2_direction(pl_, a, d, h_ctx)
        y_lat.append(yl)
        y_ctx.append(yc)
    d_skip = d_skip.reshape(M2_GROUPS, M2_HPG, 1)

    def finish(ys, xs, z):
        bsz, l = z.shape[:2]
        y = (ys[0] + ys[1] + d_skip * xs).reshape(bsz, l, M2_GROUPS, -1)
        y = y * jax.nn.silu(z).reshape(bsz, l, M2_GROUPS, -1)
        return _t_rmsnorm(y, norm_g.reshape(M2_GROUPS, -1)).reshape(bsz, l, M2_WIDTH)

    return finish(y_ctx, pc[0], ctx_in[0]), finish(y_lat, pl_[0], lat_in[0])


def _t_token_mixer(p, lc, s5_p, gdn_p, m2_p):
    def seg(t):
        return dict(u=t[..., P_U:P_U + S5_WIDTH], gz=t[..., P_GZ:P_GZ + GDN_WIDTH],
                    qkv=t[..., P_QKV:P_QKV + GDN_CONV_CH], mz=t[..., P_MZ:P_MZ + M2_WIDTH],
                    xbc=t[..., P_XBC:P_XBC + M2_CONV_CH],
                    al=t[..., P_SMALL:P_SMALL + 8], be=t[..., P_SMALL + 8:P_SMALL + 16],
                    dt=t[..., P_SMALL + 16:P_SMALL + 48])
    c, l = seg(p[:, :lc]), seg(p[:, lc:])
    a_c, a_l = _t_s5_mixer(c["u"], l["u"], *s5_p)
    b_c, b_l = _t_gdn_mixer((c["qkv"], c["gz"], c["al"], c["be"]), (l["qkv"], l["gz"], l["al"], l["be"]),
                            *gdn_p)
    m_c, m_l = _t_m2_mixer((c["mz"], c["xbc"], c["dt"]), (l["mz"], l["xbc"], l["dt"]), *m2_p)
    cat = lambda x, y: jnp.concatenate([x, y], axis=1)
    return cat(a_c, a_l), cat(b_c, b_l), cat(m_c, m_l)


def _grid_t(xl, rows, cols):
    b, l, ch = xl.shape
    return xl.reshape(b, rows, cols, ch).transpose(0, 2, 1, 3).reshape(b, l, ch)


def kernel(x, c, ctx, c_ctx, ada_w, ada_b, norm1_g, norm2_g, w_in, w_out,
           s5_lam_re, s5_lam_im, s5_b_re, s5_b_im, s5_c_re, s5_c_im, s5_log_dt, s5_d, s5_w_glu,
           gdn_conv_w, gdn_a_log, gdn_dt_bias, gdn_norm_g,
           m2_conv_w, m2_conv_b, m2_a_log, m2_dt_bias, m2_d, m2_norm_g,
           ffn_w_gate, ffn_w_up, ffn_w_down,
           moe_router, moe_w_gate, moe_w_up, moe_w_down, final_norm_g):
    b, l, d = x.shape
    lc = ctx.shape[1]
    depth = ada_w.shape[0]
    rows = l // GRID_W

    cpad = jnp.zeros((2 * SUBLANES, d), F32).at[:b].set(c).at[b].set(c_ctx)
    mods = _ada_all(cpad, ada_w, ada_b).reshape(depth, 2 * SUBLANES, 6, d)

    xt = jnp.concatenate([ctx, x], axis=1)
    col_major = False
    for i in range(depth):
        want_cm = i % 2 == 1
        if want_cm != col_major:
            lat = xt[:, lc:]
            lat = _grid_t(lat, rows, GRID_W) if want_cm else _grid_t(lat, GRID_W, rows)
            xt = jnp.concatenate([xt[:, :lc], lat], axis=1)
            col_major = want_cm
        mod = jnp.stack([jnp.broadcast_to(mods[i, b][None], (b, 6, d)), mods[i, :b]], axis=1)

        p = _in_proj(xt, mod, norm1_g[i], _pack_w_in(w_in[i]), lc)
        u_tb = jnp.transpose(p[:, :, P_U:P_U + S5_WIDTH], (1, 0, 2)).reshape((lc + l) * b, S5_WIDTH)
        s5_par = _s5_params(s5_lam_re[i], s5_lam_im[i], s5_b_re[i], s5_b_im[i], s5_c_re[i], s5_c_im[i],
                            s5_log_dt[i])
        ya = _s5_mixer(u_tb, lc, b, s5_par, s5_d[i], s5_w_glu[i])
        ya = jnp.transpose(ya.reshape(lc + l, b, S5_WIDTH), (1, 0, 2))
        yb = _gdn_mixer(p, lc, gdn_conv_w[i], gdn_a_log[i], gdn_dt_bias[i], gdn_norm_g[i])
        ym = _ssd_mixer(p, lc, m2_conv_w[i], m2_conv_b[i], m2_a_log[i], m2_dt_bias[i], m2_d[i],
                        m2_norm_g[i])
        xt = _out_proj(ya, yb, ym, xt, mod, w_out[i].astype(BF16), lc)

        j = i // 2
        if i % 2 == 0:
            wg, wu, wd = _split_ff(ffn_w_gate[j], ffn_w_up[j], ffn_w_down[j])
            xt = _ffn_dense(xt, mod, norm2_g[i], wg, wu, wd, lc)
        else:
            wg, wu, wd = _split_ff(moe_w_gate[j], moe_w_up[j], moe_w_down[j])
            xt = _moe(xt, mod, norm2_g[i], moe_router[j], wg, wu, wd, lc)

    lat = xt[:, lc:]
    if col_major:
        lat = _grid_t(lat, GRID_W, rows)
    return _final_norm(lat, final_norm_g)


S5_HALF = S5_WIDTH // 2
S5_HS = (S5_GROUPS // 2) * S5_STATE
S5_LQ = 512


def _s5_scan_kernel(u_ref, bk_ref, ck_ref, a_ref, y_ref, s_scr, h_scr, *, tt, nb):
    d = pl.program_id(0)

    @pl.when(pl.program_id(1) == 0)
    def _():
        h_scr[...] = jnp.zeros_like(h_scr)

    u = u_ref[...].astype(BF16)
    for k in range(2):
        s_scr[:, k * 2 * S5_HS:(k + 1) * 2 * S5_HS] = jnp.dot(
            u[:, k * S5_HALF:(k + 1) * S5_HALF], bk_ref[k], preferred_element_type=F32)

    for k in range(2):
        for q in range(S5_HS // S5_LQ):
            re0 = k * 2 * S5_HS + q * S5_LQ
            im0 = re0 + S5_HS
            a_re = jnp.broadcast_to(a_ref[2 * k:2 * k + 1, q * S5_LQ:(q + 1) * S5_LQ], (nb, S5_LQ))
            a_im = jnp.broadcast_to(a_ref[2 * k + 1:2 * k + 2, q * S5_LQ:(q + 1) * S5_LQ], (nb, S5_LQ))

            def step(i, carry, re0=re0, im0=im0, a_re=a_re, a_im=a_im):
                hr, hi = carry
                t = i + d * (tt - 1 - 2 * i)
                r = pl.multiple_of(t * nb, nb)
                nr = a_re * hr - a_im * hi + s_scr[pl.ds(r, nb), re0:re0 + S5_LQ]
                ni = a_re * hi + a_im * hr + s_scr[pl.ds(r, nb), im0:im0 + S5_LQ]
                s_scr[pl.ds(r, nb), re0:re0 + S5_LQ] = nr
                s_scr[pl.ds(r, nb), im0:im0 + S5_LQ] = ni
                return nr, ni

            hr, hi = lax.fori_loop(0, tt, step, (h_scr[:, re0:re0 + S5_LQ], h_scr[:, im0:im0 + S5_LQ]),
                                   unroll=4)
            h_scr[:, re0:re0 + S5_LQ] = hr
            h_scr[:, im0:im0 + S5_LQ] = hi

    for k in range(2):
        y_ref[:, k * S5_HALF:(k + 1) * S5_HALF] = jnp.dot(
            s_scr[:, k * 2 * S5_HS:(k + 1) * 2 * S5_HS].astype(BF16), ck_ref[k],
            preferred_element_type=F32)


def _s5_fin_kernel(y_ref, u_ref, d_ref, w_ref, o_ref):
    u = u_ref[...]
    y = y_ref[0] + y_ref[1] + d_ref[...] * u
    y = jax.nn.gelu(y)
    o_ref[...] = y * jax.nn.sigmoid(jnp.dot(y.astype(BF16), w_ref[...], preferred_element_type=F32))


def _s5_params(lam_re, lam_im, b_re, b_im, c_re, c_im, log_dt):
    lam = lax.complex(jnp.minimum(lam_re, -1e-4), lam_im)
    dt = jnp.exp(log_dt)[..., None]
    lam_bar = jnp.exp(lam * dt)
    gamma = (lam_bar - 1.0) / lam
    bt = lax.complex(b_re, b_im) * gamma[..., None]
    gh = S5_GROUPS // 2
    eye = jnp.eye(gh, dtype=F32)

    def bd_in(m):
        m = m.reshape(2, 2, gh, S5_STATE, S5_GROUP)
        return jnp.einsum("gh,dkgpi->dkgihp", eye, m).reshape(2, 2, gh * S5_GROUP, gh * S5_STATE)

    def bd_out(m):
        m = m.reshape(2, 2, gh, S5_GROUP, S5_STATE)
        return jnp.einsum("gh,dkgip->dkgphi", eye, m).reshape(2, 2, gh * S5_STATE, gh * S5_GROUP)

    bk = jnp.concatenate([bd_in(bt.real), bd_in(bt.imag)], axis=-1).astype(BF16)
    ck = jnp.concatenate([bd_out(c_re), -bd_out(c_im)], axis=-2).astype(BF16)
    lr = lam_bar.real.reshape(2, 2, 1, S5_HS)
    li = lam_bar.imag.reshape(2, 2, 1, S5_HS)
    a = jnp.concatenate([lr, li], axis=2).reshape(2, 4, S5_HS)
    return bk, ck, a


def _s5_mixer(u_tb, lc, nb, params, d_skip, w_glu):
    n, w = u_tb.shape
    lt = n // nb
    tt = CHUNK
    n_c, n_t = lc // tt, lt // tt
    bk, ck, a = params

    def tile(dd, s):
        bwd = jnp.where(s < n_c, n_c - 1 - s, n_t - 1 - s + n_c)
        return jnp.where(dd == 0, s, bwd)

    kern = functools.partial(_s5_scan_kernel, tt=tt, nb=nb)
    y = pl.pallas_call(
        kern,
        grid=(2, n_t),
        in_specs=[pl.BlockSpec((tt * nb, w), lambda dd, s: (tile(dd, s), 0)),
                  pl.BlockSpec((None, 2, S5_HALF, 2 * S5_HS), lambda dd, s: (dd, 0, 0, 0)),
                  pl.BlockSpec((None, 2, 2 * S5_HS, S5_HALF), lambda dd, s: (dd, 0, 0, 0)),
                  pl.BlockSpec((None, 4, S5_HS), lambda dd, s: (dd, 0, 0))],
        out_specs=pl.BlockSpec((None, tt * nb, w), lambda dd, s: (dd, tile(dd, s), 0)),
        out_shape=jax.ShapeDtypeStruct((2, n, w), F32),
        scratch_shapes=[pltpu.VMEM((tt * nb, 4 * S5_HS), F32), pltpu.VMEM((nb, 4 * S5_HS), F32)],
        compiler_params=_cparams(("arbitrary", "arbitrary")),
        name="s5_scan",
    )(u_tb, bk, ck, a)

    tm = math.gcd(n, 512)
    return pl.pallas_call(
        _s5_fin_kernel,
        grid=(n // tm,),
        in_specs=[pl.BlockSpec((2, tm, w), lambda i: (0, i, 0)),
                  pl.BlockSpec((tm, w), lambda i: (i, 0)),
                  pl.BlockSpec((1, w), lambda i: (0, 0)),
                  pl.BlockSpec((w, w), lambda i: (0, 0))],
        out_specs=pl.BlockSpec((tm, w), lambda i: (i, 0)),
        out_shape=jax.ShapeDtypeStruct((n, w), F32),
        compiler_params=_cparams(("parallel",)),
        name="s5_finish",
    )(y, u_tb, d_skip.reshape(1, w), w_glu.astype(BF16))


def _seq_shift(x, s, row, lc, lt):
    tgt = row + s
    ok = (tgt >= 0) & (tgt < lt) & ((row < lc) == (tgt < lc))
    return jnp.where(ok, pltpu.roll(x, (-s) % lt, 0), 0.0)


def _dwconv_silu(x, w_ref, bias, row, lc, lt):
    pad = CONV_W // 2
    acc = w_ref[pad:pad + 1, :] * x
    for j in range(CONV_W):
        if j != pad:
            acc = acc + w_ref[j:j + 1, :] * _seq_shift(x, j - pad, row, lc, lt)
    if bias is not None:
        acc = acc + bias
    return _silu(acc)


def _chunk_cumsum(x, row, fwd_lanes):
    lt = x.shape[0]
    pos = row & (CHUNK - 1)
    s = 1
    while s < CHUNK:
        dn = jnp.where(pos >= s, pltpu.roll(x, s, 0), 0.0)
        up = jnp.where(pos < CHUNK - s, pltpu.roll(x, lt - s, 0), 0.0)
        x = x + jnp.where(fwd_lanes, dn, up)
        s *= 2
    return x


def _col(arr, lane, idx):
    return jnp.sum(jnp.where(lane == idx, arr, 0.0), axis=-1, keepdims=True)


def _bwd_chunk(s, n_cc, n_c):
    return jnp.where(s < n_cc, n_cc - 1 - s, n_c - 1 - s + n_cc)


def _unroll_for(trips, want):
    return max(u for u in range(1, want + 1) if trips % u == 0)


GDN_GROUP = 4

_TDOT = (((0,), (0,)), ((), ()))
_NTDOT = (((1,), (1,)), ((), ()))


def _sdot(a, b):
    return jnp.dot(a.astype(BF16), b.astype(BF16), preferred_element_type=F32)


def _gdn_kernel(q_ref, k_ref, v_ref, z_ref, sm_ref, wq_ref, wk_ref, wv_ref, al_ref, dtb_ref, ng_ref,
                o_ref, qs, ks, vs, gcf_s, gcb_s, bf_s, bb_s, pm_s, n_s, qw_s, au_s, gl_s, of_s, ob_s,
                *, lc, lt):
    h = pl.program_id(1)
    n_c, n_cc = lt // CHUNK, lc // CHUNK
    row = lax.broadcasted_iota(jnp.int32, (lt, 1), 0)
    lane = lax.broadcasted_iota(jnp.int32, (1, LANES), 1)

    def l2n(t):
        return t * lax.rsqrt(jnp.sum(t * t, axis=-1, keepdims=True) + EPS)

    qs[...] = l2n(_dwconv_silu(q_ref[...], wq_ref, None, row, lc, lt)) * (GDN_DK ** -0.5)
    ks[...] = l2n(_dwconv_silu(k_ref[...], wk_ref, None, row, lc, lt))
    vs[...] = _dwconv_silu(v_ref[...], wv_ref, None, row, lc, lt)

    sm = sm_ref[...]
    g_all = -jnp.exp(al_ref[...]) * jax.nn.softplus(sm + dtb_ref[...])
    gc_all = _chunk_cumsum(g_all, row, lane < GDN_HEADS)
    beta_all = jax.nn.sigmoid(sm)
    full = (lt, LANES)
    gcf_s[...] = jnp.broadcast_to(_col(gc_all, lane, h), full)
    gcb_s[...] = jnp.broadcast_to(_col(gc_all, lane, GDN_HEADS + h), full)
    bf_s[...] = jnp.broadcast_to(_col(beta_all, lane, 2 * GDN_HEADS + h), full)
    bb_s[...] = jnp.broadcast_to(_col(beta_all, lane, 3 * GDN_HEADS + h), full)

    gc_n = _unroll_for(n_c, GDN_GROUP)
    gs = gc_n * CHUNK
    ii = lax.broadcasted_iota(jnp.int32, (gs, gs), 0)
    jj = lax.broadcasted_iota(jnp.int32, (gs, gs), 1)
    same = (ii // CHUNK) == (jj // CHUNK)
    eye = (ii == jj).astype(F32)
    col_chunk = lax.broadcasted_iota(jnp.int32, (GDN_DK, gs), 1) // CHUNK

    def prep(gi, carry):
        rows = pl.ds(pl.multiple_of(gi * gs, gs), gs)
        k = ks[rows, :]
        q = qs[rows, :]
        v = vs[rows, :]
        kbf = k.astype(BF16)
        qk = lax.dot_general(q.astype(BF16), kbf, _NTDOT, preferred_element_type=F32)
        for dirn in range(2):
            gc = (gcf_s if dirn == 0 else gcb_s)[rows, :]
            beta = (bf_s if dirn == 0 else bb_s)[rows, :]
            incl = same & ((ii >= jj) if dirn == 0 else (ii <= jj))
            strict = same & ((ii > jj) if dirn == 0 else (ii < jj))
            g_i = jnp.concatenate([gc] * (gs // LANES), axis=1) if gs % LANES == 0 else \
                jnp.broadcast_to(gc[:, :1], (gs, gs))
            g_j = jnp.broadcast_to(jnp.transpose(gc)[0:1, :], (gs, gs))
            decay = jnp.where(incl, jnp.exp(jnp.where(incl, g_i - g_j, 0.0)), 0.0)
            edge = CHUNK - 1 if dirn == 0 else 0
            tots = [gc[cc * CHUNK + edge:cc * CHUNK + edge + 1, :] for cc in range(gc_n)]
            tot = jnp.concatenate([jnp.broadcast_to(t, (CHUNK, LANES)) for t in tots], axis=0)
            kb = k * beta
            akk = lax.dot_general(kb.astype(BF16), kbf, _NTDOT, preferred_element_type=F32)
            x = jnp.where(strict, -(akk * decay), 0.0)
            pinv = eye + x
            xp = x
            m = 2
            while m < CHUNK:
                xp = _sdot(xp, xp)
                pinv = _sdot(pinv, eye + xp)
                m *= 2
            rhs = jnp.concatenate([v * beta, kb * jnp.exp(gc)], axis=1)
            uw = _sdot(pinv, rhs).astype(BF16)
            kdt = jnp.transpose(k * jnp.exp(tot - gc))
            kd_bd = jnp.concatenate([jnp.where(col_chunk == cc, kdt, 0.0) for cc in range(gc_n)],
                                    axis=0).astype(BF16)
            nw = jnp.dot(kd_bd, uw, preferred_element_type=F32)
            aqk = jnp.where(incl, qk * decay, 0.0).astype(BF16)
            aw = jnp.dot(aqk, uw, preferred_element_type=F32)
            qw_s[dirn, rows, :] = (q * jnp.exp(gc) - aw[:, GDN_DV:]).astype(BF16)
            au_s[dirn, rows, :] = aw[:, :GDN_DV]
            for cc in range(gc_n):
                c = gi * gc_n + cc
                blk = nw[cc * GDN_DK:(cc + 1) * GDN_DK, :]
                n_s[dirn, c] = blk[:, :GDN_DV]
                pm_s[dirn, c] = blk[:, GDN_DV:].astype(BF16)
                gl_s[dirn, c] = jnp.broadcast_to(jnp.exp(tots[cc]), (SUBLANES, LANES))
        return carry

    lax.fori_loop(0, n_c // gc_n, prep, 0)

    def step(s, carry):
        new = []
        for dirn, st in enumerate(carry):
            c = s if dirn == 0 else _bwd_chunk(s, n_cc, n_c)
            sbf = st.astype(BF16)
            crow = pl.ds(pl.multiple_of(c * CHUNK, CHUNK), CHUNK)
            o = jnp.dot(qw_s[dirn, crow, :], sbf, preferred_element_type=F32) + au_s[dirn, crow, :]
            (of_s if dirn == 0 else ob_s)[crow, :] = o
            new.append(gl_s[dirn, c][0:1, :] * st
                       - jnp.dot(pm_s[dirn, c], sbf, preferred_element_type=F32) + n_s[dirn, c])
        return tuple(new)

    zero = jnp.zeros((GDN_DK, GDN_DV), F32)
    lax.fori_loop(0, n_c, step, (zero, zero))

    o = of_s[...] + ob_s[...]
    o = o * lax.rsqrt(jnp.mean(o * o, axis=-1, keepdims=True) + EPS) * ng_ref[...]
    o_ref[...] = o * _silu(z_ref[...])


def _gdn_mixer(p, lc, conv_w, a_log, dt_bias, norm_g):
    b, lt, _ = p.shape
    n_c = lt // CHUNK
    nh = GDN_HEADS
    kern = functools.partial(_gdn_kernel, lc=lc, lt=lt)
    pcol = lambda base: pl.BlockSpec((None, lt, LANES), lambda i, h, base=base: (i, 0, base + h))
    wcol = lambda base: pl.BlockSpec((CONV_W, LANES), lambda i, h, base=base: (0, base + h))
    vec = pl.BlockSpec((1, LANES), lambda i, h: (0, 0))
    pad = jnp.zeros((LANES - 2 * nh,), F32)
    al = jnp.concatenate([a_log.reshape(-1), pad]).reshape(1, LANES)
    dtb = jnp.concatenate([dt_bias.reshape(-1), pad]).reshape(1, LANES)
    seq = lambda dt: pltpu.VMEM((lt, LANES), dt)
    return pl.pallas_call(
        kern,
        grid=(b, nh),
        in_specs=[pcol(P_QKV // LANES), pcol(P_QKV // LANES + nh), pcol(P_QKV // LANES + 2 * nh),
                  pcol(P_GZ // LANES),
                  pl.BlockSpec((None, lt, LANES), lambda i, h: (i, 0, P_SMALL // LANES)),
                  wcol(0), wcol(nh), wcol(2 * nh), vec, vec, vec],
        out_specs=pl.BlockSpec((None, lt, LANES), lambda i, h: (i, 0, h)),
        out_shape=jax.ShapeDtypeStruct((b, lt, GDN_WIDTH), F32),
        scratch_shapes=[seq(F32), seq(F32), seq(F32), seq(F32), seq(F32), seq(F32), seq(F32),
                        pltpu.VMEM((2, n_c, GDN_DK, GDN_DK), BF16),
                        pltpu.VMEM((2, n_c, GDN_DK, GDN_DV), F32),
                        pltpu.VMEM((2, lt, GDN_DK), BF16),
                        pltpu.VMEM((2, lt, GDN_DV), F32),
                        pltpu.VMEM((2, n_c, SUBLANES, LANES), F32),
                        seq(F32), seq(F32)],
        compiler_params=_cparams(("parallel", "arbitrary")),
        name="gdn_mixer",
    )(p, p, p, p, p, conv_w, conv_w, conv_w, al, dtb, norm_g.reshape(1, LANES))


M2_GW = M2_HPG * M2_HEADDIM
M2_DT_LANE = 4 * GDN_HEADS


def _ssd_kernel(x_ref, b_ref, c_ref, z_ref, sm_ref, wx_ref, wb_ref, wc_ref, bx_ref, bb_ref, bc_ref,
                al_ref, dtb_ref, dsk_ref, ng_ref, o_ref, xs, bs, cs, dt_s, ac_s, tr_s, lhs_s, h_s,
                *, lc, lt):
    g = pl.program_id(1)
    n_c, n_cc = lt // CHUNK, lc // CHUNK
    row = lax.broadcasted_iota(jnp.int32, (lt, 1), 0)
    lane = lax.broadcasted_iota(jnp.int32, (1, LANES), 1)
    head_of_lane = lax.broadcasted_iota(jnp.int32, (1, M2_GW), 1) // M2_HEADDIM

    xs[...] = _dwconv_silu(x_ref[...], wx_ref, bx_ref[...], row, lc, lt)
    bs[...] = _dwconv_silu(b_ref[...], wb_ref, bb_ref[...], row, lc, lt)
    cs[...] = _dwconv_silu(c_ref[...], wc_ref, bc_ref[...], row, lc, lt)

    dt_all = jax.nn.softplus(sm_ref[...] + dtb_ref[...])
    dt_s[...] = dt_all
    ac_s[...] = _chunk_cumsum(dt_all * (-jnp.exp(al_ref[...])), row, lane < M2_DT_LANE + M2_HEADS)

    o_ref[...] = jnp.zeros_like(o_ref)
    h_s[...] = jnp.zeros_like(h_s)

    ii = lax.broadcasted_iota(jnp.int32, (CHUNK, CHUNK), 0)
    jj = lax.broadcasted_iota(jnp.int32, (CHUNK, CHUNK), 1)

    def step(s, carry):
        for dirn in range(2):
            c = s if dirn == 0 else _bwd_chunk(s, n_cc, n_c)
            rows = pl.ds(pl.multiple_of(c * CHUNK, CHUNK), CHUNK)
            incl = (ii >= jj) if dirn == 0 else (ii <= jj)
            x = xs[rows, :]
            bc = bs[rows, :].astype(BF16)
            cc = cs[rows, :].astype(BF16)
            dtc = dt_s[rows, :]
            acc = ac_s[rows, :]
            tr_s[...] = jnp.transpose(acc)
            cb = lax.dot_general(cc, bc, _NTDOT, preferred_element_type=F32)
            dte = jnp.zeros((CHUNK, M2_GW), F32)
            ace = jnp.zeros((CHUNK, M2_GW), F32)
            for j in range(M2_HPG):
                l_idx = M2_DT_LANE + dirn * M2_HEADS + g * M2_HPG + j
                a_col = _col(acc, lane, l_idx)
                hm = head_of_lane == j
                dte = jnp.where(hm, _col(dtc, lane, l_idx), dte)
                ace = jnp.where(hm, a_col, ace)
                a_row = tr_s[pl.ds(l_idx, 1), :]
                dec = jnp.where(incl, jnp.exp(jnp.where(incl, a_col - a_row, 0.0)), 0.0)
                lhs_s[j * CHUNK:(j + 1) * CHUNK, :] = (cb * dec).astype(BF16)
            tot = ace[CHUNK - 1:CHUNK, :] if dirn == 0 else ace[0:1, :]
            xdt = x * dte
            xdb = xdt.astype(BF16)
            r = jnp.dot(lhs_s[...], xdb, preferred_element_type=F32)
            hprev = h_s[dirn]
            y = jnp.dot(cc, hprev.astype(BF16), preferred_element_type=F32) * jnp.exp(ace)
            for j in range(M2_HPG):
                y = y + jnp.where(head_of_lane == j, r[j * CHUNK:(j + 1) * CHUNK, :], 0.0)
            o_ref[rows, :] += y
            xd = (xdt * jnp.exp(tot - ace)).astype(BF16)
            h_s[dirn] = jnp.exp(tot) * hprev + lax.dot_general(bc, xd, _TDOT, preferred_element_type=F32)
        return carry

    lax.fori_loop(0, n_c, step, 0, unroll=_unroll_for(n_c, 2))

    def fin(c, carry):
        rows = pl.ds(pl.multiple_of(c * CHUNK, CHUNK), CHUNK)
        y = (o_ref[rows, :] + dsk_ref[...] * xs[rows, :]) * _silu(z_ref[rows, :])
        o_ref[rows, :] = y * lax.rsqrt(jnp.mean(y * y, axis=-1, keepdims=True) + EPS) * ng_ref[...]
        return carry

    lax.fori_loop(0, n_c, fin, 0)


def _ssd_mixer(p, lc, conv_w, conv_b, a_log, dt_bias, d_skip, norm_g):
    b, lt, _ = p.shape
    kern = functools.partial(_ssd_kernel, lc=lc, lt=lt)
    gw = M2_GW
    xb0 = P_XBC // LANES
    bcol = xb0 + M2_WIDTH // LANES
    ccol = bcol + M2_GROUPS * M2_STATE // LANES
    wide = lambda base: pl.BlockSpec((None, lt, gw), lambda i, g, base=base: (i, 0, base + g))
    narrow = lambda base: pl.BlockSpec((None, lt, LANES), lambda i, g, base=base: (i, 0, base + g))
    wwide = lambda rows: pl.BlockSpec((rows, gw), lambda i, g: (0, g))
    wnar = lambda rows, base: pl.BlockSpec((rows, LANES), lambda i, g, base=base: (0, base + g))
    vec = pl.BlockSpec((1, LANES), lambda i, g: (0, 0))
    lead = jnp.zeros((M2_DT_LANE,), F32)
    tail = jnp.zeros((LANES - M2_DT_LANE - 2 * M2_HEADS,), F32)
    al = jnp.concatenate([lead, a_log.reshape(-1), tail]).reshape(1, LANES)
    dtb = jnp.concatenate([lead, dt_bias.reshape(-1), tail]).reshape(1, LANES)
    cbias = conv_b.reshape(1, M2_CONV_CH)
    wb0 = M2_WIDTH // LANES
    wc0 = wb0 + M2_GROUPS * M2_STATE // LANES
    return pl.pallas_call(
        kern,
        grid=(b, M2_GROUPS),
        in_specs=[wide(P_XBC // gw), narrow(bcol), narrow(ccol), wide(P_MZ // gw),
                  pl.BlockSpec((None, lt, LANES), lambda i, g: (i, 0, P_SMALL // LANES)),
                  wwide(CONV_W), wnar(CONV_W, wb0), wnar(CONV_W, wc0),
                  wwide(1), wnar(1, wb0), wnar(1, wc0),
                  vec, vec, wwide(1), wwide(1)],
        out_specs=pl.BlockSpec((None, lt, gw), lambda i, g: (i, 0, g)),
        out_shape=jax.ShapeDtypeStruct((b, lt, M2_WIDTH), F32),
        scratch_shapes=[pltpu.VMEM((lt, gw), F32), pltpu.VMEM((lt, LANES), F32), pltpu.VMEM((lt, LANES), F32),
                        pltpu.VMEM((lt, LANES), F32), pltpu.VMEM((lt, LANES), F32),
                        pltpu.VMEM((LANES, CHUNK), F32), pltpu.VMEM((M2_HPG * CHUNK, CHUNK), BF16),
                        pltpu.VMEM((2, M2_STATE, gw), F32)],
        compiler_params=_cparams(("parallel", "arbitrary")),
        name="ssd_mixer",
    )(p, p, p, p, p, conv_w, conv_w, conv_w, cbias, cbias, cbias, al, dtb,
      jnp.repeat(d_skip, M2_HEADDIM).reshape(1, M2_WIDTH), norm_g.reshape(1, M2_WIDTH))
```

```python
import functools
import math

import jax
import jax.numpy as jnp
from jax import lax
from jax.experimental import pallas as pl
from jax.experimental.pallas import tpu as pltpu

F32 = jnp.float32
BF16 = jnp.bfloat16

D_MODEL = 1024
GRID_W = 64
EPS = 1e-6
CHUNK = 64
CONV_W = 5
S5_WIDTH = D_MODEL // 2
S5_GROUP = 16
S5_GROUPS = S5_WIDTH // S5_GROUP
S5_STATE = 64
GDN_DK = 128
GDN_DV = 128
GDN_WIDTH = D_MODEL // 2
GDN_HEADS = GDN_WIDTH // GDN_DV
GDN_CONV_CH = 2 * GDN_HEADS * GDN_DK + GDN_WIDTH
M2_WIDTH = D_MODEL
M2_HEADDIM = 64
M2_HEADS = M2_WIDTH // M2_HEADDIM
M2_GROUPS = 2
M2_HPG = M2_HEADS // M2_GROUPS
M2_STATE = 128
M2_CONV_CH = M2_WIDTH + 2 * M2_GROUPS * M2_STATE
MIX_WIDTH = S5_WIDTH + GDN_WIDTH + M2_WIDTH
IN_SIZES = (S5_WIDTH, GDN_CONV_CH, GDN_WIDTH, 2 * GDN_HEADS, 2 * GDN_HEADS,
            M2_WIDTH, M2_CONV_CH, 2 * M2_HEADS)
D_FF = 256 * ((8 * D_MODEL // 3 + 255) // 256)
N_EXPERTS = 8

LANES = 128
SUBLANES = 8
MXU_TILE = 256
VMEM_LIMIT = 56 * 1024 * 1024

P_U = 0
P_GZ = P_U + S5_WIDTH
P_QKV = P_GZ + GDN_WIDTH
P_MZ = P_QKV + GDN_CONV_CH
P_XBC = P_MZ + M2_WIDTH
P_SMALL = P_XBC + M2_CONV_CH
IN_TN = 896
NP = 6 * IN_TN
assert P_SMALL + LANES <= NP

FF_TF = 256
FF_NF = D_FF // FF_TF
MOE_TM = 512


def _cparams(sem):
    return pltpu.CompilerParams(dimension_semantics=sem, vmem_limit_bytes=VMEM_LIMIT)


def _silu(x):
    return x * jax.nn.sigmoid(x)


def _rms(x):
    return x * lax.rsqrt(jnp.mean(x * x, axis=-1, keepdims=True) + EPS)


def _mod_vec(mod_ref, k, is_ctx):
    return jnp.where(is_ctx, mod_ref[0, k:k + 1, :], mod_ref[1, k:k + 1, :])


def _largest_divisor(n, cap):
    return max(u for u in range(1, cap + 1) if n % u == 0)


_TDOT = (((0,), (0,)), ((), ()))
_NTDOT = (((1,), (1,)), ((), ()))


def _bdot(a, b):
    return jnp.dot(a.astype(BF16), b.astype(BF16), preferred_element_type=F32)


def _bbdot(a, b):
    return lax.dot_general(a.astype(BF16), b.astype(BF16), (((2,), (1,)), ((0,), (0,))),
                           preferred_element_type=F32)


def _ada_kernel(c_ref, w_ref, b_ref, o_ref):
    o_ref[...] = _bdot(_silu(c_ref[...]), w_ref[...]) + b_ref[...]


def _ada_all(cpad, ada_w, ada_b):
    depth, d, n6 = ada_w.shape
    tn = 1536
    rows = cpad.shape[0]
    return pl.pallas_call(
        _ada_kernel,
        grid=(depth, n6 // tn),
        in_specs=[pl.BlockSpec((rows, d), lambda i, j: (0, 0)),
                  pl.BlockSpec((None, d, tn), lambda i, j: (i, 0, j)),
                  pl.BlockSpec((None, 1, tn), lambda i, j: (i, 0, j))],
        out_specs=pl.BlockSpec((None, rows, tn), lambda i, j: (i, 0, j)),
        out_shape=jax.ShapeDtypeStruct((depth, rows, n6), F32),
        compiler_params=_cparams(("parallel", "parallel")),
        name="ada_mod",
    )(cpad, ada_w, ada_b.reshape(depth, 1, n6))


def _in_proj_kernel(x_ref, mod_ref, g_ref, w_ref, o_ref, h_scr, *, lc, tm, rc):
    r = pl.program_id(1)

    @pl.when(pl.program_id(2) == 0)
    def _():
        def body(c, carry):
            r0 = pl.multiple_of(c * rc, rc)
            x = x_ref[pl.ds(r0, rc), :]
            row = r * tm + r0 + lax.broadcasted_iota(jnp.int32, (rc, 1), 0)
            is_ctx = row < lc
            h = (_rms(x) * g_ref[...]) * (1.0 + _mod_vec(mod_ref, 1, is_ctx)) + _mod_vec(mod_ref, 0, is_ctx)
            h_scr[pl.ds(r0, rc), :] = h.astype(BF16)
            return carry
        lax.fori_loop(0, tm // rc, body, 0)

    o_ref[...] = jnp.dot(h_scr[...], w_ref[...], preferred_element_type=F32)


def _in_proj(x, mod, g, w_packed, lc):
    b, lt, d = x.shape
    tm = lt // 2
    rc = math.gcd(tm, 128)
    assert lc <= tm and tm % 16 == 0
    kern = functools.partial(_in_proj_kernel, lc=lc, tm=tm, rc=rc)
    return pl.pallas_call(
        kern,
        grid=(b, 2, NP // IN_TN),
        in_specs=[pl.BlockSpec((None, tm, d), lambda i, r, j: (i, r, 0)),
                  pl.BlockSpec((None, 2, 6, d), lambda i, r, j: (i, 0, 0, 0)),
                  pl.BlockSpec((1, d), lambda i, r, j: (0, 0)),
                  pl.BlockSpec((d, IN_TN), lambda i, r, j: (0, j))],
        out_specs=pl.BlockSpec((None, tm, IN_TN), lambda i, r, j: (i, r, j)),
        out_shape=jax.ShapeDtypeStruct((b, lt, NP), F32),
        scratch_shapes=[pltpu.VMEM((tm, d), BF16)],
        compiler_params=_cparams(("parallel", "parallel", "arbitrary")),
        name="in_proj",
    )(x, mod, g.reshape(1, d), w_packed)


def _pack_w_in(w):
    o = [0]
    for s in IN_SIZES:
        o.append(o[-1] + s)
    u, qkv, gz, al, be, mz, xbc, dt = [w[:, o[i]:o[i + 1]] for i in range(8)]
    small_pad = jnp.zeros((w.shape[0], LANES - 4 * GDN_HEADS - 2 * M2_HEADS), w.dtype)
    tail = jnp.zeros((w.shape[0], NP - P_SMALL - LANES), w.dtype)
    return jnp.concatenate([u, gz, qkv, mz, xbc, al, be, dt, small_pad, tail], axis=1).astype(BF16)


S5_HALF = S5_WIDTH // 2
S5_HS = (S5_GROUPS // 2) * S5_STATE
S5_LQ = 512


def _s5_scan_kernel(u_ref, bk_ref, ck_ref, a_ref, y_ref, s_scr, h_scr, *, tt, nb):
    d = pl.program_id(0)

    @pl.when(pl.program_id(1) == 0)
    def _():
        h_scr[...] = jnp.zeros_like(h_scr)

    u = u_ref[...].astype(BF16)
    for k in range(2):
        s_scr[:, k * 2 * S5_HS:(k + 1) * 2 * S5_HS] = jnp.dot(
            u[:, k * S5_HALF:(k + 1) * S5_HALF], bk_ref[k], preferred_element_type=F32)

    for k in range(2):
        for q in range(S5_HS // S5_LQ):
            re0 = k * 2 * S5_HS + q * S5_LQ
            im0 = re0 + S5_HS
            a_re = jnp.broadcast_to(a_ref[2 * k:2 * k + 1, q * S5_LQ:(q + 1) * S5_LQ], (nb, S5_LQ))
            a_im = jnp.broadcast_to(a_ref[2 * k + 1:2 * k + 2, q * S5_LQ:(q + 1) * S5_LQ], (nb, S5_LQ))

            def step(i, carry, re0=re0, im0=im0, a_re=a_re, a_im=a_im):
                hr, hi = carry
                t = i + d * (tt - 1 - 2 * i)
                r = pl.multiple_of(t * nb, nb)
                nr = a_re * hr - a_im * hi + s_scr[pl.ds(r, nb), re0:re0 + S5_LQ]
                ni = a_re * hi + a_im * hr + s_scr[pl.ds(r, nb), im0:im0 + S5_LQ]
                s_scr[pl.ds(r, nb), re0:re0 + S5_LQ] = nr
                s_scr[pl.ds(r, nb), im0:im0 + S5_LQ] = ni
                return nr, ni

            hr, hi = lax.fori_loop(0, tt, step, (h_scr[:, re0:re0 + S5_LQ], h_scr[:, im0:im0 + S5_LQ]),
                                   unroll=4)
            h_scr[:, re0:re0 + S5_LQ] = hr
            h_scr[:, im0:im0 + S5_LQ] = hi

    for k in range(2):
        y_ref[:, k * S5_HALF:(k + 1) * S5_HALF] = jnp.dot(
            s_scr[:, k * 2 * S5_HS:(k + 1) * 2 * S5_HS].astype(BF16), ck_ref[k],
            preferred_element_type=F32)


def _s5_fin_kernel(y_ref, u_ref, d_ref, w_ref, o_ref):
    y = y_ref[0] + y_ref[1] + d_ref[...] * u_ref[...]
    y = jax.nn.gelu(y)
    o_ref[...] = y * jax.nn.sigmoid(jnp.dot(y.astype(BF16), w_ref[...], preferred_element_type=F32))


def _s5_params(lam_re, lam_im, b_re, b_im, c_re, c_im, log_dt):
    lam = lax.complex(jnp.minimum(lam_re, -1e-4), lam_im)
    dt = jnp.exp(log_dt)[..., None]
    lam_bar = jnp.exp(lam * dt)
    gamma = (lam_bar - 1.0) / lam
    bt = lax.complex(b_re, b_im) * gamma[..., None]
    gh = S5_GROUPS // 2
    eye = jnp.eye(gh, dtype=F32)

    def bd_in(m):
        m = m.reshape(2, 2, gh, S5_STATE, S5_GROUP)
        return jnp.einsum("gh,dkgpi->dkgihp", eye, m).reshape(2, 2, gh * S5_GROUP, gh * S5_STATE)

    def bd_out(m):
        m = m.reshape(2, 2, gh, S5_GROUP, S5_STATE)
        return jnp.einsum("gh,dkgip->dkgphi", eye, m).reshape(2, 2, gh * S5_STATE, gh * S5_GROUP)

    bk = jnp.concatenate([bd_in(bt.real), bd_in(bt.imag)], axis=-1).astype(BF16)
    ck = jnp.concatenate([bd_out(c_re), -bd_out(c_im)], axis=-2).astype(BF16)
    lr = lam_bar.real.reshape(2, 2, 1, S5_HS)
    li = lam_bar.imag.reshape(2, 2, 1, S5_HS)
    a = jnp.concatenate([lr, li], axis=2).reshape(2, 4, S5_HS)
    return bk, ck, a


def _s5_mixer(u_tb, lc, nb, params, d_skip, w_glu):
    n, w = u_tb.shape
    lt = n // nb
    tt = CHUNK
    n_c, n_t = lc // tt, lt // tt
    bk, ck, a = params

    def tile(dd, s):
        return jnp.where(dd == 0, s, jnp.where(s < n_c, n_c - 1 - s, n_t - 1 - s + n_c))

    kern = functools.partial(_s5_scan_kernel, tt=tt, nb=nb)
    y = pl.pallas_call(
        kern,
        grid=(2, n_t),
        in_specs=[pl.BlockSpec((tt * nb, w), lambda dd, s: (tile(dd, s), 0)),
                  pl.BlockSpec((None, 2, S5_HALF, 2 * S5_HS), lambda dd, s: (dd, 0, 0, 0)),
                  pl.BlockSpec((None, 2, 2 * S5_HS, S5_HALF), lambda dd, s: (dd, 0, 0, 0)),
                  pl.BlockSpec((None, 4, S5_HS), lambda dd, s: (dd, 0, 0))],
        out_specs=pl.BlockSpec((None, tt * nb, w), lambda dd, s: (dd, tile(dd, s), 0)),
        out_shape=jax.ShapeDtypeStruct((2, n, w), F32),
        scratch_shapes=[pltpu.VMEM((tt * nb, 4 * S5_HS), F32), pltpu.VMEM((nb, 4 * S5_HS), F32)],
        compiler_params=_cparams(("arbitrary", "arbitrary")),
        name="s5_scan",
    )(u_tb, bk, ck, a)

    tm = math.gcd(n, 512)
    return pl.pallas_call(
        _s5_fin_kernel,
        grid=(n // tm,),
        in_specs=[pl.BlockSpec((2, tm, w), lambda i: (0, i, 0)),
                  pl.BlockSpec((tm, w), lambda i: (i, 0)),
                  pl.BlockSpec((1, w), lambda i: (0, 0)),
                  pl.BlockSpec((w, w), lambda i: (0, 0))],
        out_specs=pl.BlockSpec((tm, w), lambda i: (i, 0)),
        out_shape=jax.ShapeDtypeStruct((n, w), F32),
        compiler_params=_cparams(("parallel",)),
        name="s5_finish",
    )(y, u_tb, d_skip.reshape(1, w), w_glu.astype(BF16))


def _seq_shift(x, s, row, lc, lt):
    tgt = row + s
    ok = (tgt >= 0) & (tgt < lt) & ((row < lc) == (tgt < lc))
    return jnp.where(ok, pltpu.roll(x, (-s) % lt, 0), 0.0)


def _dwconv_silu(x, w_ref, bias, row, lc, lt):
    pad = CONV_W // 2
    acc = w_ref[pad:pad + 1, :] * x
    for j in range(CONV_W):
        if j != pad:
            acc = acc + w_ref[j:j + 1, :] * _seq_shift(x, j - pad, row, lc, lt)
    if bias is not None:
        acc = acc + bias
    return _silu(acc)


def _chunk_cumsum(x, row, fwd_lanes):
    lt = x.shape[0]
    pos = row & (CHUNK - 1)
    s = 1
    while s < CHUNK:
        dn = jnp.where(pos >= s, pltpu.roll(x, s, 0), 0.0)
        up = jnp.where(pos < CHUNK - s, pltpu.roll(x, lt - s, 0), 0.0)
        x = x + jnp.where(fwd_lanes, dn, up)
        s *= 2
    return x


def _col(arr, lane, idx):
    return jnp.sum(jnp.where(lane == idx, arr, 0.0), axis=-1, keepdims=True)


def _bwd_chunk(s, n_cc, n_c):
    return jnp.where(s < n_cc, n_cc - 1 - s, n_c - 1 - s + n_cc)


GDN_GROUP = MXU_TILE // CHUNK
GDN_GROUPS_PER_PASS = 3


def _gdn_kernel(q_ref, k_ref, v_ref, z_ref, sm_ref, wq_ref, wk_ref, wv_ref, al_ref, dtb_ref, ng_ref,
                o_ref, qs, ks, vs, gcf_s, gcb_s, bf_s, bb_s, pm_s, n_s, qw_s, au_s, gl_s, of_s, ob_s,
                *, lc, lt):
    h = pl.program_id(1)
    n_c, n_cc = lt // CHUNK, lc // CHUNK
    row = lax.broadcasted_iota(jnp.int32, (lt, 1), 0)
    lane = lax.broadcasted_iota(jnp.int32, (1, LANES), 1)

    def l2n(t):
        return t * lax.rsqrt(jnp.sum(t * t, axis=-1, keepdims=True) + EPS)

    qs[...] = l2n(_dwconv_silu(q_ref[...], wq_ref, None, row, lc, lt)) * (GDN_DK ** -0.5)
    ks[...] = l2n(_dwconv_silu(k_ref[...], wk_ref, None, row, lc, lt))
    vs[...] = _dwconv_silu(v_ref[...], wv_ref, None, row, lc, lt)

    sm = sm_ref[...]
    g_all = -jnp.exp(al_ref[...]) * jax.nn.softplus(sm + dtb_ref[...])
    gc_all = _chunk_cumsum(g_all, row, lane < GDN_HEADS)
    beta_all = jax.nn.sigmoid(sm)
    full = (lt, LANES)
    gcf_s[...] = jnp.broadcast_to(_col(gc_all, lane, h), full)
    gcb_s[...] = jnp.broadcast_to(_col(gc_all, lane, GDN_HEADS + h), full)
    bf_s[...] = jnp.broadcast_to(_col(beta_all, lane, 2 * GDN_HEADS + h), full)
    bb_s[...] = jnp.broadcast_to(_col(beta_all, lane, 3 * GDN_HEADS + h), full)

    gc_n = _largest_divisor(n_c, GDN_GROUP)
    gs = gc_n * CHUNK
    ii = lax.broadcasted_iota(jnp.int32, (gs, gs), 0)
    jj = lax.broadcasted_iota(jnp.int32, (gs, gs), 1)
    same = (ii // CHUNK) == (jj // CHUNK)
    eye = (ii == jj).astype(F32)
    col_chunk = lax.broadcasted_iota(jnp.int32, (GDN_DK, gs), 1) // CHUNK

    n_g = n_c // gc_n
    gb = _largest_divisor(n_g, GDN_GROUPS_PER_PASS)

    def prep(it, carry):
        xs_, rhs_, lhs_, meta = [], [], [], []
        for u in range(gb):
            gi = it * gb + u
            rows = pl.ds(pl.multiple_of(gi * gs, gs), gs)
            k = ks[rows, :]
            q = qs[rows, :]
            v = vs[rows, :]
            gcs = (gcf_s[rows, :], gcb_s[rows, :])
            betas = (bf_s[rows, :], bb_s[rows, :])
            kbs = (k * betas[0], k * betas[1])
            prods = lax.dot_general(jnp.concatenate([kbs[0], kbs[1], q], axis=0).astype(BF16),
                                    k.astype(BF16), _NTDOT, preferred_element_type=F32)
            qk = prods[2 * gs:, :]
            for dirn in range(2):
                gc, beta, kb = gcs[dirn], betas[dirn], kbs[dirn]
                incl = same & ((ii >= jj) if dirn == 0 else (ii <= jj))
                strict = same & ((ii > jj) if dirn == 0 else (ii < jj))
                g_i = jnp.concatenate([gc] * (gs // LANES), axis=1) if gs % LANES == 0 else \
                    jnp.broadcast_to(gc[:, :1], (gs, gs))
                g_j = jnp.broadcast_to(jnp.transpose(gc)[0:1, :], (gs, gs))
                decay = jnp.where(incl, jnp.exp(jnp.where(incl, g_i - g_j, 0.0)), 0.0)
                edge = CHUNK - 1 if dirn == 0 else 0
                tots = [gc[cc * CHUNK + edge:cc * CHUNK + edge + 1, :] for cc in range(gc_n)]
                tot = jnp.concatenate([jnp.broadcast_to(t, (CHUNK, LANES)) for t in tots], axis=0)
                kdt = jnp.transpose(k * jnp.exp(tot - gc))
                kd_bd = [jnp.where(col_chunk == cc, kdt, 0.0) for cc in range(gc_n)]
                aqk = jnp.where(incl, qk * decay, 0.0)
                xs_.append(jnp.where(strict, -(prods[dirn * gs:(dirn + 1) * gs, :] * decay), 0.0))
                rhs_.append(jnp.concatenate([v * beta, kb * jnp.exp(gc)], axis=1))
                lhs_.append(jnp.concatenate(kd_bd + [aqk], axis=0).astype(BF16))
                meta.append((dirn, gi, rows, q * jnp.exp(gc), tots))
        xp = jnp.stack(xs_)
        pinv = eye[None] + xp
        xp = _bbdot(xp, xp)
        m = 2
        while 2 * m < CHUNK:
            t = _bbdot(xp, jnp.concatenate([xp, pinv], axis=2))
            xp, pinv = t[:, :, :gs], pinv + t[:, :, gs:]
            m *= 2
        pinv = pinv + _bbdot(xp, pinv)
        uw = _bbdot(pinv, jnp.stack(rhs_)).astype(BF16)
        res_all = _bbdot(jnp.stack(lhs_), uw)
        for idx, (dirn, gi, rows, qd, tots) in enumerate(meta):
            res = res_all[idx]
            aw = res[gc_n * GDN_DK:, :]
            qw_s[dirn, rows, :] = (qd - aw[:, GDN_DV:]).astype(BF16)
            au_s[dirn, rows, :] = aw[:, :GDN_DV]
            for cc in range(gc_n):
                c = gi * gc_n + cc
                blk = res[cc * GDN_DK:(cc + 1) * GDN_DK, :]
                n_s[dirn, c] = blk[:, :GDN_DV]
                pm_s[dirn, c] = blk[:, GDN_DV:].astype(BF16)
                gl_s[dirn, c] = jnp.broadcast_to(jnp.exp(tots[cc]), (SUBLANES, LANES))
        return carry

    lax.fori_loop(0, n_g // gb, prep, 0)

    def step(s, st):
        cs_ = (s, _bwd_chunk(s, n_cc, n_c))
        crows = [pl.ds(pl.multiple_of(c * CHUNK, CHUNK), CHUNK) for c in cs_]
        lhs = jnp.stack([jnp.concatenate([qw_s[dirn, crows[dirn], :], pm_s[dirn, cs_[dirn]]], axis=0)
                         for dirn in range(2)])
        r = _bbdot(lhs, st)
        of_s[crows[0], :] = r[0, :CHUNK, :] + au_s[0, crows[0], :]
        ob_s[crows[1], :] = r[1, :CHUNK, :] + au_s[1, crows[1], :]
        gl = jnp.stack([gl_s[dirn, cs_[dirn]][0:1, :] for dirn in range(2)])
        nn = jnp.stack([n_s[dirn, cs_[dirn]] for dirn in range(2)])
        return gl * st - r[:, CHUNK:, :] + nn

    lax.fori_loop(0, n_c, step, jnp.zeros((2, GDN_DK, GDN_DV), F32))

    o = of_s[...] + ob_s[...]
    o = o * lax.rsqrt(jnp.mean(o * o, axis=-1, keepdims=True) + EPS) * ng_ref[...]
    o_ref[...] = o * _silu(z_ref[...])


def _gdn_mixer(p, lc, conv_w, a_log, dt_bias, norm_g):
    b, lt, _ = p.shape
    n_c = lt // CHUNK
    nh = GDN_HEADS
    kern = functools.partial(_gdn_kernel, lc=lc, lt=lt)
    pcol = lambda base: pl.BlockSpec((None, lt, LANES), lambda i, h, base=base: (i, 0, base + h))
    wcol = lambda base: pl.BlockSpec((CONV_W, LANES), lambda i, h, base=base: (0, base + h))
    vec = pl.BlockSpec((1, LANES), lambda i, h: (0, 0))
    pad = jnp.zeros((LANES - 2 * nh,), F32)
    al = jnp.concatenate([a_log.reshape(-1), pad]).reshape(1, LANES)
    dtb = jnp.concatenate([dt_bias.reshape(-1), pad]).reshape(1, LANES)
    seq = lambda dt: pltpu.VMEM((lt, LANES), dt)
    return pl.pallas_call(
        kern,
        grid=(b, nh),
        in_specs=[pcol(P_QKV // LANES), pcol(P_QKV // LANES + nh), pcol(P_QKV // LANES + 2 * nh),
                  pcol(P_GZ // LANES),
                  pl.BlockSpec((None, lt, LANES), lambda i, h: (i, 0, P_SMALL // LANES)),
                  wcol(0), wcol(nh), wcol(2 * nh), vec, vec, vec],
        out_specs=pl.BlockSpec((None, lt, LANES), lambda i, h: (i, 0, h)),
        out_shape=jax.ShapeDtypeStruct((b, lt, GDN_WIDTH), F32),
        scratch_shapes=[seq(F32), seq(F32), seq(F32), seq(F32), seq(F32), seq(F32), seq(F32),
                        pltpu.VMEM((2, n_c, GDN_DK, GDN_DK), BF16),
                        pltpu.VMEM((2, n_c, GDN_DK, GDN_DV), F32),
                        pltpu.VMEM((2, lt, GDN_DK), BF16),
                        pltpu.VMEM((2, lt, GDN_DV), F32),
                        pltpu.VMEM((2, n_c, SUBLANES, LANES), F32),
                        seq(F32), seq(F32)],
        compiler_params=_cparams(("parallel", "arbitrary")),
        name="gdn_mixer",
    )(p, p, p, p, p, conv_w, conv_w, conv_w, al, dtb, norm_g.reshape(1, LANES))


M2_GW = M2_HPG * M2_HEADDIM
M2_DT_LANE = 4 * GDN_HEADS
M2_KPAD = MXU_TILE


def _ssd_kernel(x_ref, b_ref, c_ref, z_ref, sm_ref, wx_ref, wb_ref, wc_ref, bx_ref, bb_ref, bc_ref,
                al_ref, dtb_ref, dsk_ref, ng_ref, o_ref, xs, bs, cs, dt_s, ac_s, tr_s, lhs_s, rhs_s, h_s,
                *, lc, lt):
    g = pl.program_id(1)
    n_c, n_cc = lt // CHUNK, lc // CHUNK
    row = lax.broadcasted_iota(jnp.int32, (lt, 1), 0)
    lane = lax.broadcasted_iota(jnp.int32, (1, LANES), 1)
    head_of_lane = lax.broadcasted_iota(jnp.int32, (1, M2_GW), 1) // M2_HEADDIM

    xs[...] = _dwconv_silu(x_ref[...], wx_ref, bx_ref[...], row, lc, lt)
    bs[...] = _dwconv_silu(b_ref[...], wb_ref, bb_ref[...], row, lc, lt)
    cs[...] = _dwconv_silu(c_ref[...], wc_ref, bc_ref[...], row, lc, lt)

    dt_all = jax.nn.softplus(sm_ref[...] + dtb_ref[...])
    dt_s[...] = dt_all
    ac_s[...] = _chunk_cumsum(dt_all * (-jnp.exp(al_ref[...])), row, lane < M2_DT_LANE + M2_HEADS)

    o_ref[...] = jnp.zeros_like(o_ref)
    h_s[...] = jnp.zeros_like(h_s)
    lhs_s[...] = jnp.zeros_like(lhs_s)
    rhs_s[...] = jnp.zeros_like(rhs_s)

    ii = lax.broadcasted_iota(jnp.int32, (CHUNK, CHUNK), 0)
    jj = lax.broadcasted_iota(jnp.int32, (CHUNK, CHUNK), 1)

    def step(s, carry):
        cs_ = (s, _bwd_chunk(s, n_cc, n_c))
        rws = [pl.ds(pl.multiple_of(c * CHUNK, CHUNK), CHUNK) for c in cs_]
        bc2 = jnp.stack([bs[rws[d], :] for d in range(2)]).astype(BF16)
        cf_ = [cs[rws[d], :] for d in range(2)]
        cb2 = lax.dot_general(jnp.stack(cf_).astype(BF16), bc2, (((2,), (2,)), ((0,), (0,))),
                              preferred_element_type=F32)
        hprev = h_s[...]
        xd_, tot_ = [], []
        for dirn in range(2):
            incl = (ii >= jj) if dirn == 0 else (ii <= jj)
            dtc = dt_s[rws[dirn], :]
            acc = ac_s[rws[dirn], :]
            tr_s[dirn] = jnp.transpose(acc)
            dte = jnp.zeros((CHUNK, M2_GW), F32)
            ace = jnp.zeros((CHUNK, M2_GW), F32)
            for j in range(M2_HPG):
                l_idx = M2_DT_LANE + dirn * M2_HEADS + g * M2_HPG + j
                a_col = _col(acc, lane, l_idx)
                hm = head_of_lane == j
                dte = jnp.where(hm, _col(dtc, lane, l_idx), dte)
                ace = jnp.where(hm, a_col, ace)
                a_row = tr_s[dirn, pl.ds(l_idx, 1), :]
                dec = jnp.where(incl, jnp.exp(jnp.where(incl, a_col - a_row, 0.0)), 0.0)
                blk = pl.ds(j * CHUNK, CHUNK)
                lhs_s[dirn, blk, 0:M2_STATE] = (jnp.exp(a_col) * cf_[dirn]).astype(BF16)
                lhs_s[dirn, blk, M2_STATE:M2_STATE + CHUNK] = (cb2[dirn] * dec).astype(BF16)
            tot = ace[CHUNK - 1:CHUNK, :] if dirn == 0 else ace[0:1, :]
            xdt = xs[rws[dirn], :] * dte
            rhs_s[dirn, 0:M2_STATE, :] = hprev[dirn].astype(BF16)
            rhs_s[dirn, M2_STATE:M2_STATE + CHUNK, :] = xdt.astype(BF16)
            xd_.append((xdt * jnp.exp(tot - ace)).astype(BF16))
            tot_.append(tot)
        r2 = lax.dot_general(lhs_s[...], rhs_s[...], (((2,), (1,)), ((0,), (0,))),
                             preferred_element_type=F32)
        for dirn in range(2):
            y = jnp.where(head_of_lane == 0, r2[dirn, 0:CHUNK, :], 0.0)
            for j in range(1, M2_HPG):
                y = y + jnp.where(head_of_lane == j, r2[dirn, j * CHUNK:(j + 1) * CHUNK, :], 0.0)
            o_ref[rws[dirn], :] += y
        st2 = lax.dot_general(bc2, jnp.stack(xd_), (((1,), (1,)), ((0,), (0,))),
                              preferred_element_type=F32)
        h_s[...] = jnp.exp(jnp.stack(tot_)) * hprev + st2
        return carry

    lax.fori_loop(0, n_c, step, 0, unroll=_largest_divisor(n_c, 2))

    def fin(c, carry):
        rows = pl.ds(pl.multiple_of(c * CHUNK, CHUNK), CHUNK)
        y = (o_ref[rows, :] + dsk_ref[...] * xs[rows, :]) * _silu(z_ref[rows, :])
        o_ref[rows, :] = y * lax.rsqrt(jnp.mean(y * y, axis=-1, keepdims=True) + EPS) * ng_ref[...]
        return carry

    lax.fori_loop(0, n_c, fin, 0)


def _ssd_mixer(p, lc, conv_w, conv_b, a_log, dt_bias, d_skip, norm_g):
    b, lt, _ = p.shape
    kern = functools.partial(_ssd_kernel, lc=lc, lt=lt)
    gw = M2_GW
    bcol = (P_XBC + M2_WIDTH) // LANES
    ccol = bcol + M2_GROUPS * M2_STATE // LANES
    wide = lambda base: pl.BlockSpec((None, lt, gw), lambda i, g, base=base: (i, 0, base + g))
    narrow = lambda base: pl.BlockSpec((None, lt, LANES), lambda i, g, base=base: (i, 0, base + g))
    wwide = lambda rows: pl.BlockSpec((rows, gw), lambda i, g: (0, g))
    wnar = lambda rows, base: pl.BlockSpec((rows, LANES), lambda i, g, base=base: (0, base + g))
    vec = pl.BlockSpec((1, LANES), lambda i, g: (0, 0))
    lead = jnp.zeros((M2_DT_LANE,), F32)
    tail = jnp.zeros((LANES - M2_DT_LANE - 2 * M2_HEADS,), F32)
    al = jnp.concatenate([lead, a_log.reshape(-1), tail]).reshape(1, LANES)
    dtb = jnp.concatenate([lead, dt_bias.reshape(-1), tail]).reshape(1, LANES)
    cbias = conv_b.reshape(1, M2_CONV_CH)
    wb0 = M2_WIDTH // LANES
    wc0 = wb0 + M2_GROUPS * M2_STATE // LANES
    return pl.pallas_call(
        kern,
        grid=(b, M2_GROUPS),
        in_specs=[wide(P_XBC // gw), narrow(bcol), narrow(ccol), wide(P_MZ // gw),
                  pl.BlockSpec((None, lt, LANES), lambda i, g: (i, 0, P_SMALL // LANES)),
                  wwide(CONV_W), wnar(CONV_W, wb0), wnar(CONV_W, wc0),
                  wwide(1), wnar(1, wb0), wnar(1, wc0),
                  vec, vec, wwide(1), wwide(1)],
        out_specs=pl.BlockSpec((None, lt, gw), lambda i, g: (i, 0, g)),
        out_shape=jax.ShapeDtypeStruct((b, lt, M2_WIDTH), F32),
        scratch_shapes=[pltpu.VMEM((lt, gw), F32), pltpu.VMEM((lt, LANES), F32), pltpu.VMEM((lt, LANES), F32),
                        pltpu.VMEM((lt, LANES), F32), pltpu.VMEM((lt, LANES), F32),
                        pltpu.VMEM((2, LANES, CHUNK), F32),
                        pltpu.VMEM((2, M2_HPG * CHUNK, M2_KPAD), BF16),
                        pltpu.VMEM((2, M2_KPAD, gw), BF16),
                        pltpu.VMEM((2, M2_STATE, gw), F32)],
        compiler_params=_cparams(("parallel", "arbitrary")),
        name="ssd_mixer",
    )(p, p, p, p, p, conv_w, conv_w, conv_w, cbias, cbias, cbias, al, dtb,
      jnp.repeat(d_skip, M2_HEADDIM).reshape(1, M2_WIDTH), norm_g.reshape(1, M2_WIDTH))


def _out_proj_kernel(a_ref, b_ref, m_ref, x_ref, mod_ref, w_ref, o_ref, *, lc, tm):
    r = pl.program_id(1)
    acc = jnp.dot(a_ref[...].astype(BF16), w_ref[0:S5_WIDTH, :], preferred_element_type=F32)
    acc += jnp.dot(b_ref[...].astype(BF16), w_ref[S5_WIDTH:S5_WIDTH + GDN_WIDTH, :],
                   preferred_element_type=F32)
    acc += jnp.dot(m_ref[...].astype(BF16), w_ref[S5_WIDTH + GDN_WIDTH:, :],
                   preferred_element_type=F32)
    row = r * tm + lax.broadcasted_iota(jnp.int32, (tm, 1), 0)
    o_ref[...] = x_ref[...] + _mod_vec(mod_ref, 2, row < lc) * acc


def _out_proj(a, bm, m, x, mod, w_out_bf, lc):
    b, lt, d = x.shape
    tm = lt // 4 if (lt // 4) % 8 == 0 and lc <= lt // 4 else lt // 2
    kern = functools.partial(_out_proj_kernel, lc=lc, tm=tm)
    tok = lambda w: pl.BlockSpec((None, tm, w), lambda i, r: (i, r, 0))
    return pl.pallas_call(
        kern,
        grid=(b, lt // tm),
        in_specs=[tok(S5_WIDTH), tok(GDN_WIDTH), tok(M2_WIDTH), tok(d),
                  pl.BlockSpec((None, 2, 6, d), lambda i, r: (i, 0, 0, 0)),
                  pl.BlockSpec((MIX_WIDTH, d), lambda i, r: (0, 0))],
        out_specs=tok(d),
        out_shape=jax.ShapeDtypeStruct((b, lt, d), F32),
        compiler_params=_cparams(("parallel", "parallel")),
        name="out_proj",
    )(a, bm, m, x, mod, w_out_bf)


def _swiglu_acc(h, wg_ref, wu_ref, wd_ref):
    acc = jnp.zeros((h.shape[0], D_MODEL), F32)
    for f in range(FF_NF):
        cols = slice(f * FF_TF, (f + 1) * FF_TF)
        g = jnp.dot(h, wg_ref[:, cols], preferred_element_type=F32)
        u = jnp.dot(h, wu_ref[:, cols], preferred_element_type=F32)
        acc = acc + jnp.dot((_silu(g) * u).astype(BF16), wd_ref[cols, :], preferred_element_type=F32)
    return acc


def _ffn_dense_kernel(x_ref, mod_ref, g_ref, wg_ref, wu_ref, wd_ref, o_ref, *, lc, tm):
    r = pl.program_id(1)
    x = x_ref[...]
    row = r * tm + lax.broadcasted_iota(jnp.int32, (tm, 1), 0)
    is_ctx = row < lc
    h = (_rms(x) * g_ref[...]) * (1.0 + _mod_vec(mod_ref, 4, is_ctx)) + _mod_vec(mod_ref, 3, is_ctx)
    acc = _swiglu_acc(h.astype(BF16), wg_ref, wu_ref, wd_ref)
    o_ref[...] = x + _mod_vec(mod_ref, 5, is_ctx) * acc


def _tok_tile(lc, l):
    return math.gcd(math.gcd(lc, l), 256)


def _ffn_dense(x, mod, g, wg, wu, wd, lc):
    b, lt, d = x.shape
    tm = lt // 3 if lt % (3 * SUBLANES) == 0 else _tok_tile(lc, lt - lc)
    kern = functools.partial(_ffn_dense_kernel, lc=lc, tm=tm)
    tok = pl.BlockSpec((None, tm, d), lambda i, r: (i, r, 0))
    wspec = lambda shp: pl.BlockSpec(shp, lambda i, r: (0, 0))
    return pl.pallas_call(
        kern,
        grid=(b, lt // tm),
        in_specs=[tok, pl.BlockSpec((None, 2, 6, d), lambda i, r: (i, 0, 0, 0)),
                  pl.BlockSpec((1, d), lambda i, r: (0, 0)),
                  wspec((d, D_FF)), wspec((d, D_FF)), wspec((D_FF, d))],
        out_specs=tok,
        out_shape=jax.ShapeDtypeStruct((b, lt, d), F32),
        compiler_params=_cparams(("parallel", "parallel")),
        name="ffn_dense",
    )(x, mod, g.reshape(1, d), wg.astype(BF16), wu.astype(BF16), wd.astype(BF16))


def _router_kernel(x_ref, mod_ref, g_ref, rw_ref, h_ref, rt_ref, *, lc, tm):
    r = pl.program_id(1)
    x = x_ref[...]
    row = r * tm + lax.broadcasted_iota(jnp.int32, (tm, 1), 0)
    is_ctx = row < lc
    h = (_rms(x) * g_ref[...]) * (1.0 + _mod_vec(mod_ref, 4, is_ctx)) + _mod_vec(mod_ref, 3, is_ctx)
    h_ref[...] = h
    logits = jnp.dot(h, rw_ref[...], preferred_element_type=F32, precision=lax.Precision.HIGHEST)
    lane = lax.broadcasted_iota(jnp.int32, logits.shape, 1)
    neg = jnp.float32(-jnp.inf)
    lg = jnp.where(lane < N_EXPERTS, logits, neg)
    m1 = jnp.max(lg, axis=-1, keepdims=True)
    i1 = jnp.min(jnp.where(lg == m1, lane, LANES), axis=-1, keepdims=True)
    lg2 = jnp.where(lane == i1, neg, lg)
    m2 = jnp.max(lg2, axis=-1, keepdims=True)
    i2 = jnp.min(jnp.where(lg2 == m2, lane, LANES), axis=-1, keepdims=True)
    e2 = jnp.exp(m2 - m1)
    den = 1.0 + e2
    out = jnp.where(lane == 0, i1.astype(F32), 0.0)
    out = jnp.where(lane == 1, i2.astype(F32), out)
    out = jnp.where(lane == 2, 1.0 / den, out)
    out = jnp.where(lane == 3, e2 / den, out)
    rt_ref[...] = out


def _router(x, mod, g, router_w, lc):
    b, lt, d = x.shape
    tm = _tok_tile(lc, lt - lc)
    rw = jnp.concatenate([router_w, jnp.zeros((d, LANES - N_EXPERTS), F32)], axis=1)
    kern = functools.partial(_router_kernel, lc=lc, tm=tm)
    tok = lambda w: pl.BlockSpec((None, tm, w), lambda i, r: (i, r, 0))
    return pl.pallas_call(
        kern,
        grid=(b, lt // tm),
        in_specs=[tok(d), pl.BlockSpec((None, 2, 6, d), lambda i, r: (i, 0, 0, 0)),
                  pl.BlockSpec((1, d), lambda i, r: (0, 0)),
                  pl.BlockSpec((d, LANES), lambda i, r: (0, 0))],
        out_specs=[tok(d), tok(LANES)],
        out_shape=[jax.ShapeDtypeStruct((b, lt, d), F32), jax.ShapeDtypeStruct((b, lt, LANES), F32)],
        compiler_params=_cparams(("parallel", "parallel")),
        name="moe_router",
    )(x, mod, g.reshape(1, d), rw)


def _row_copy(src_hbm, dst, idx, r, sem):
    return pltpu.make_async_copy(src_hbm.at[pl.ds(idx, 1), :], dst.at[pl.ds(r, 1), :], sem)


def _gather_rows_kernel(idx_ref, src_hbm, o_ref, sem, *, tg):
    def start(r, c):
        _row_copy(src_hbm, o_ref, idx_ref[r], r, sem).start()
        return c
    lax.fori_loop(0, tg, start, 0)

    def wait(r, c):
        _row_copy(src_hbm, o_ref, 0, r, sem).wait()
        return c
    lax.fori_loop(0, tg, wait, 0)


def _gather_rows(src, idx, tg):
    n_out = idx.shape[0]
    d = src.shape[1]
    kern = functools.partial(_gather_rows_kernel, tg=tg)
    return pl.pallas_call(
        kern,
        grid=(n_out // tg,),
        in_specs=[pl.BlockSpec((tg,), lambda i: (i,), memory_space=pltpu.SMEM),
                  pl.BlockSpec(memory_space=pl.ANY)],
        out_specs=pl.BlockSpec((tg, d), lambda i: (i, 0)),
        out_shape=jax.ShapeDtypeStruct((n_out, d), F32),
        scratch_shapes=[pltpu.SemaphoreType.DMA(())],
        compiler_params=_cparams(("arbitrary",)),
        name="moe_gather",
    )(idx, src)


def _expert_kernel(te_ref, nt_ref, x_ref, wg_ref, wu_ref, wd_ref, o_ref):
    @pl.when(pl.program_id(0) < nt_ref[0])
    def _():
        o_ref[...] = _swiglu_acc(x_ref[...].astype(BF16), wg_ref, wu_ref, wd_ref)

    @pl.when(pl.program_id(0) >= nt_ref[0])
    def _():
        o_ref[...] = jnp.zeros_like(o_ref)


def _experts(xs, tile_expert, n_tiles_used, wg, wu, wd):
    rp, d = xs.shape
    wspec = lambda shp: pl.BlockSpec((None,) + shp, lambda t, te, nt: (te[t], 0, 0))
    grid_spec = pltpu.PrefetchScalarGridSpec(
        num_scalar_prefetch=2,
        grid=(rp // MOE_TM,),
        in_specs=[pl.BlockSpec((MOE_TM, d), lambda t, te, nt: (t, 0)),
                  wspec((d, D_FF)), wspec((d, D_FF)), wspec((D_FF, d))],
        out_specs=pl.BlockSpec((MOE_TM, d), lambda t, te, nt: (t, 0)),
    )
    return pl.pallas_call(
        _expert_kernel,
        grid_spec=grid_spec,
        out_shape=jax.ShapeDtypeStruct((rp, d), F32),
        compiler_params=_cparams(("arbitrary",)),
        name="moe_experts",
    )(tile_expert, n_tiles_used, xs, wg.astype(BF16), wu.astype(BF16), wd.astype(BF16))


def _combine_kernel(p1_ref, p2_ref, y_hbm, x_ref, rt_ref, mod_ref, o_ref, ya, yb, sem, *, lc, tm):
    r = pl.program_id(1)

    def start(k, c):
        _row_copy(y_hbm, ya, p1_ref[k], k, sem).start()
        _row_copy(y_hbm, yb, p2_ref[k], k, sem).start()
        return c
    lax.fori_loop(0, tm, start, 0)

    def wait(k, c):
        _row_copy(y_hbm, ya, 0, k, sem).wait()
        _row_copy(y_hbm, yb, 0, k, sem).wait()
        return c
    lax.fori_loop(0, tm, wait, 0)

    row = r * tm + lax.broadcasted_iota(jnp.int32, (tm, 1), 0)
    rt = rt_ref[...]
    y = rt[:, 2:3] * ya[...] + rt[:, 3:4] * yb[...]
    o_ref[...] = x_ref[...] + _mod_vec(mod_ref, 5, row < lc) * y


def _combine(p1, p2, y, x, rt, mod, lc):
    b, lt, d = x.shape
    tm = _tok_tile(lc, lt - lc)
    nt = lt // tm
    kern = functools.partial(_combine_kernel, lc=lc, tm=tm)
    tok = lambda w: pl.BlockSpec((None, tm, w), lambda i, r: (i, r, 0))
    ispec = pl.BlockSpec((tm,), lambda i, r: (i * nt + r,), memory_space=pltpu.SMEM)
    return pl.pallas_call(
        kern,
        grid=(b, nt),
        in_specs=[ispec, ispec, pl.BlockSpec(memory_space=pl.ANY), tok(d), tok(LANES),
                  pl.BlockSpec((None, 2, 6, d), lambda i, r: (i, 0, 0, 0))],
        out_specs=tok(d),
        out_shape=jax.ShapeDtypeStruct((b, lt, d), F32),
        scratch_shapes=[pltpu.VMEM((tm, d), F32), pltpu.VMEM((tm, d), F32),
                        pltpu.SemaphoreType.DMA(())],
        compiler_params=_cparams(("arbitrary", "arbitrary")),
        name="moe_combine",
    )(p1, p2, y, x, rt, mod)


def _moe(x, mod, g, router_w, wg, wu, wd, lc):
    b, lt, d = x.shape
    n = b * lt
    h2, rt = _router(x, mod, g, router_w, lc)
    rt2 = rt.reshape(n, LANES)
    e_flat = jnp.concatenate([rt2[:, 0], rt2[:, 1]]).astype(jnp.int32)
    onehot = (e_flat[:, None] == jnp.arange(N_EXPERTS, dtype=jnp.int32)[None, :]).astype(jnp.int32)
    counts = jnp.sum(onehot, axis=0)
    rank = jnp.sum((jnp.cumsum(onehot, axis=0) - 1) * onehot, axis=1)
    padded = ((counts + MOE_TM - 1) // MOE_TM) * MOE_TM
    ends = jnp.cumsum(padded)
    pos = (ends - padded)[e_flat] + rank
    rp = ((2 * n + MOE_TM - 1) // MOE_TM + N_EXPERTS) * MOE_TM
    tok_id = jnp.concatenate([jnp.arange(n, dtype=jnp.int32)] * 2)
    src_row = jnp.zeros((rp,), jnp.int32).at[pos].set(tok_id)
    tile_start = jnp.arange(rp // MOE_TM, dtype=jnp.int32) * MOE_TM
    tile_expert = jnp.minimum(jnp.sum(tile_start[:, None] >= ends[None, :], axis=1),
                              N_EXPERTS - 1).astype(jnp.int32)
    n_used = (ends[-1] // MOE_TM).astype(jnp.int32).reshape(1)
    xs = _gather_rows(h2.reshape(n, d), src_row, MOE_TM)
    ys = _experts(xs, tile_expert, n_used, wg, wu, wd)
    return _combine(pos[:n].astype(jnp.int32), pos[n:].astype(jnp.int32), ys, x, rt, mod, lc)


def _final_norm_kernel(x_ref, g_ref, o_ref):
    o_ref[...] = _rms(x_ref[...]) * g_ref[...]


def _final_norm(x, g):
    b, l, d = x.shape
    tm = math.gcd(l, 512)
    tok = pl.BlockSpec((None, tm, d), lambda i, r: (i, r, 0))
    return pl.pallas_call(
        _final_norm_kernel,
        grid=(b, l // tm),
        in_specs=[tok, pl.BlockSpec((1, d), lambda i, r: (0, 0))],
        out_specs=tok,
        out_shape=jax.ShapeDtypeStruct((b, l, d), F32),
        compiler_params=_cparams(("parallel", "parallel")),
        name="final_norm",
    )(x, g.reshape(1, d))


def _grid_t(xl, rows, cols):
    b, l, ch = xl.shape
    return xl.reshape(b, rows, cols, ch).transpose(0, 2, 1, 3).reshape(b, l, ch)


def kernel(x, c, ctx, c_ctx, ada_w, ada_b, norm1_g, norm2_g, w_in, w_out,
           s5_lam_re, s5_lam_im, s5_b_re, s5_b_im, s5_c_re, s5_c_im, s5_log_dt, s5_d, s5_w_glu,
           gdn_conv_w, gdn_a_log, gdn_dt_bias, gdn_norm_g,
           m2_conv_w, m2_conv_b, m2_a_log, m2_dt_bias, m2_d, m2_norm_g,
           ffn_w_gate, ffn_w_up, ffn_w_down,
           moe_router, moe_w_gate, moe_w_up, moe_w_down, final_norm_g):
    b, l, d = x.shape
    lc = ctx.shape[1]
    lt = lc + l
    depth = ada_w.shape[0]
    rows = l // GRID_W
    assert b % SUBLANES == 0 and lc % CHUNK == 0 and l % CHUNK == 0

    cpad = jnp.zeros((b + SUBLANES, d), F32).at[:b].set(c).at[b].set(c_ctx)
    mods = _ada_all(cpad, ada_w, ada_b).reshape(depth, b + SUBLANES, 6, d)

    xt = jnp.concatenate([ctx, x], axis=1)
    col_major = False
    for i in range(depth):
        want_cm = i % 2 == 1
        if want_cm != col_major:
            lat = xt[:, lc:]
            lat = _grid_t(lat, rows, GRID_W) if want_cm else _grid_t(lat, GRID_W, rows)
            xt = jnp.concatenate([xt[:, :lc], lat], axis=1)
            col_major = want_cm
        mod = jnp.stack([jnp.broadcast_to(mods[i, b][None], (b, 6, d)), mods[i, :b]], axis=1)

        p = _in_proj(xt, mod, norm1_g[i], _pack_w_in(w_in[i]), lc)
        u_tb = jnp.transpose(p[:, :, P_U:P_U + S5_WIDTH], (1, 0, 2)).reshape(lt * b, S5_WIDTH)
        s5_par = _s5_params(s5_lam_re[i], s5_lam_im[i], s5_b_re[i], s5_b_im[i], s5_c_re[i], s5_c_im[i],
                            s5_log_dt[i])
        ya = _s5_mixer(u_tb, lc, b, s5_par, s5_d[i], s5_w_glu[i])
        ya = jnp.transpose(ya.reshape(lt, b, S5_WIDTH), (1, 0, 2))
        yb = _gdn_mixer(p, lc, gdn_conv_w[i], gdn_a_log[i], gdn_dt_bias[i], gdn_norm_g[i])
        ym = _ssd_mixer(p, lc, m2_conv_w[i], m2_conv_b[i], m2_a_log[i], m2_dt_bias[i], m2_d[i],
                        m2_norm_g[i])
        xt = _out_proj(ya, yb, ym, xt, mod, w_out[i].astype(BF16), lc)

        j = i // 2
        if i % 2 == 0:
            xt = _ffn_dense(xt, mod, norm2_g[i], ffn_w_gate[j], ffn_w_up[j], ffn_w_down[j], lc)
        else:
            xt = _moe(xt, mod, norm2_g[i], moe_router[j], moe_w_gate[j], moe_w_up[j], moe_w_down[j], lc)

    lat = xt[:, lc:]
    if col_major:
        lat = _grid_t(lat, GRID_W, rows)
    return _final_norm(lat, final_norm_g)
```

```python
import functools
import math

import jax
import jax.numpy as jnp
from jax import lax
from jax.experimental import pallas as pl
from jax.experimental.pallas import tpu as pltpu

F32 = jnp.float32
BF16 = jnp.bfloat16

D_MODEL = 1024
GRID_W = 64
EPS = 1e-6
CHUNK = 64
CONV_W = 5
S5_WIDTH = D_MODEL // 2
S5_GROUP = 16
S5_GROUPS = S5_WIDTH // S5_GROUP
S5_STATE = 64
GDN_DK = 128
GDN_DV = 128
GDN_WIDTH = D_MODEL // 2
GDN_HEADS = GDN_WIDTH // GDN_DV
GDN_CONV_CH = 2 * GDN_HEADS * GDN_DK + GDN_WIDTH
M2_WIDTH = D_MODEL
M2_HEADDIM = 64
M2_HEADS = M2_WIDTH // M2_HEADDIM
M2_GROUPS = 2
M2_HPG = M2_HEADS // M2_GROUPS
M2_STATE = 128
M2_CONV_CH = M2_WIDTH + 2 * M2_GROUPS * M2_STATE
MIX_WIDTH = S5_WIDTH + GDN_WIDTH + M2_WIDTH
IN_SIZES = (S5_WIDTH, GDN_CONV_CH, GDN_WIDTH, 2 * GDN_HEADS, 2 * GDN_HEADS,
            M2_WIDTH, M2_CONV_CH, 2 * M2_HEADS)
D_FF = 256 * ((8 * D_MODEL // 3 + 255) // 256)
N_EXPERTS = 8

LANES = 128
SUBLANES = 8
MXU_TILE = 256
VMEM_LIMIT = 56 * 1024 * 1024

P_U = 0
P_GZ = P_U + S5_WIDTH
P_QKV = P_GZ + GDN_WIDTH
P_MZ = P_QKV + GDN_CONV_CH
P_XBC = P_MZ + M2_WIDTH
P_SMALL = P_XBC + M2_CONV_CH
IN_TN = 896
NP = 6 * IN_TN
assert P_SMALL + LANES <= NP

FF_TF = 256
FF_NF = D_FF // FF_TF
MOE_TM = 512


def _cparams(sem):
    return pltpu.CompilerParams(dimension_semantics=sem, vmem_limit_bytes=VMEM_LIMIT)


def _silu(x):
    return x * jax.nn.sigmoid(x)


def _rms(x):
    return x * lax.rsqrt(jnp.mean(x * x, axis=-1, keepdims=True) + EPS)


def _mod_vec(mod_ref, k, is_ctx):
    return jnp.where(is_ctx, mod_ref[0, k:k + 1, :], mod_ref[1, k:k + 1, :])


def _largest_divisor(n, cap):
    return max(u for u in range(1, cap + 1) if n % u == 0)


_TDOT = (((0,), (0,)), ((), ()))
_NTDOT = (((1,), (1,)), ((), ()))


def _bdot(a, b):
    return jnp.dot(a.astype(BF16), b.astype(BF16), preferred_element_type=F32)


def _bbdot(a, b):
    return lax.dot_general(a.astype(BF16), b.astype(BF16), (((2,), (1,)), ((0,), (0,))),
                           preferred_element_type=F32)


def _ada_kernel(c_ref, w_ref, b_ref, o_ref):
    o_ref[...] = _bdot(_silu(c_ref[...]), w_ref[...]) + b_ref[...]


def _ada_all(cpad, ada_w, ada_b):
    depth, d, n6 = ada_w.shape
    tn = 1536
    rows = cpad.shape[0]
    return pl.pallas_call(
        _ada_kernel,
        grid=(depth, n6 // tn),
        in_specs=[pl.BlockSpec((rows, d), lambda i, j: (0, 0)),
                  pl.BlockSpec((None, d, tn), lambda i, j: (i, 0, j)),
                  pl.BlockSpec((None, 1, tn), lambda i, j: (i, 0, j))],
        out_specs=pl.BlockSpec((None, rows, tn), lambda i, j: (i, 0, j)),
        out_shape=jax.ShapeDtypeStruct((depth, rows, n6), F32),
        compiler_params=_cparams(("parallel", "parallel")),
        name="ada_mod",
    )(cpad, ada_w, ada_b.reshape(depth, 1, n6))


def _in_proj_kernel(x_ref, mod_ref, g_ref, w_ref, o_ref, h_scr, *, lc, tm, rc):
    r = pl.program_id(1)

    @pl.when(pl.program_id(2) == 0)
    def _():
        def body(c, carry):
            r0 = pl.multiple_of(c * rc, rc)
            x = x_ref[pl.ds(r0, rc), :]
            row = r * tm + r0 + lax.broadcasted_iota(jnp.int32, (rc, 1), 0)
            is_ctx = row < lc
            h = (_rms(x) * g_ref[...]) * (1.0 + _mod_vec(mod_ref, 1, is_ctx)) + _mod_vec(mod_ref, 0, is_ctx)
            h_scr[pl.ds(r0, rc), :] = h.astype(BF16)
            return carry
        lax.fori_loop(0, tm // rc, body, 0)

    o_ref[...] = jnp.dot(h_scr[...], w_ref[...], preferred_element_type=F32)


def _in_proj(x, mod, g, w_packed, lc):
    b, lt, d = x.shape
    tm = lt // 2
    rc = math.gcd(tm, 128)
    assert lc <= tm and tm % 16 == 0
    kern = functools.partial(_in_proj_kernel, lc=lc, tm=tm, rc=rc)
    return pl.pallas_call(
        kern,
        grid=(b, 2, NP // IN_TN),
        in_specs=[pl.BlockSpec((None, tm, d), lambda i, r, j: (i, r, 0)),
                  pl.BlockSpec((None, 2, 6, d), lambda i, r, j: (i, 0, 0, 0)),
                  pl.BlockSpec((1, d), lambda i, r, j: (0, 0)),
                  pl.BlockSpec((d, IN_TN), lambda i, r, j: (0, j))],
        out_specs=pl.BlockSpec((None, tm, IN_TN), lambda i, r, j: (i, r, j)),
        out_shape=jax.ShapeDtypeStruct((b, lt, NP), F32),
        scratch_shapes=[pltpu.VMEM((tm, d), BF16)],
        compiler_params=_cparams(("parallel", "parallel", "arbitrary")),
        name="in_proj",
    )(x, mod, g.reshape(1, d), w_packed)


def _pack_w_in(w):
    o = [0]
    for s in IN_SIZES:
        o.append(o[-1] + s)
    u, qkv, gz, al, be, mz, xbc, dt = [w[:, o[i]:o[i + 1]] for i in range(8)]
    small_pad = jnp.zeros((w.shape[0], LANES - 4 * GDN_HEADS - 2 * M2_HEADS), w.dtype)
    tail = jnp.zeros((w.shape[0], NP - P_SMALL - LANES), w.dtype)
    return jnp.concatenate([u, gz, qkv, mz, xbc, al, be, dt, small_pad, tail], axis=1).astype(BF16)


S5_HALF = S5_WIDTH // 2
S5_HS = (S5_GROUPS // 2) * S5_STATE
S5_LQ = 512


def _s5_scan_kernel(u_ref, bk_ref, ck_ref, a_ref, y_ref, s_scr, h_scr, *, tt, nb):
    d = pl.program_id(0)

    @pl.when(pl.program_id(1) == 0)
    def _():
        h_scr[...] = jnp.zeros_like(h_scr)

    u = u_ref[...].astype(BF16)
    for k in range(2):
        s_scr[:, k * 2 * S5_HS:(k + 1) * 2 * S5_HS] = jnp.dot(
            u[:, k * S5_HALF:(k + 1) * S5_HALF], bk_ref[k], preferred_element_type=F32)

    for k in range(2):
        for q in range(S5_HS // S5_LQ):
            re0 = k * 2 * S5_HS + q * S5_LQ
            im0 = re0 + S5_HS
            a_re = jnp.broadcast_to(a_ref[2 * k:2 * k + 1, q * S5_LQ:(q + 1) * S5_LQ], (nb, S5_LQ))
            a_im = jnp.broadcast_to(a_ref[2 * k + 1:2 * k + 2, q * S5_LQ:(q + 1) * S5_LQ], (nb, S5_LQ))

            def step(i, carry, re0=re0, im0=im0, a_re=a_re, a_im=a_im):
                hr, hi = carry
                t = i + d * (tt - 1 - 2 * i)
                r = pl.multiple_of(t * nb, nb)
                nr = a_re * hr - a_im * hi + s_scr[pl.ds(r, nb), re0:re0 + S5_LQ]
                ni = a_re * hi + a_im * hr + s_scr[pl.ds(r, nb), im0:im0 + S5_LQ]
                s_scr[pl.ds(r, nb), re0:re0 + S5_LQ] = nr
                s_scr[pl.ds(r, nb), im0:im0 + S5_LQ] = ni
                return nr, ni

            hr, hi = lax.fori_loop(0, tt, step, (h_scr[:, re0:re0 + S5_LQ], h_scr[:, im0:im0 + S5_LQ]),
                                   unroll=4)
            h_scr[:, re0:re0 + S5_LQ] = hr
            h_scr[:, im0:im0 + S5_LQ] = hi

    for k in range(2):
        y_ref[:, k * S5_HALF:(k + 1) * S5_HALF] = jnp.dot(
            s_scr[:, k * 2 * S5_HS:(k + 1) * 2 * S5_HS].astype(BF16), ck_ref[k],
            preferred_element_type=F32)


def _s5_fin_kernel(y_ref, u_ref, d_ref, w_ref, o_ref):
    y = y_ref[0] + y_ref[1] + d_ref[...] * u_ref[...]
    y = jax.nn.gelu(y)
    o_ref[...] = y * jax.nn.sigmoid(jnp.dot(y.astype(BF16), w_ref[...], preferred_element_type=F32))


def _s5_params(lam_re, lam_im, b_re, b_im, c_re, c_im, log_dt):
    lam = lax.complex(jnp.minimum(lam_re, -1e-4), lam_im)
    dt = jnp.exp(log_dt)[..., None]
    lam_bar = jnp.exp(lam * dt)
    gamma = (lam_bar - 1.0) / lam
    bt = lax.complex(b_re, b_im) * gamma[..., None]
    gh = S5_GROUPS // 2
    eye = jnp.eye(gh, dtype=F32)

    def bd_in(m):
        m = m.reshape(2, 2, gh, S5_STATE, S5_GROUP)
        return jnp.einsum("gh,dkgpi->dkgihp", eye, m).reshape(2, 2, gh * S5_GROUP, gh * S5_STATE)

    def bd_out(m):
        m = m.reshape(2, 2, gh, S5_GROUP, S5_STATE)
        return jnp.einsum("gh,dkgip->dkgphi", eye, m).reshape(2, 2, gh * S5_STATE, gh * S5_GROUP)

    bk = jnp.concatenate([bd_in(bt.real), bd_in(bt.imag)], axis=-1).astype(BF16)
    ck = jnp.concatenate([bd_out(c_re), -bd_out(c_im)], axis=-2).astype(BF16)
    lr = lam_bar.real.reshape(2, 2, 1, S5_HS)
    li = lam_bar.imag.reshape(2, 2, 1, S5_HS)
    a = jnp.concatenate([lr, li], axis=2).reshape(2, 4, S5_HS)
    return bk, ck, a


def _s5_mixer(u_tb, lc, nb, params, d_skip, w_glu):
    n, w = u_tb.shape
    lt = n // nb
    tt = CHUNK
    n_c, n_t = lc // tt, lt // tt
    bk, ck, a = params

    def tile(dd, s):
        return jnp.where(dd == 0, s, jnp.where(s < n_c, n_c - 1 - s, n_t - 1 - s + n_c))

    kern = functools.partial(_s5_scan_kernel, tt=tt, nb=nb)
    y = pl.pallas_call(
        kern,
        grid=(2, n_t),
        in_specs=[pl.BlockSpec((tt * nb, w), lambda dd, s: (tile(dd, s), 0)),
                  pl.BlockSpec((None, 2, S5_HALF, 2 * S5_HS), lambda dd, s: (dd, 0, 0, 0)),
                  pl.BlockSpec((None, 2, 2 * S5_HS, S5_HALF), lambda dd, s: (dd, 0, 0, 0)),
                  pl.BlockSpec((None, 4, S5_HS), lambda dd, s: (dd, 0, 0))],
        out_specs=pl.BlockSpec((None, tt * nb, w), lambda dd, s: (dd, tile(dd, s), 0)),
        out_shape=jax.ShapeDtypeStruct((2, n, w), F32),
        scratch_shapes=[pltpu.VMEM((tt * nb, 4 * S5_HS), F32), pltpu.VMEM((nb, 4 * S5_HS), F32)],
        compiler_params=_cparams(("arbitrary", "arbitrary")),
        name="s5_scan",
    )(u_tb, bk, ck, a)

    tm = math.gcd(n, 512)
    return pl.pallas_call(
        _s5_fin_kernel,
        grid=(n // tm,),
        in_specs=[pl.BlockSpec((2, tm, w), lambda i: (0, i, 0)),
                  pl.BlockSpec((tm, w), lambda i: (i, 0)),
                  pl.BlockSpec((1, w), lambda i: (0, 0)),
                  pl.BlockSpec((w, w), lambda i: (0, 0))],
        out_specs=pl.BlockSpec((tm, w), lambda i: (i, 0)),
        out_shape=jax.ShapeDtypeStruct((n, w), F32),
        compiler_params=_cparams(("parallel",)),
        name="s5_finish",
    )(y, u_tb, d_skip.reshape(1, w), w_glu.astype(BF16))


def _conv_masks(row, lc, lt):
    pad = CONV_W // 2
    masks = {}
    for s in range(-pad, pad + 1):
        tgt = row + s
        masks[s] = (tgt >= 0) & (tgt < lt) & ((row < lc) == (tgt < lc))
    return masks


def _dwconv_silu(x, w_ref, bias, masks):
    lt = x.shape[0]
    pad = CONV_W // 2
    acc = w_ref[pad:pad + 1, :] * x
    for j in range(CONV_W):
        s = j - pad
        if s != 0:
            acc = acc + w_ref[j:j + 1, :] * jnp.where(masks[s], pltpu.roll(x, (-s) % lt, 0), 0.0)
    if bias is not None:
        acc = acc + bias
    return _silu(acc)


def _chunk_cumsum(x, row, fwd_lanes):
    lt = x.shape[0]
    pos = row & (CHUNK - 1)
    s = 1
    while s < CHUNK:
        dn = jnp.where(pos >= s, pltpu.roll(x, s, 0), 0.0)
        up = jnp.where(pos < CHUNK - s, pltpu.roll(x, lt - s, 0), 0.0)
        x = x + jnp.where(fwd_lanes, dn, up)
        s *= 2
    return x


def _col(arr, lane, idx):
    return jnp.sum(jnp.where(lane == idx, arr, 0.0), axis=-1, keepdims=True)


def _bwd_chunk(s, n_cc, n_c):
    return jnp.where(s < n_cc, n_cc - 1 - s, n_c - 1 - s + n_cc)


GDN_GROUP = MXU_TILE // CHUNK
GDN_GROUPS_PER_PASS = 3


def _gdn_kernel(q_ref, k_ref, v_ref, z_ref, sm_ref, wq_ref, wk_ref, wv_ref, al_ref, dtb_ref, ng_ref,
                o_ref, qs, ks, vs, gcf_s, gcb_s, bf_s, bb_s, pm_s, n_s, qw_s, au_s, gl_s, of_s, ob_s,
                *, lc, lt):
    h = pl.program_id(1)
    n_c, n_cc = lt // CHUNK, lc // CHUNK
    row = lax.broadcasted_iota(jnp.int32, (lt, 1), 0)
    lane = lax.broadcasted_iota(jnp.int32, (1, LANES), 1)

    def l2n(t):
        return t * lax.rsqrt(jnp.sum(t * t, axis=-1, keepdims=True) + EPS)

    masks = _conv_masks(row, lc, lt)
    qs[...] = l2n(_dwconv_silu(q_ref[...], wq_ref, None, masks)) * (GDN_DK ** -0.5)
    ks[...] = l2n(_dwconv_silu(k_ref[...], wk_ref, None, masks))
    vs[...] = _dwconv_silu(v_ref[...], wv_ref, None, masks)

    sm = sm_ref[...]
    g_all = -jnp.exp(al_ref[...]) * jax.nn.softplus(sm + dtb_ref[...])
    gc_all = _chunk_cumsum(g_all, row, lane < GDN_HEADS)
    beta_all = jax.nn.sigmoid(sm)
    full = (lt, LANES)
    gcf_s[...] = jnp.broadcast_to(_col(gc_all, lane, h), full)
    gcb_s[...] = jnp.broadcast_to(_col(gc_all, lane, GDN_HEADS + h), full)
    bf_s[...] = jnp.broadcast_to(_col(beta_all, lane, 2 * GDN_HEADS + h), full)
    bb_s[...] = jnp.broadcast_to(_col(beta_all, lane, 3 * GDN_HEADS + h), full)

    gc_n = _largest_divisor(n_c, GDN_GROUP)
    gs = gc_n * CHUNK
    ii = lax.broadcasted_iota(jnp.int32, (gs, gs), 0)
    jj = lax.broadcasted_iota(jnp.int32, (gs, gs), 1)
    same = (ii // CHUNK) == (jj // CHUNK)
    eye = (ii == jj).astype(F32)
    col_chunk = lax.broadcasted_iota(jnp.int32, (GDN_DK, gs), 1) // CHUNK

    n_g = n_c // gc_n
    gb = _largest_divisor(n_g, GDN_GROUPS_PER_PASS)

    def prep(it, carry):
        xs_, rhs_, lhs_, meta = [], [], [], []
        for u in range(gb):
            gi = it * gb + u
            rows = pl.ds(pl.multiple_of(gi * gs, gs), gs)
            k = ks[rows, :]
            q = qs[rows, :]
            v = vs[rows, :]
            gcs = (gcf_s[rows, :], gcb_s[rows, :])
            betas = (bf_s[rows, :], bb_s[rows, :])
            kbs = (k * betas[0], k * betas[1])
            prods = lax.dot_general(jnp.concatenate([kbs[0], kbs[1], q], axis=0).astype(BF16),
                                    k.astype(BF16), _NTDOT, preferred_element_type=F32)
            qk = prods[2 * gs:, :]
            for dirn in range(2):
                gc, beta, kb = gcs[dirn], betas[dirn], kbs[dirn]
                incl = same & ((ii >= jj) if dirn == 0 else (ii <= jj))
                strict = same & ((ii > jj) if dirn == 0 else (ii < jj))
                g_i = jnp.concatenate([gc] * (gs // LANES), axis=1) if gs % LANES == 0 else \
                    jnp.broadcast_to(gc[:, :1], (gs, gs))
                g_j = jnp.broadcast_to(jnp.transpose(gc)[0:1, :], (gs, gs))
                decay = jnp.where(incl, jnp.exp(jnp.where(incl, g_i - g_j, 0.0)), 0.0)
                edge = CHUNK - 1 if dirn == 0 else 0
                tots = [gc[cc * CHUNK + edge:cc * CHUNK + edge + 1, :] for cc in range(gc_n)]
                tot = jnp.concatenate([jnp.broadcast_to(t, (CHUNK, LANES)) for t in tots], axis=0)
                kdt = jnp.transpose(k * jnp.exp(tot - gc))
                kd_bd = [jnp.where(col_chunk == cc, kdt, 0.0) for cc in range(gc_n)]
                aqk = jnp.where(incl, qk * decay, 0.0)
                xs_.append(jnp.where(strict, -(prods[dirn * gs:(dirn + 1) * gs, :] * decay), 0.0))
                rhs_.append(jnp.concatenate([v * beta, kb * jnp.exp(gc)], axis=1))
                lhs_.append(jnp.concatenate(kd_bd + [aqk], axis=0).astype(BF16))
                meta.append((dirn, gi, rows, q * jnp.exp(gc), tots))
        xp = jnp.stack(xs_)
        pinv = eye[None] + xp
        xp = _bbdot(xp, xp)
        m = 2
        while 2 * m < CHUNK:
            t = _bbdot(xp, jnp.concatenate([xp, pinv], axis=2))
            xp, pinv = t[:, :, :gs], pinv + t[:, :, gs:]
            m *= 2
        pinv = pinv + _bbdot(xp, pinv)
        uw = _bbdot(pinv, jnp.stack(rhs_)).astype(BF16)
        res_all = _bbdot(jnp.stack(lhs_), uw)
        for idx, (dirn, gi, rows, qd, tots) in enumerate(meta):
            res = res_all[idx]
            aw = res[gc_n * GDN_DK:, :]
            qw_s[dirn, rows, :] = (qd - aw[:, GDN_DV:]).astype(BF16)
            au_s[dirn, rows, :] = aw[:, :GDN_DV]
            for cc in range(gc_n):
                c = gi * gc_n + cc
                blk = res[cc * GDN_DK:(cc + 1) * GDN_DK, :]
                n_s[dirn, c] = blk[:, :GDN_DV]
                pm_s[dirn, c] = blk[:, GDN_DV:].astype(BF16)
                gl_s[dirn, c] = jnp.broadcast_to(jnp.exp(tots[cc]), (SUBLANES, LANES))
        return carry

    lax.fori_loop(0, n_g // gb, prep, 0)

    def step(s, st):
        cs_ = (s, _bwd_chunk(s, n_cc, n_c))
        crows = [pl.ds(pl.multiple_of(c * CHUNK, CHUNK), CHUNK) for c in cs_]
        lhs = jnp.stack([jnp.concatenate([qw_s[dirn, crows[dirn], :], pm_s[dirn, cs_[dirn]]], axis=0)
                         for dirn in range(2)])
        r = _bbdot(lhs, st)
        of_s[crows[0], :] = r[0, :CHUNK, :] + au_s[0, crows[0], :]
        ob_s[crows[1], :] = r[1, :CHUNK, :] + au_s[1, crows[1], :]
        gl = jnp.stack([gl_s[dirn, cs_[dirn]][0:1, :] for dirn in range(2)])
        nn = jnp.stack([n_s[dirn, cs_[dirn]] for dirn in range(2)])
        return gl * st - r[:, CHUNK:, :] + nn

    lax.fori_loop(0, n_c, step, jnp.zeros((2, GDN_DK, GDN_DV), F32))

    o = of_s[...] + ob_s[...]
    o = o * lax.rsqrt(jnp.mean(o * o, axis=-1, keepdims=True) + EPS) * ng_ref[...]
    o_ref[...] = o * _silu(z_ref[...])


def _gdn_mixer(p, lc, conv_w, a_log, dt_bias, norm_g):
    b, lt, _ = p.shape
    n_c = lt // CHUNK
    nh = GDN_HEADS
    kern = functools.partial(_gdn_kernel, lc=lc, lt=lt)
    pcol = lambda base: pl.BlockSpec((None, lt, LANES), lambda i, h, base=base: (i, 0, base + h))
    wcol = lambda base: pl.BlockSpec((CONV_W, LANES), lambda i, h, base=base: (0, base + h))
    vec = pl.BlockSpec((1, LANES), lambda i, h: (0, 0))
    pad = jnp.zeros((LANES - 2 * nh,), F32)
    al = jnp.concatenate([a_log.reshape(-1), pad]).reshape(1, LANES)
    dtb = jnp.concatenate([dt_bias.reshape(-1), pad]).reshape(1, LANES)
    seq = lambda dt: pltpu.VMEM((lt, LANES), dt)
    return pl.pallas_call(
        kern,
        grid=(b, nh),
        in_specs=[pcol(P_QKV // LANES), pcol(P_QKV // LANES + nh), pcol(P_QKV // LANES + 2 * nh),
                  pcol(P_GZ // LANES),
                  pl.BlockSpec((None, lt, LANES), lambda i, h: (i, 0, P_SMALL // LANES)),
                  wcol(0), wcol(nh), wcol(2 * nh), vec, vec, vec],
        out_specs=pl.BlockSpec((None, lt, LANES), lambda i, h: (i, 0, h)),
        out_shape=jax.ShapeDtypeStruct((b, lt, GDN_WIDTH), F32),
        scratch_shapes=[seq(F32), seq(F32), seq(F32), seq(F32), seq(F32), seq(F32), seq(F32),
                        pltpu.VMEM((2, n_c, GDN_DK, GDN_DK), BF16),
                        pltpu.VMEM((2, n_c, GDN_DK, GDN_DV), F32),
                        pltpu.VMEM((2, lt, GDN_DK), BF16),
                        pltpu.VMEM((2, lt, GDN_DV), F32),
                        pltpu.VMEM((2, n_c, SUBLANES, LANES), F32),
                        seq(F32), seq(F32)],
        compiler_params=_cparams(("parallel", "arbitrary")),
        name="gdn_mixer",
    )(p, p, p, p, p, conv_w, conv_w, conv_w, al, dtb, norm_g.reshape(1, LANES))


M2_GW = M2_HPG * M2_HEADDIM
M2_DT_LANE = 4 * GDN_HEADS
M2_KPAD = MXU_TILE


def _ssd_kernel(x_ref, b_ref, c_ref, z_ref, sm_ref, wx_ref, wb_ref, wc_ref, bx_ref, bb_ref, bc_ref,
                al_ref, dtb_ref, dsk_ref, ng_ref, o_ref, xs, bs, cs, dt_s, ac_s, tr_s, lhs_s, rhs_s, h_s,
                *, lc, lt):
    g = pl.program_id(1)
    n_c, n_cc = lt // CHUNK, lc // CHUNK
    row = lax.broadcasted_iota(jnp.int32, (lt, 1), 0)
    lane = lax.broadcasted_iota(jnp.int32, (1, LANES), 1)
    head_of_lane = lax.broadcasted_iota(jnp.int32, (1, M2_GW), 1) // M2_HEADDIM

    masks = _conv_masks(row, lc, lt)
    xs[...] = _dwconv_silu(x_ref[...], wx_ref, bx_ref[...], masks)
    bs[...] = _dwconv_silu(b_ref[...], wb_ref, bb_ref[...], masks)
    cs[...] = _dwconv_silu(c_ref[...], wc_ref, bc_ref[...], masks)

    dt_all = jax.nn.softplus(sm_ref[...] + dtb_ref[...])
    dt_s[...] = dt_all
    ac_s[...] = _chunk_cumsum(dt_all * (-jnp.exp(al_ref[...])), row, lane < M2_DT_LANE + M2_HEADS)

    o_ref[...] = jnp.zeros_like(o_ref)
    h_s[...] = jnp.zeros_like(h_s)
    lhs_s[...] = jnp.zeros_like(lhs_s)
    rhs_s[...] = jnp.zeros_like(rhs_s)

    ii = lax.broadcasted_iota(jnp.int32, (CHUNK, CHUNK), 0)
    jj = lax.broadcasted_iota(jnp.int32, (CHUNK, CHUNK), 1)

    def step(s, carry):
        cs_ = (s, _bwd_chunk(s, n_cc, n_c))
        rws = [pl.ds(pl.multiple_of(c * CHUNK, CHUNK), CHUNK) for c in cs_]
        bc2 = jnp.stack([bs[rws[d], :] for d in range(2)]).astype(BF16)
        cf_ = [cs[rws[d], :] for d in range(2)]
        cb2 = lax.dot_general(jnp.stack(cf_).astype(BF16), bc2, (((2,), (2,)), ((0,), (0,))),
                              preferred_element_type=F32)
        hprev = h_s[...]
        xd_, tot_ = [], []
        for dirn in range(2):
            incl = (ii >= jj) if dirn == 0 else (ii <= jj)
            dtc = dt_s[rws[dirn], :]
            acc = ac_s[rws[dirn], :]
            tr_s[dirn] = jnp.transpose(acc)
            dte = jnp.zeros((CHUNK, M2_GW), F32)
            ace = jnp.zeros((CHUNK, M2_GW), F32)
            for j in range(M2_HPG):
                l_idx = M2_DT_LANE + dirn * M2_HEADS + g * M2_HPG + j
                a_col = _col(acc, lane, l_idx)
                hm = head_of_lane == j
                dte = jnp.where(hm, _col(dtc, lane, l_idx), dte)
                ace = jnp.where(hm, a_col, ace)
                a_row = tr_s[dirn, pl.ds(l_idx, 1), :]
                dec = jnp.where(incl, jnp.exp(jnp.where(incl, a_col - a_row, 0.0)), 0.0)
                blk = pl.ds(j * CHUNK, CHUNK)
                lhs_s[dirn, blk, 0:M2_STATE] = (jnp.exp(a_col) * cf_[dirn]).astype(BF16)
                lhs_s[dirn, blk, M2_STATE:M2_STATE + CHUNK] = (cb2[dirn] * dec).astype(BF16)
            tot = ace[CHUNK - 1:CHUNK, :] if dirn == 0 else ace[0:1, :]
            xdt = xs[rws[dirn], :] * dte
            rhs_s[dirn, 0:M2_STATE, :] = hprev[dirn].astype(BF16)
            rhs_s[dirn, M2_STATE:M2_STATE + CHUNK, :] = xdt.astype(BF16)
            xd_.append((xdt * jnp.exp(tot - ace)).astype(BF16))
            tot_.append(tot)
        r2 = lax.dot_general(lhs_s[...], rhs_s[...], (((2,), (1,)), ((0,), (0,))),
                             preferred_element_type=F32)
        for dirn in range(2):
            y = jnp.where(head_of_lane == 0, r2[dirn, 0:CHUNK, :], 0.0)
            for j in range(1, M2_HPG):
                y = y + jnp.where(head_of_lane == j, r2[dirn, j * CHUNK:(j + 1) * CHUNK, :], 0.0)
            o_ref[rws[dirn], :] += y
        st2 = lax.dot_general(bc2, jnp.stack(xd_), (((1,), (1,)), ((0,), (0,))),
                              preferred_element_type=F32)
        h_s[...] = jnp.exp(jnp.stack(tot_)) * hprev + st2
        return carry

    lax.fori_loop(0, n_c, step, 0, unroll=_largest_divisor(n_c, 2))

    def fin(c, carry):
        rows = pl.ds(pl.multiple_of(c * CHUNK, CHUNK), CHUNK)
        y = (o_ref[rows, :] + dsk_ref[...] * xs[rows, :]) * _silu(z_ref[rows, :])
        o_ref[rows, :] = y * lax.rsqrt(jnp.mean(y * y, axis=-1, keepdims=True) + EPS) * ng_ref[...]
        return carry

    lax.fori_loop(0, n_c, fin, 0)


def _ssd_mixer(p, lc, conv_w, conv_b, a_log, dt_bias, d_skip, norm_g):
    b, lt, _ = p.shape
    kern = functools.partial(_ssd_kernel, lc=lc, lt=lt)
    gw = M2_GW
    bcol = (P_XBC + M2_WIDTH) // LANES
    ccol = bcol + M2_GROUPS * M2_STATE // LANES
    wide = lambda base: pl.BlockSpec((None, lt, gw), lambda i, g, base=base: (i, 0, base + g))
    narrow = lambda base: pl.BlockSpec((None, lt, LANES), lambda i, g, base=base: (i, 0, base + g))
    wwide = lambda rows: pl.BlockSpec((rows, gw), lambda i, g: (0, g))
    wnar = lambda rows, base: pl.BlockSpec((rows, LANES), lambda i, g, base=base: (0, base + g))
    vec = pl.BlockSpec((1, LANES), lambda i, g: (0, 0))
    lead = jnp.zeros((M2_DT_LANE,), F32)
    tail = jnp.zeros((LANES - M2_DT_LANE - 2 * M2_HEADS,), F32)
    al = jnp.concatenate([lead, a_log.reshape(-1), tail]).reshape(1, LANES)
    dtb = jnp.concatenate([lead, dt_bias.reshape(-1), tail]).reshape(1, LANES)
    cbias = conv_b.reshape(1, M2_CONV_CH)
    wb0 = M2_WIDTH // LANES
    wc0 = wb0 + M2_GROUPS * M2_STATE // LANES
    return pl.pallas_call(
        kern,
        grid=(b, M2_GROUPS),
        in_specs=[wide(P_XBC // gw), narrow(bcol), narrow(ccol), wide(P_MZ // gw),
                  pl.BlockSpec((None, lt, LANES), lambda i, g: (i, 0, P_SMALL // LANES)),
                  wwide(CONV_W), wnar(CONV_W, wb0), wnar(CONV_W, wc0),
                  wwide(1), wnar(1, wb0), wnar(1, wc0),
                  vec, vec, wwide(1), wwide(1)],
        out_specs=pl.BlockSpec((None, lt, gw), lambda i, g: (i, 0, g)),
        out_shape=jax.ShapeDtypeStruct((b, lt, M2_WIDTH), F32),
        scratch_shapes=[pltpu.VMEM((lt, gw), F32), pltpu.VMEM((lt, LANES), F32), pltpu.VMEM((lt, LANES), F32),
                        pltpu.VMEM((lt, LANES), F32), pltpu.VMEM((lt, LANES), F32),
                        pltpu.VMEM((2, LANES, CHUNK), F32),
                        pltpu.VMEM((2, M2_HPG * CHUNK, M2_KPAD), BF16),
                        pltpu.VMEM((2, M2_KPAD, gw), BF16),
                        pltpu.VMEM((2, M2_STATE, gw), F32)],
        compiler_params=_cparams(("parallel", "arbitrary")),
        name="ssd_mixer",
    )(p, p, p, p, p, conv_w, conv_w, conv_w, cbias, cbias, cbias, al, dtb,
      jnp.repeat(d_skip, M2_HEADDIM).reshape(1, M2_WIDTH), norm_g.reshape(1, M2_WIDTH))


def _out_proj_kernel(a_ref, b_ref, m_ref, x_ref, mod_ref, w_ref, o_ref, *, lc, tm):
    r = pl.program_id(1)
    acc = jnp.dot(a_ref[...].astype(BF16), w_ref[0:S5_WIDTH, :], preferred_element_type=F32)
    acc += jnp.dot(b_ref[...].astype(BF16), w_ref[S5_WIDTH:S5_WIDTH + GDN_WIDTH, :],
                   preferred_element_type=F32)
    acc += jnp.dot(m_ref[...].astype(BF16), w_ref[S5_WIDTH + GDN_WIDTH:, :],
                   preferred_element_type=F32)
    row = r * tm + lax.broadcasted_iota(jnp.int32, (tm, 1), 0)
    o_ref[...] = x_ref[...] + _mod_vec(mod_ref, 2, row < lc) * acc


def _out_proj(a, bm, m, x, mod, w_out_bf, lc):
    b, lt, d = x.shape
    tm = lt // 4 if (lt // 4) % 8 == 0 and lc <= lt // 4 else lt // 2
    kern = functools.partial(_out_proj_kernel, lc=lc, tm=tm)
    tok = lambda w: pl.BlockSpec((None, tm, w), lambda i, r: (i, r, 0))
    return pl.pallas_call(
        kern,
        grid=(b, lt // tm),
        in_specs=[tok(S5_WIDTH), tok(GDN_WIDTH), tok(M2_WIDTH), tok(d),
                  pl.BlockSpec((None, 2, 6, d), lambda i, r: (i, 0, 0, 0)),
                  pl.BlockSpec((MIX_WIDTH, d), lambda i, r: (0, 0))],
        out_specs=tok(d),
        out_shape=jax.ShapeDtypeStruct((b, lt, d), F32),
        compiler_params=_cparams(("parallel", "parallel")),
        name="out_proj",
    )(a, bm, m, x, mod, w_out_bf)


def _swiglu_acc(h, wg_ref, wu_ref, wd_ref):
    acc = jnp.zeros((h.shape[0], D_MODEL), F32)
    for f in range(FF_NF):
        cols = slice(f * FF_TF, (f + 1) * FF_TF)
        g = jnp.dot(h, wg_ref[:, cols], preferred_element_type=F32)
        u = jnp.dot(h, wu_ref[:, cols], preferred_element_type=F32)
        acc = acc + jnp.dot((_silu(g) * u).astype(BF16), wd_ref[cols, :], preferred_element_type=F32)
    return acc


def _ffn_dense_kernel(x_ref, mod_ref, g_ref, wg_ref, wu_ref, wd_ref, o_ref, *, lc, tm):
    r = pl.program_id(1)
    x = x_ref[...]
    row = r * tm + lax.broadcasted_iota(jnp.int32, (tm, 1), 0)
    is_ctx = row < lc
    h = (_rms(x) * g_ref[...]) * (1.0 + _mod_vec(mod_ref, 4, is_ctx)) + _mod_vec(mod_ref, 3, is_ctx)
    acc = _swiglu_acc(h.astype(BF16), wg_ref, wu_ref, wd_ref)
    o_ref[...] = x + _mod_vec(mod_ref, 5, is_ctx) * acc


def _tok_tile(lc, l):
    return math.gcd(math.gcd(lc, l), 256)


def _ffn_dense(x, mod, g, wg, wu, wd, lc):
    b, lt, d = x.shape
    tm = lt // 3 if lt % (3 * SUBLANES) == 0 else _tok_tile(lc, lt - lc)
    kern = functools.partial(_ffn_dense_kernel, lc=lc, tm=tm)
    tok = pl.BlockSpec((None, tm, d), lambda i, r: (i, r, 0))
    wspec = lambda shp: pl.BlockSpec(shp, lambda i, r: (0, 0))
    return pl.pallas_call(
        kern,
        grid=(b, lt // tm),
        in_specs=[tok, pl.BlockSpec((None, 2, 6, d), lambda i, r: (i, 0, 0, 0)),
                  pl.BlockSpec((1, d), lambda i, r: (0, 0)),
                  wspec((d, D_FF)), wspec((d, D_FF)), wspec((D_FF, d))],
        out_specs=tok,
        out_shape=jax.ShapeDtypeStruct((b, lt, d), F32),
        compiler_params=_cparams(("parallel", "parallel")),
        name="ffn_dense",
    )(x, mod, g.reshape(1, d), wg.astype(BF16), wu.astype(BF16), wd.astype(BF16))


def _router_kernel(x_ref, mod_ref, g_ref, rw_ref, h_ref, rt_ref, *, lc, tm):
    r = pl.program_id(1)
    x = x_ref[...]
    row = r * tm + lax.broadcasted_iota(jnp.int32, (tm, 1), 0)
    is_ctx = row < lc
    h = (_rms(x) * g_ref[...]) * (1.0 + _mod_vec(mod_ref, 4, is_ctx)) + _mod_vec(mod_ref, 3, is_ctx)
    h_ref[...] = h
    logits = jnp.dot(h, rw_ref[...], preferred_element_type=F32, precision=lax.Precision.HIGHEST)
    lane = lax.broadcasted_iota(jnp.int32, logits.shape, 1)
    neg = jnp.float32(-jnp.inf)
    lg = jnp.where(lane < N_EXPERTS, logits, neg)
    m1 = jnp.max(lg, axis=-1, keepdims=True)
    i1 = jnp.min(jnp.where(lg == m1, lane, LANES), axis=-1, keepdims=True)
    lg2 = jnp.where(lane == i1, neg, lg)
    m2 = jnp.max(lg2, axis=-1, keepdims=True)
    i2 = jnp.min(jnp.where(lg2 == m2, lane, LANES), axis=-1, keepdims=True)
    e2 = jnp.exp(m2 - m1)
    den = 1.0 + e2
    out = jnp.where(lane == 0, i1.astype(F32), 0.0)
    out = jnp.where(lane == 1, i2.astype(F32), out)
    out = jnp.where(lane == 2, 1.0 / den, out)
    out = jnp.where(lane == 3, e2 / den, out)
    rt_ref[...] = out


def _router(x, mod, g, router_w, lc):
    b, lt, d = x.shape
    tm = _tok_tile(lc, lt - lc)
    rw = jnp.concatenate([router_w, jnp.zeros((d, LANES - N_EXPERTS), F32)], axis=1)
    kern = functools.partial(_router_kernel, lc=lc, tm=tm)
    tok = lambda w: pl.BlockSpec((None, tm, w), lambda i, r: (i, r, 0))
    return pl.pallas_call(
        kern,
        grid=(b, lt // tm),
        in_specs=[tok(d), pl.BlockSpec((None, 2, 6, d), lambda i, r: (i, 0, 0, 0)),
                  pl.BlockSpec((1, d), lambda i, r: (0, 0)),
                  pl.BlockSpec((d, LANES), lambda i, r: (0, 0))],
        out_specs=[tok(d), tok(LANES)],
        out_shape=[jax.ShapeDtypeStruct((b, lt, d), F32), jax.ShapeDtypeStruct((b, lt, LANES), F32)],
        compiler_params=_cparams(("parallel", "parallel")),
        name="moe_router",
    )(x, mod, g.reshape(1, d), rw)


def _row_copy(src_hbm, dst, idx, r, sem):
    return pltpu.make_async_copy(src_hbm.at[pl.ds(idx, 1), :], dst.at[pl.ds(r, 1), :], sem)


def _expert_kernel(te_ref, nt_ref, idx_ref, idx_next_ref, h_hbm, wg_ref, wu_ref, wd_ref, o_ref, xbuf, sems):
    t = pl.program_id(0)
    nt = nt_ref[0]
    slot = lax.rem(t, 2)

    def gather(idx, s, start):
        def body(r, c):
            cp = _row_copy(h_hbm, xbuf.at[s], idx[r] if start else 0, r, sems.at[s])
            cp.start() if start else cp.wait()
            return c
        lax.fori_loop(0, MOE_TM, body, 0, unroll=8)

    @pl.when((t == 0) & (nt > 0))
    def _():
        gather(idx_ref, 0, True)

    @pl.when(t + 1 < nt)
    def _():
        gather(idx_next_ref, 1 - slot, True)

    @pl.when(t < nt)
    def _():
        gather(idx_ref, slot, False)
        o_ref[...] = _swiglu_acc(xbuf[slot].astype(BF16), wg_ref, wu_ref, wd_ref)

    @pl.when(t >= nt)
    def _():
        o_ref[...] = jnp.zeros_like(o_ref)


def _experts(h2, src_row, tile_expert, n_tiles_used, wg, wu, wd):
    rp = src_row.shape[0]
    d = h2.shape[1]
    n_tiles = rp // MOE_TM
    wspec = lambda shp: pl.BlockSpec((None,) + shp, lambda t, te, nt: (te[t], 0, 0))
    grid_spec = pltpu.PrefetchScalarGridSpec(
        num_scalar_prefetch=2,
        grid=(n_tiles,),
        in_specs=[pl.BlockSpec((MOE_TM,), lambda t, te, nt: (t,), memory_space=pltpu.SMEM),
                  pl.BlockSpec((MOE_TM,), lambda t, te, nt: (jnp.minimum(t + 1, n_tiles - 1),),
                               memory_space=pltpu.SMEM),
                  pl.BlockSpec(memory_space=pl.ANY),
                  wspec((d, D_FF)), wspec((d, D_FF)), wspec((D_FF, d))],
        out_specs=pl.BlockSpec((MOE_TM, d), lambda t, te, nt: (t, 0)),
        scratch_shapes=[pltpu.VMEM((2, MOE_TM, d), F32), pltpu.SemaphoreType.DMA((2,))],
    )
    return pl.pallas_call(
        _expert_kernel,
        grid_spec=grid_spec,
        out_shape=jax.ShapeDtypeStruct((rp, d), F32),
        compiler_params=_cparams(("arbitrary",)),
        name="moe_experts",
    )(tile_expert, n_tiles_used, src_row, src_row, h2, wg.astype(BF16), wu.astype(BF16), wd.astype(BF16))


def _combine_kernel(p1_ref, p2_ref, y_hbm, x_ref, rt_ref, mod_ref, o_ref, ya, yb, sem, *, lc, tm):
    r = pl.program_id(1)

    def start(k, c):
        _row_copy(y_hbm, ya, p1_ref[k], k, sem).start(priority=0)
        _row_copy(y_hbm, yb, p2_ref[k], k, sem).start(priority=1)
        return c
    lax.fori_loop(0, tm, start, 0)

    def wait(k, c):
        _row_copy(y_hbm, ya, 0, k, sem).wait()
        _row_copy(y_hbm, yb, 0, k, sem).wait()
        return c
    lax.fori_loop(0, tm, wait, 0)

    row = r * tm + lax.broadcasted_iota(jnp.int32, (tm, 1), 0)
    rt = rt_ref[...]
    y = rt[:, 2:3] * ya[...] + rt[:, 3:4] * yb[...]
    o_ref[...] = x_ref[...] + _mod_vec(mod_ref, 5, row < lc) * y


def _combine(p1, p2, y, x, rt, mod, lc):
    b, lt, d = x.shape
    tm = _tok_tile(lc, lt - lc)
    nt = lt // tm
    kern = functools.partial(_combine_kernel, lc=lc, tm=tm)
    tok = lambda w: pl.BlockSpec((None, tm, w), lambda i, r: (i, r, 0))
    ispec = pl.BlockSpec((tm,), lambda i, r: (i * nt + r,), memory_space=pltpu.SMEM)
    return pl.pallas_call(
        kern,
        grid=(b, nt),
        in_specs=[ispec, ispec, pl.BlockSpec(memory_space=pl.ANY), tok(d), tok(LANES),
                  pl.BlockSpec((None, 2, 6, d), lambda i, r: (i, 0, 0, 0))],
        out_specs=tok(d),
        out_shape=jax.ShapeDtypeStruct((b, lt, d), F32),
        scratch_shapes=[pltpu.VMEM((tm, d), F32), pltpu.VMEM((tm, d), F32),
                        pltpu.SemaphoreType.DMA(())],
        compiler_params=_cparams(("arbitrary", "arbitrary")),
        name="moe_combine",
    )(p1, p2, y, x, rt, mod)


def _moe(x, mod, g, router_w, wg, wu, wd, lc):
    b, lt, d = x.shape
    n = b * lt
    h2, rt = _router(x, mod, g, router_w, lc)
    rt2 = rt.reshape(n, LANES)
    e_flat = jnp.concatenate([rt2[:, 0], rt2[:, 1]]).astype(jnp.int32)
    onehot = (e_flat[:, None] == jnp.arange(N_EXPERTS, dtype=jnp.int32)[None, :]).astype(jnp.int32)
    counts = jnp.sum(onehot, axis=0)
    rank = jnp.sum((jnp.cumsum(onehot, axis=0) - 1) * onehot, axis=1)
    padded = ((counts + MOE_TM - 1) // MOE_TM) * MOE_TM
    ends = jnp.cumsum(padded)
    pos = (ends - padded)[e_flat] + rank
    rp = ((2 * n + MOE_TM - 1) // MOE_TM + N_EXPERTS) * MOE_TM
    tok_id = jnp.concatenate([jnp.arange(n, dtype=jnp.int32)] * 2)
    src_row = jnp.zeros((rp,), jnp.int32).at[pos].set(tok_id)
    tile_start = jnp.arange(rp // MOE_TM, dtype=jnp.int32) * MOE_TM
    tile_expert = jnp.minimum(jnp.sum(tile_start[:, None] >= ends[None, :], axis=1),
                              N_EXPERTS - 1).astype(jnp.int32)
    n_used = (ends[-1] // MOE_TM).astype(jnp.int32).reshape(1)
    ys = _experts(h2.reshape(n, d), src_row, tile_expert, n_used, wg, wu, wd)
    return _combine(pos[:n].astype(jnp.int32), pos[n:].astype(jnp.int32), ys, x, rt, mod, lc)


def _final_norm_kernel(x_ref, g_ref, o_ref):
    o_ref[...] = _rms(x_ref[...]) * g_ref[...]


def _final_norm(x, g):
    b, l, d = x.shape
    tm = math.gcd(l, 512)
    tok = pl.BlockSpec((None, tm, d), lambda i, r: (i, r, 0))
    return pl.pallas_call(
        _final_norm_kernel,
        grid=(b, l // tm),
        in_specs=[tok, pl.BlockSpec((1, d), lambda i, r: (0, 0))],
        out_specs=tok,
        out_shape=jax.ShapeDtypeStruct((b, l, d), F32),
        compiler_params=_cparams(("parallel", "parallel")),
        name="final_norm",
    )(x, g.reshape(1, d))


def _grid_t(xl, rows, cols):
    b, l, ch = xl.shape
    return xl.reshape(b, rows, cols, ch).transpose(0, 2, 1, 3).reshape(b, l, ch)


def kernel(x, c, ctx, c_ctx, ada_w, ada_b, norm1_g, norm2_g, w_in, w_out,
           s5_lam_re, s5_lam_im, s5_b_re, s5_b_im, s5_c_re, s5_c_im, s5_log_dt, s5_d, s5_w_glu,
           gdn_conv_w, gdn_a_log, gdn_dt_bias, gdn_norm_g,
           m2_conv_w, m2_conv_b, m2_a_log, m2_dt_bias, m2_d, m2_norm_g,
           ffn_w_gate, ffn_w_up, ffn_w_down,
           moe_router, moe_w_gate, moe_w_up, moe_w_down, final_norm_g):
    b, l, d = x.shape
    lc = ctx.shape[1]
    lt = lc + l
    depth = ada_w.shape[0]
    rows = l // GRID_W
    assert b % SUBLANES == 0 and lc % CHUNK == 0 and l % CHUNK == 0

    cpad = jnp.zeros((b + SUBLANES, d), F32).at[:b].set(c).at[b].set(c_ctx)
    mods = _ada_all(cpad, ada_w, ada_b).reshape(depth, b + SUBLANES, 6, d)

    xt = jnp.concatenate([ctx, x], axis=1)
    col_major = False
    for i in range(depth):
        want_cm = i % 2 == 1
        if want_cm != col_major:
            lat = xt[:, lc:]
            lat = _grid_t(lat, rows, GRID_W) if want_cm else _grid_t(lat, GRID_W, rows)
            xt = jnp.concatenate([xt[:, :lc], lat], axis=1)
            col_major = want_cm
        mod = jnp.stack([jnp.broadcast_to(mods[i, b][None], (b, 6, d)), mods[i, :b]], axis=1)

        p = _in_proj(xt, mod, norm1_g[i], _pack_w_in(w_in[i]), lc)
        u_tb = jnp.transpose(p[:, :, P_U:P_U + S5_WIDTH], (1, 0, 2)).reshape(lt * b, S5_WIDTH)
        s5_par = _s5_params(s5_lam_re[i], s5_lam_im[i], s5_b_re[i], s5_b_im[i], s5_c_re[i], s5_c_im[i],
                            s5_log_dt[i])
        ya = _s5_mixer(u_tb, lc, b, s5_par, s5_d[i], s5_w_glu[i])
        ya = jnp.transpose(ya.reshape(lt, b, S5_WIDTH), (1, 0, 2))
        yb = _gdn_mixer(p, lc, gdn_conv_w[i], gdn_a_log[i], gdn_dt_bias[i], gdn_norm_g[i])
        ym = _ssd_mixer(p, lc, m2_conv_w[i], m2_conv_b[i], m2_a_log[i], m2_dt_bias[i], m2_d[i],
                        m2_norm_g[i])
        xt = _out_proj(ya, yb, ym, xt, mod, w_out[i].astype(BF16), lc)

        j = i // 2
        if i % 2 == 0:
            xt = _ffn_dense(xt, mod, norm2_g[i], ffn_w_gate[j], ffn_w_up[j], ffn_w_down[j], lc)
        else:
            xt = _moe(xt, mod, norm2_g[i], moe_router[j], moe_w_gate[j], moe_w_up[j], moe_w_down[j], lc)

    lat = xt[:, lc:]
    if col_major:
        lat = _grid_t(lat, GRID_W, rows)
    return _final_norm(lat, final_norm_g)
```

```python
import functools
import math

import jax
import jax.numpy as jnp
from jax import lax
from jax.experimental import pallas as pl
from jax.experimental.pallas import tpu as pltpu

F32 = jnp.float32
BF16 = jnp.bfloat16

D_MODEL = 1024
GRID_W = 64
EPS = 1e-6
CHUNK = 64
CONV_W = 5
S5_WIDTH = D_MODEL // 2
S5_GROUP = 16
S5_GROUPS = S5_WIDTH // S5_GROUP
S5_STATE = 64
GDN_DK = 128
GDN_DV = 128
GDN_WIDTH = D_MODEL // 2
GDN_HEADS = GDN_WIDTH // GDN_DV
GDN_CONV_CH = 2 * GDN_HEADS * GDN_DK + GDN_WIDTH
M2_WIDTH = D_MODEL
M2_HEADDIM = 64
M2_HEADS = M2_WIDTH // M2_HEADDIM
M2_GROUPS = 2
M2_HPG = M2_HEADS // M2_GROUPS
M2_STATE = 128
M2_CONV_CH = M2_WIDTH + 2 * M2_GROUPS * M2_STATE
MIX_WIDTH = S5_WIDTH + GDN_WIDTH + M2_WIDTH
IN_SIZES = (S5_WIDTH, GDN_CONV_CH, GDN_WIDTH, 2 * GDN_HEADS, 2 * GDN_HEADS,
            M2_WIDTH, M2_CONV_CH, 2 * M2_HEADS)
D_FF = 256 * ((8 * D_MODEL // 3 + 255) // 256)
N_EXPERTS = 8

LANES = 128
SUBLANES = 8
MXU_TILE = 256
VMEM_LIMIT = 56 * 1024 * 1024

P_U = 0
P_GZ = P_U + S5_WIDTH
P_QKV = P_GZ + GDN_WIDTH
P_MZ = P_QKV + GDN_CONV_CH
P_XBC = P_MZ + M2_WIDTH
P_SMALL = P_XBC + M2_CONV_CH
IN_TN = 896
NP = 6 * IN_TN
assert P_SMALL + LANES <= NP

FF_TF = 256
FF_NF = D_FF // FF_TF
MOE_TM = 512


def _cparams(sem):
    return pltpu.CompilerParams(dimension_semantics=sem, vmem_limit_bytes=VMEM_LIMIT)


def _silu(x):
    return x * jax.nn.sigmoid(x)


def _rms(x):
    return x * lax.rsqrt(jnp.mean(x * x, axis=-1, keepdims=True) + EPS)


def _mod_vec(mod_ref, k, is_ctx):
    return jnp.where(is_ctx, mod_ref[0, k:k + 1, :], mod_ref[1, k:k + 1, :])


def _largest_divisor(n, cap):
    return max(u for u in range(1, cap + 1) if n % u == 0)


_TDOT = (((0,), (0,)), ((), ()))
_NTDOT = (((1,), (1,)), ((), ()))


def _bdot(a, b):
    return jnp.dot(a.astype(BF16), b.astype(BF16), preferred_element_type=F32)


def _bbdot(a, b):
    return lax.dot_general(a.astype(BF16), b.astype(BF16), (((2,), (1,)), ((0,), (0,))),
                           preferred_element_type=F32)


def _ada_kernel(c_ref, w_ref, b_ref, o_ref):
    o_ref[...] = _bdot(_silu(c_ref[...]), w_ref[...]) + b_ref[...]


def _ada_all(cpad, ada_w, ada_b):
    depth, d, n6 = ada_w.shape
    tn = 1536
    rows = cpad.shape[0]
    return pl.pallas_call(
        _ada_kernel,
        grid=(depth, n6 // tn),
        in_specs=[pl.BlockSpec((rows, d), lambda i, j: (0, 0)),
                  pl.BlockSpec((None, d, tn), lambda i, j: (i, 0, j)),
                  pl.BlockSpec((None, 1, tn), lambda i, j: (i, 0, j))],
        out_specs=pl.BlockSpec((None, rows, tn), lambda i, j: (i, 0, j)),
        out_shape=jax.ShapeDtypeStruct((depth, rows, n6), F32),
        compiler_params=_cparams(("parallel", "parallel")),
        name="ada_mod",
    )(cpad, ada_w, ada_b.reshape(depth, 1, n6))


def _in_proj_kernel(x_ref, mod_ref, g_ref, w_ref, o_ref, h_scr, *, lc, tm, rc):
    r = pl.program_id(1)

    @pl.when(pl.program_id(2) == 0)
    def _():
        def body(c, carry):
            r0 = pl.multiple_of(c * rc, rc)
            x = x_ref[pl.ds(r0, rc), :]
            row = r * tm + r0 + lax.broadcasted_iota(jnp.int32, (rc, 1), 0)
            is_ctx = row < lc
            h = (_rms(x) * g_ref[...]) * (1.0 + _mod_vec(mod_ref, 1, is_ctx)) + _mod_vec(mod_ref, 0, is_ctx)
            h_scr[pl.ds(r0, rc), :] = h.astype(BF16)
            return carry
        lax.fori_loop(0, tm // rc, body, 0)

    o_ref[...] = jnp.dot(h_scr[...], w_ref[...], preferred_element_type=F32)


def _in_proj(x, mod, g, w_packed, layer, lc):
    b, lt, d = x.shape
    tm = lt // 2
    rc = math.gcd(tm, 128)
    assert lc <= tm and tm % 16 == 0
    kern = functools.partial(_in_proj_kernel, lc=lc, tm=tm, rc=rc)
    return pl.pallas_call(
        kern,
        grid=(b, 2, NP // IN_TN),
        in_specs=[pl.BlockSpec((None, tm, d), lambda i, r, j: (i, r, 0)),
                  pl.BlockSpec((None, 2, 6, d), lambda i, r, j: (i, 0, 0, 0)),
                  pl.BlockSpec((1, d), lambda i, r, j: (0, 0)),
                  pl.BlockSpec((None, d, IN_TN), lambda i, r, j: (layer, 0, j))],
        out_specs=pl.BlockSpec((None, tm, IN_TN), lambda i, r, j: (i, r, j)),
        out_shape=jax.ShapeDtypeStruct((b, lt, NP), F32),
        scratch_shapes=[pltpu.VMEM((tm, d), BF16)],
        compiler_params=_cparams(("parallel", "parallel", "arbitrary")),
        name="in_proj",
    )(x, mod, g.reshape(1, d), w_packed)


def _pack_w_in(w):
    o = [0]
    for s in IN_SIZES:
        o.append(o[-1] + s)
    u, qkv, gz, al, be, mz, xbc, dt = [w[:, o[i]:o[i + 1]] for i in range(8)]
    small_pad = jnp.zeros((w.shape[0], LANES - 4 * GDN_HEADS - 2 * M2_HEADS), w.dtype)
    tail = jnp.zeros((w.shape[0], NP - P_SMALL - LANES), w.dtype)
    return jnp.concatenate([u, gz, qkv, mz, xbc, al, be, dt, small_pad, tail], axis=1).astype(BF16)


S5_HALF = S5_WIDTH // 2
S5_HS = (S5_GROUPS // 2) * S5_STATE
S5_LQ = 512


def _s5_scan_kernel(u_ref, bk_ref, ck_ref, a_ref, y_ref, s_scr, h_scr, *, tt, nb):
    d = pl.program_id(0)

    @pl.when(pl.program_id(1) == 0)
    def _():
        h_scr[...] = jnp.zeros_like(h_scr)

    u = u_ref[...].astype(BF16)
    for k in range(2):
        s_scr[:, k * 2 * S5_HS:(k + 1) * 2 * S5_HS] = jnp.dot(
            u[:, k * S5_HALF:(k + 1) * S5_HALF], bk_ref[k], preferred_element_type=F32)

    for k in range(2):
        for q in range(S5_HS // S5_LQ):
            re0 = k * 2 * S5_HS + q * S5_LQ
            im0 = re0 + S5_HS
            a_re = jnp.broadcast_to(a_ref[2 * k:2 * k + 1, q * S5_LQ:(q + 1) * S5_LQ], (nb, S5_LQ))
            a_im = jnp.broadcast_to(a_ref[2 * k + 1:2 * k + 2, q * S5_LQ:(q + 1) * S5_LQ], (nb, S5_LQ))

            def step(i, carry, re0=re0, im0=im0, a_re=a_re, a_im=a_im):
                hr, hi = carry
                t = i + d * (tt - 1 - 2 * i)
                r = pl.multiple_of(t * nb, nb)
                nr = a_re * hr - a_im * hi + s_scr[pl.ds(r, nb), re0:re0 + S5_LQ]
                ni = a_re * hi + a_im * hr + s_scr[pl.ds(r, nb), im0:im0 + S5_LQ]
                s_scr[pl.ds(r, nb), re0:re0 + S5_LQ] = nr
                s_scr[pl.ds(r, nb), im0:im0 + S5_LQ] = ni
                return nr, ni

            hr, hi = lax.fori_loop(0, tt, step, (h_scr[:, re0:re0 + S5_LQ], h_scr[:, im0:im0 + S5_LQ]),
                                   unroll=4)
            h_scr[:, re0:re0 + S5_LQ] = hr
            h_scr[:, im0:im0 + S5_LQ] = hi

    for k in range(2):
        y_ref[:, k * S5_HALF:(k + 1) * S5_HALF] = jnp.dot(
            s_scr[:, k * 2 * S5_HS:(k + 1) * 2 * S5_HS].astype(BF16), ck_ref[k],
            preferred_element_type=F32)


def _s5_fin_kernel(y_ref, u_ref, d_ref, w_ref, o_ref):
    y = y_ref[0] + y_ref[1] + d_ref[...] * u_ref[...]
    y = jax.nn.gelu(y)
    o_ref[...] = y * jax.nn.sigmoid(jnp.dot(y.astype(BF16), w_ref[...], preferred_element_type=F32))


def _s5_params(lam_re, lam_im, b_re, b_im, c_re, c_im, log_dt):
    lam = lax.complex(jnp.minimum(lam_re, -1e-4), lam_im)
    dt = jnp.exp(log_dt)[..., None]
    lam_bar = jnp.exp(lam * dt)
    gamma = (lam_bar - 1.0) / lam
    bt = lax.complex(b_re, b_im) * gamma[..., None]
    gh = S5_GROUPS // 2
    eye = jnp.eye(gh, dtype=F32)

    def bd_in(m):
        m = m.reshape(2, 2, gh, S5_STATE, S5_GROUP)
        return jnp.einsum("gh,dkgpi->dkgihp", eye, m).reshape(2, 2, gh * S5_GROUP, gh * S5_STATE)

    def bd_out(m):
        m = m.reshape(2, 2, gh, S5_GROUP, S5_STATE)
        return jnp.einsum("gh,dkgip->dkgphi", eye, m).reshape(2, 2, gh * S5_STATE, gh * S5_GROUP)

    bk = jnp.concatenate([bd_in(bt.real), bd_in(bt.imag)], axis=-1).astype(BF16)
    ck = jnp.concatenate([bd_out(c_re), -bd_out(c_im)], axis=-2).astype(BF16)
    lr = lam_bar.real.reshape(2, 2, 1, S5_HS)
    li = lam_bar.imag.reshape(2, 2, 1, S5_HS)
    a = jnp.concatenate([lr, li], axis=2).reshape(2, 4, S5_HS)
    return bk, ck, a


def _s5_mixer(u_tb, lc, nb, params, d_skip, w_glu, layer):
    n, w = u_tb.shape
    lt = n // nb
    tt = CHUNK
    n_c, n_t = lc // tt, lt // tt
    bk, ck, a = params

    def tile(dd, s):
        return jnp.where(dd == 0, s, jnp.where(s < n_c, n_c - 1 - s, n_t - 1 - s + n_c))

    kern = functools.partial(_s5_scan_kernel, tt=tt, nb=nb)
    y = pl.pallas_call(
        kern,
        grid=(2, n_t),
        in_specs=[pl.BlockSpec((tt * nb, w), lambda dd, s: (tile(dd, s), 0)),
                  pl.BlockSpec((None, None, 2, S5_HALF, 2 * S5_HS), lambda dd, s: (layer, dd, 0, 0, 0)),
                  pl.BlockSpec((None, None, 2, 2 * S5_HS, S5_HALF), lambda dd, s: (layer, dd, 0, 0, 0)),
                  pl.BlockSpec((None, None, 4, S5_HS), lambda dd, s: (layer, dd, 0, 0))],
        out_specs=pl.BlockSpec((None, tt * nb, w), lambda dd, s: (dd, tile(dd, s), 0)),
        out_shape=jax.ShapeDtypeStruct((2, n, w), F32),
        scratch_shapes=[pltpu.VMEM((tt * nb, 4 * S5_HS), F32), pltpu.VMEM((nb, 4 * S5_HS), F32)],
        compiler_params=_cparams(("arbitrary", "arbitrary")),
        name="s5_scan",
    )(u_tb, bk, ck, a)

    tm = math.gcd(n, 512)
    return pl.pallas_call(
        _s5_fin_kernel,
        grid=(n // tm,),
        in_specs=[pl.BlockSpec((2, tm, w), lambda i: (0, i, 0)),
                  pl.BlockSpec((tm, w), lambda i: (i, 0)),
                  pl.BlockSpec((1, w), lambda i: (0, 0)),
                  pl.BlockSpec((None, w, w), lambda i: (layer, 0, 0))],
        out_specs=pl.BlockSpec((tm, w), lambda i: (i, 0)),
        out_shape=jax.ShapeDtypeStruct((n, w), F32),
        compiler_params=_cparams(("parallel",)),
        name="s5_finish",
    )(y, u_tb, d_skip.reshape(1, w), w_glu)


def _conv_masks(row, lc, lt):
    pad = CONV_W // 2
    masks = {}
    for s in range(-pad, pad + 1):
        tgt = row + s
        masks[s] = (tgt >= 0) & (tgt < lt) & ((row < lc) == (tgt < lc))
    return masks


def _dwconv_silu(x, w_ref, bias, masks):
    lt = x.shape[0]
    pad = CONV_W // 2
    acc = w_ref[pad:pad + 1, :] * x
    for j in range(CONV_W):
        s = j - pad
        if s != 0:
            acc = acc + w_ref[j:j + 1, :] * jnp.where(masks[s], pltpu.roll(x, (-s) % lt, 0), 0.0)
    if bias is not None:
        acc = acc + bias
    return _silu(acc)


def _chunk_cumsum(x, row, fwd_lanes):
    lt = x.shape[0]
    pos = row & (CHUNK - 1)
    s = 1
    while s < CHUNK:
        dn = jnp.where(pos >= s, pltpu.roll(x, s, 0), 0.0)
        up = jnp.where(pos < CHUNK - s, pltpu.roll(x, lt - s, 0), 0.0)
        x = x + jnp.where(fwd_lanes, dn, up)
        s *= 2
    return x


def _col(arr, lane, idx):
    return jnp.sum(jnp.where(lane == idx, arr, 0.0), axis=-1, keepdims=True)


def _bwd_chunk(s, n_cc, n_c):
    return jnp.where(s < n_cc, n_cc - 1 - s, n_c - 1 - s + n_cc)


GDN_GROUP = MXU_TILE // CHUNK
GDN_GROUPS_PER_PASS = 3


def _gdn_kernel(q_ref, k_ref, v_ref, z_ref, sm_ref, wq_ref, wk_ref, wv_ref, al_ref, dtb_ref, ng_ref,
                o_ref, qs, ks, vs, gcf_s, gcb_s, bf_s, bb_s, pm_s, n_s, qw_s, au_s, gl_s, of_s, ob_s,
                *, lc, lt):
    h = pl.program_id(1)
    n_c, n_cc = lt // CHUNK, lc // CHUNK
    row = lax.broadcasted_iota(jnp.int32, (lt, 1), 0)
    lane = lax.broadcasted_iota(jnp.int32, (1, LANES), 1)

    def l2n(t):
        return t * lax.rsqrt(jnp.sum(t * t, axis=-1, keepdims=True) + EPS)

    masks = _conv_masks(row, lc, lt)
    qs[...] = l2n(_dwconv_silu(q_ref[...], wq_ref, None, masks)) * (GDN_DK ** -0.5)
    ks[...] = l2n(_dwconv_silu(k_ref[...], wk_ref, None, masks))
    vs[...] = _dwconv_silu(v_ref[...], wv_ref, None, masks)

    sm = sm_ref[...]
    g_all = -jnp.exp(al_ref[...]) * jax.nn.softplus(sm + dtb_ref[...])
    gc_all = _chunk_cumsum(g_all, row, lane < GDN_HEADS)
    beta_all = jax.nn.sigmoid(sm)
    full = (lt, LANES)
    gcf_s[...] = jnp.broadcast_to(_col(gc_all, lane, h), full)
    gcb_s[...] = jnp.broadcast_to(_col(gc_all, lane, GDN_HEADS + h), full)
    bf_s[...] = jnp.broadcast_to(_col(beta_all, lane, 2 * GDN_HEADS + h), full)
    bb_s[...] = jnp.broadcast_to(_col(beta_all, lane, 3 * GDN_HEADS + h), full)

    gc_n = _largest_divisor(n_c, GDN_GROUP)
    gs = gc_n * CHUNK
    ii = lax.broadcasted_iota(jnp.int32, (gs, gs), 0)
    jj = lax.broadcasted_iota(jnp.int32, (gs, gs), 1)
    same = (ii // CHUNK) == (jj // CHUNK)
    eye = (ii == jj).astype(F32)
    col_chunk = lax.broadcasted_iota(jnp.int32, (GDN_DK, gs), 1) // CHUNK

    n_g = n_c // gc_n
    gb = _largest_divisor(n_g, GDN_GROUPS_PER_PASS)

    def prep(it, carry):
        xs_, rhs_, lhs_, meta = [], [], [], []
        for u in range(gb):
            gi = it * gb + u
            rows = pl.ds(pl.multiple_of(gi * gs, gs), gs)
            k = ks[rows, :]
            q = qs[rows, :]
            v = vs[rows, :]
            gcs = (gcf_s[rows, :], gcb_s[rows, :])
            betas = (bf_s[rows, :], bb_s[rows, :])
            kbs = (k * betas[0], k * betas[1])
            prods = lax.dot_general(jnp.concatenate([kbs[0], kbs[1], q], axis=0).astype(BF16),
                                    k.astype(BF16), _NTDOT, preferred_element_type=F32)
            qk = prods[2 * gs:, :]
            for dirn in range(2):
                gc, beta, kb = gcs[dirn], betas[dirn], kbs[dirn]
                incl = same & ((ii >= jj) if dirn == 0 else (ii <= jj))
                strict = same & ((ii > jj) if dirn == 0 else (ii < jj))
                g_i = jnp.concatenate([gc] * (gs // LANES), axis=1) if gs % LANES == 0 else \
                    jnp.broadcast_to(gc[:, :1], (gs, gs))
                g_j = jnp.broadcast_to(jnp.transpose(gc)[0:1, :], (gs, gs))
                decay = jnp.where(incl, jnp.exp(jnp.where(incl, g_i - g_j, 0.0)), 0.0)
                edge = CHUNK - 1 if dirn == 0 else 0
                tots = [gc[cc * CHUNK + edge:cc * CHUNK + edge + 1, :] for cc in range(gc_n)]
                tot = jnp.concatenate([jnp.broadcast_to(t, (CHUNK, LANES)) for t in tots], axis=0)
                kdt = jnp.transpose(k * jnp.exp(tot - gc))
                kd_bd = [jnp.where(col_chunk == cc, kdt, 0.0) for cc in range(gc_n)]
                aqk = jnp.where(incl, qk * decay, 0.0)
                xs_.append(jnp.where(strict, -(prods[dirn * gs:(dirn + 1) * gs, :] * decay), 0.0))
                rhs_.append(jnp.concatenate([v * beta, kb * jnp.exp(gc)], axis=1))
                lhs_.append(jnp.concatenate(kd_bd + [aqk], axis=0).astype(BF16))
                meta.append((dirn, gi, rows, q * jnp.exp(gc), tots))
        xp = jnp.stack(xs_)
        pinv = eye[None] + xp
        xp = _bbdot(xp, xp)
        m = 2
        while 2 * m < CHUNK:
            t = _bbdot(xp, jnp.concatenate([xp, pinv], axis=2))
            xp, pinv = t[:, :, :gs], pinv + t[:, :, gs:]
            m *= 2
        pinv = pinv + _bbdot(xp, pinv)
        uw = _bbdot(pinv, jnp.stack(rhs_)).astype(BF16)
        res_all = _bbdot(jnp.stack(lhs_), uw)
        for idx, (dirn, gi, rows, qd, tots) in enumerate(meta):
            res = res_all[idx]
            aw = res[gc_n * GDN_DK:, :]
            qw_s[dirn, rows, :] = (qd - aw[:, GDN_DV:]).astype(BF16)
            au_s[dirn, rows, :] = aw[:, :GDN_DV]
            for cc in range(gc_n):
                c = gi * gc_n + cc
                blk = res[cc * GDN_DK:(cc + 1) * GDN_DK, :]
                n_s[dirn, c] = blk[:, :GDN_DV]
                pm_s[dirn, c] = blk[:, GDN_DV:].astype(BF16)
                gl_s[dirn, c] = jnp.broadcast_to(jnp.exp(tots[cc]), (SUBLANES, LANES))
        return carry

    lax.fori_loop(0, n_g // gb, prep, 0)

    def step(s, st):
        cs_ = (s, _bwd_chunk(s, n_cc, n_c))
        crows = [pl.ds(pl.multiple_of(c * CHUNK, CHUNK), CHUNK) for c in cs_]
        lhs = jnp.stack([jnp.concatenate([qw_s[dirn, crows[dirn], :], pm_s[dirn, cs_[dirn]]], axis=0)
                         for dirn in range(2)])
        r = _bbdot(lhs, st)
        of_s[crows[0], :] = r[0, :CHUNK, :] + au_s[0, crows[0], :]
        ob_s[crows[1], :] = r[1, :CHUNK, :] + au_s[1, crows[1], :]
        gl = jnp.stack([gl_s[dirn, cs_[dirn]][0:1, :] for dirn in range(2)])
        nn = jnp.stack([n_s[dirn, cs_[dirn]] for dirn in range(2)])
        return gl * st - r[:, CHUNK:, :] + nn

    lax.fori_loop(0, n_c, step, jnp.zeros((2, GDN_DK, GDN_DV), F32))

    o = of_s[...] + ob_s[...]
    o = o * lax.rsqrt(jnp.mean(o * o, axis=-1, keepdims=True) + EPS) * ng_ref[...]
    o_ref[...] = o * _silu(z_ref[...])


def _gdn_mixer(p, lc, conv_w, a_log, dt_bias, norm_g):
    b, lt, _ = p.shape
    n_c = lt // CHUNK
    nh = GDN_HEADS
    kern = functools.partial(_gdn_kernel, lc=lc, lt=lt)
    pcol = lambda base: pl.BlockSpec((None, lt, LANES), lambda i, h, base=base: (i, 0, base + h))
    wcol = lambda base: pl.BlockSpec((CONV_W, LANES), lambda i, h, base=base: (0, base + h))
    vec = pl.BlockSpec((1, LANES), lambda i, h: (0, 0))
    pad = jnp.zeros((LANES - 2 * nh,), F32)
    al = jnp.concatenate([a_log.reshape(-1), pad]).reshape(1, LANES)
    dtb = jnp.concatenate([dt_bias.reshape(-1), pad]).reshape(1, LANES)
    seq = lambda dt: pltpu.VMEM((lt, LANES), dt)
    return pl.pallas_call(
        kern,
        grid=(b, nh),
        in_specs=[pcol(P_QKV // LANES), pcol(P_QKV // LANES + nh), pcol(P_QKV // LANES + 2 * nh),
                  pcol(P_GZ // LANES),
                  pl.BlockSpec((None, lt, LANES), lambda i, h: (i, 0, P_SMALL // LANES)),
                  wcol(0), wcol(nh), wcol(2 * nh), vec, vec, vec],
        out_specs=pl.BlockSpec((None, lt, LANES), lambda i, h: (i, 0, h)),
        out_shape=jax.ShapeDtypeStruct((b, lt, GDN_WIDTH), F32),
        scratch_shapes=[seq(F32), seq(F32), seq(F32), seq(F32), seq(F32), seq(F32), seq(F32),
                        pltpu.VMEM((2, n_c, GDN_DK, GDN_DK), BF16),
                        pltpu.VMEM((2, n_c, GDN_DK, GDN_DV), F32),
                        pltpu.VMEM((2, lt, GDN_DK), BF16),
                        pltpu.VMEM((2, lt, GDN_DV), F32),
                        pltpu.VMEM((2, n_c, SUBLANES, LANES), F32),
                        seq(F32), seq(F32)],
        compiler_params=_cparams(("parallel", "arbitrary")),
        name="gdn_mixer",
    )(p, p, p, p, p, conv_w, conv_w, conv_w, al, dtb, norm_g.reshape(1, LANES))


M2_GW = M2_HPG * M2_HEADDIM
M2_DT_LANE = 4 * GDN_HEADS
M2_KPAD = MXU_TILE


def _ssd_kernel(x_ref, b_ref, c_ref, z_ref, sm_ref, wx_ref, wb_ref, wc_ref, bx_ref, bb_ref, bc_ref,
                al_ref, dtb_ref, dsk_ref, ng_ref, o_ref, xs, bs, cs, dt_s, ac_s, tr_s, lhs_s, rhs_s, h_s,
                *, lc, lt):
    g = pl.program_id(1)
    n_c, n_cc = lt // CHUNK, lc // CHUNK
    row = lax.broadcasted_iota(jnp.int32, (lt, 1), 0)
    lane = lax.broadcasted_iota(jnp.int32, (1, LANES), 1)
    head_of_lane = lax.broadcasted_iota(jnp.int32, (1, M2_GW), 1) // M2_HEADDIM

    masks = _conv_masks(row, lc, lt)
    xs[...] = _dwconv_silu(x_ref[...], wx_ref, bx_ref[...], masks)
    bs[...] = _dwconv_silu(b_ref[...], wb_ref, bb_ref[...], masks)
    cs[...] = _dwconv_silu(c_ref[...], wc_ref, bc_ref[...], masks)

    dt_all = jax.nn.softplus(sm_ref[...] + dtb_ref[...])
    dt_s[...] = dt_all
    ac_s[...] = _chunk_cumsum(dt_all * (-jnp.exp(al_ref[...])), row, lane < M2_DT_LANE + M2_HEADS)

    o_ref[...] = jnp.zeros_like(o_ref)
    h_s[...] = jnp.zeros_like(h_s)
    lhs_s[...] = jnp.zeros_like(lhs_s)
    rhs_s[...] = jnp.zeros_like(rhs_s)

    ii = lax.broadcasted_iota(jnp.int32, (CHUNK, CHUNK), 0)
    jj = lax.broadcasted_iota(jnp.int32, (CHUNK, CHUNK), 1)

    def step(s, carry):
        cs_ = (s, _bwd_chunk(s, n_cc, n_c))
        rws = [pl.ds(pl.multiple_of(c * CHUNK, CHUNK), CHUNK) for c in cs_]
        bc2 = jnp.stack([bs[rws[d], :] for d in range(2)]).astype(BF16)
        cf_ = [cs[rws[d], :] for d in range(2)]
        cb2 = lax.dot_general(jnp.stack(cf_).astype(BF16), bc2, (((2,), (2,)), ((0,), (0,))),
                              preferred_element_type=F32)
        hprev = h_s[...]
        xd_, tot_ = [], []
        for dirn in range(2):
            incl = (ii >= jj) if dirn == 0 else (ii <= jj)
            dtc = dt_s[rws[dirn], :]
            acc = ac_s[rws[dirn], :]
            tr_s[dirn] = jnp.transpose(acc)
            dte = jnp.zeros((CHUNK, M2_GW), F32)
            ace = jnp.zeros((CHUNK, M2_GW), F32)
            for j in range(M2_HPG):
                l_idx = M2_DT_LANE + dirn * M2_HEADS + g * M2_HPG + j
                a_col = _col(acc, lane, l_idx)
                hm = head_of_lane == j
                dte = jnp.where(hm, _col(dtc, lane, l_idx), dte)
                ace = jnp.where(hm, a_col, ace)
                a_row = tr_s[dirn, pl.ds(l_idx, 1), :]
                dec = jnp.where(incl, jnp.exp(jnp.where(incl, a_col - a_row, 0.0)), 0.0)
                blk = pl.ds(j * CHUNK, CHUNK)
                lhs_s[dirn, blk, 0:M2_STATE] = (jnp.exp(a_col) * cf_[dirn]).astype(BF16)
                lhs_s[dirn, blk, M2_STATE:M2_STATE + CHUNK] = (cb2[dirn] * dec).astype(BF16)
            tot = ace[CHUNK - 1:CHUNK, :] if dirn == 0 else ace[0:1, :]
            xdt = xs[rws[dirn], :] * dte
            rhs_s[dirn, 0:M2_STATE, :] = hprev[dirn].astype(BF16)
            rhs_s[dirn, M2_STATE:M2_STATE + CHUNK, :] = xdt.astype(BF16)
            xd_.append((xdt * jnp.exp(tot - ace)).astype(BF16))
            tot_.append(tot)
        r2 = lax.dot_general(lhs_s[...], rhs_s[...], (((2,), (1,)), ((0,), (0,))),
                             preferred_element_type=F32)
        for dirn in range(2):
            y = jnp.where(head_of_lane == 0, r2[dirn, 0:CHUNK, :], 0.0)
            for j in range(1, M2_HPG):
                y = y + jnp.where(head_of_lane == j, r2[dirn, j * CHUNK:(j + 1) * CHUNK, :], 0.0)
            o_ref[rws[dirn], :] += y
        st2 = lax.dot_general(bc2, jnp.stack(xd_), (((1,), (1,)), ((0,), (0,))),
                              preferred_element_type=F32)
        h_s[...] = jnp.exp(jnp.stack(tot_)) * hprev + st2
        return carry

    lax.fori_loop(0, n_c, step, 0, unroll=_largest_divisor(n_c, 2))

    def fin(c, carry):
        rows = pl.ds(pl.multiple_of(c * CHUNK, CHUNK), CHUNK)
        y = (o_ref[rows, :] + dsk_ref[...] * xs[rows, :]) * _silu(z_ref[rows, :])
        o_ref[rows, :] = y * lax.rsqrt(jnp.mean(y * y, axis=-1, keepdims=True) + EPS) * ng_ref[...]
        return carry

    lax.fori_loop(0, n_c, fin, 0)


def _ssd_mixer(p, lc, conv_w, conv_b, a_log, dt_bias, d_skip, norm_g):
    b, lt, _ = p.shape
    kern = functools.partial(_ssd_kernel, lc=lc, lt=lt)
    gw = M2_GW
    bcol = (P_XBC + M2_WIDTH) // LANES
    ccol = bcol + M2_GROUPS * M2_STATE // LANES
    wide = lambda base: pl.BlockSpec((None, lt, gw), lambda i, g, base=base: (i, 0, base + g))
    narrow = lambda base: pl.BlockSpec((None, lt, LANES), lambda i, g, base=base: (i, 0, base + g))
    wwide = lambda rows: pl.BlockSpec((rows, gw), lambda i, g: (0, g))
    wnar = lambda rows, base: pl.BlockSpec((rows, LANES), lambda i, g, base=base: (0, base + g))
    vec = pl.BlockSpec((1, LANES), lambda i, g: (0, 0))
    lead = jnp.zeros((M2_DT_LANE,), F32)
    tail = jnp.zeros((LANES - M2_DT_LANE - 2 * M2_HEADS,), F32)
    al = jnp.concatenate([lead, a_log.reshape(-1), tail]).reshape(1, LANES)
    dtb = jnp.concatenate([lead, dt_bias.reshape(-1), tail]).reshape(1, LANES)
    cbias = conv_b.reshape(1, M2_CONV_CH)
    wb0 = M2_WIDTH // LANES
    wc0 = wb0 + M2_GROUPS * M2_STATE // LANES
    return pl.pallas_call(
        kern,
        grid=(b, M2_GROUPS),
        in_specs=[wide(P_XBC // gw), narrow(bcol), narrow(ccol), wide(P_MZ // gw),
                  pl.BlockSpec((None, lt, LANES), lambda i, g: (i, 0, P_SMALL // LANES)),
                  wwide(CONV_W), wnar(CONV_W, wb0), wnar(CONV_W, wc0),
                  wwide(1), wnar(1, wb0), wnar(1, wc0),
                  vec, vec, wwide(1), wwide(1)],
        out_specs=pl.BlockSpec((None, lt, gw), lambda i, g: (i, 0, g)),
        out_shape=jax.ShapeDtypeStruct((b, lt, M2_WIDTH), F32),
        scratch_shapes=[pltpu.VMEM((lt, gw), F32), pltpu.VMEM((lt, LANES), F32), pltpu.VMEM((lt, LANES), F32),
                        pltpu.VMEM((lt, LANES), F32), pltpu.VMEM((lt, LANES), F32),
                        pltpu.VMEM((2, LANES, CHUNK), F32),
                        pltpu.VMEM((2, M2_HPG * CHUNK, M2_KPAD), BF16),
                        pltpu.VMEM((2, M2_KPAD, gw), BF16),
                        pltpu.VMEM((2, M2_STATE, gw), F32)],
        compiler_params=_cparams(("parallel", "arbitrary")),
        name="ssd_mixer",
    )(p, p, p, p, p, conv_w, conv_w, conv_w, cbias, cbias, cbias, al, dtb,
      jnp.repeat(d_skip, M2_HEADDIM).reshape(1, M2_WIDTH), norm_g.reshape(1, M2_WIDTH))


def _out_proj_kernel(a_ref, b_ref, m_ref, x_ref, mod_ref, w_ref, o_ref, *, lc, tm):
    r = pl.program_id(1)
    acc = jnp.dot(a_ref[...].astype(BF16), w_ref[0:S5_WIDTH, :], preferred_element_type=F32)
    acc += jnp.dot(b_ref[...].astype(BF16), w_ref[S5_WIDTH:S5_WIDTH + GDN_WIDTH, :],
                   preferred_element_type=F32)
    acc += jnp.dot(m_ref[...].astype(BF16), w_ref[S5_WIDTH + GDN_WIDTH:, :],
                   preferred_element_type=F32)
    row = r * tm + lax.broadcasted_iota(jnp.int32, (tm, 1), 0)
    o_ref[...] = x_ref[...] + _mod_vec(mod_ref, 2, row < lc) * acc


def _out_proj(a, bm, m, x, mod, w_out_bf, layer, lc):
    b, lt, d = x.shape
    tm = lt // 4 if (lt // 4) % 8 == 0 and lc <= lt // 4 else lt // 2
    kern = functools.partial(_out_proj_kernel, lc=lc, tm=tm)
    tok = lambda w: pl.BlockSpec((None, tm, w), lambda i, r: (i, r, 0))
    return pl.pallas_call(
        kern,
        grid=(b, lt // tm),
        in_specs=[tok(S5_WIDTH), tok(GDN_WIDTH), tok(M2_WIDTH), tok(d),
                  pl.BlockSpec((None, 2, 6, d), lambda i, r: (i, 0, 0, 0)),
                  pl.BlockSpec((None, MIX_WIDTH, d), lambda i, r: (layer, 0, 0))],
        out_specs=tok(d),
        out_shape=jax.ShapeDtypeStruct((b, lt, d), F32),
        compiler_params=_cparams(("parallel", "parallel")),
        name="out_proj",
    )(a, bm, m, x, mod, w_out_bf)


def _swiglu_acc(h, wg_ref, wu_ref, wd_ref):
    acc = jnp.zeros((h.shape[0], D_MODEL), F32)
    for f in range(FF_NF):
        cols = slice(f * FF_TF, (f + 1) * FF_TF)
        g = jnp.dot(h, wg_ref[:, cols], preferred_element_type=F32)
        u = jnp.dot(h, wu_ref[:, cols], preferred_element_type=F32)
        acc = acc + jnp.dot((_silu(g) * u).astype(BF16), wd_ref[cols, :], preferred_element_type=F32)
    return acc


def _ffn_dense_kernel(x_ref, mod_ref, g_ref, wg_ref, wu_ref, wd_ref, o_ref, *, lc, tm):
    r = pl.program_id(1)
    x = x_ref[...]
    row = r * tm + lax.broadcasted_iota(jnp.int32, (tm, 1), 0)
    is_ctx = row < lc
    h = (_rms(x) * g_ref[...]) * (1.0 + _mod_vec(mod_ref, 4, is_ctx)) + _mod_vec(mod_ref, 3, is_ctx)
    acc = _swiglu_acc(h.astype(BF16), wg_ref, wu_ref, wd_ref)
    o_ref[...] = x + _mod_vec(mod_ref, 5, is_ctx) * acc


def _tok_tile(lc, l):
    return math.gcd(math.gcd(lc, l), 256)


def _ffn_dense(x, mod, g, wg, wu, wd, layer, lc):
    b, lt, d = x.shape
    tm = lt // 3 if lt % (3 * SUBLANES) == 0 else _tok_tile(lc, lt - lc)
    kern = functools.partial(_ffn_dense_kernel, lc=lc, tm=tm)
    tok = pl.BlockSpec((None, tm, d), lambda i, r: (i, r, 0))
    wspec = lambda shp: pl.BlockSpec((None,) + shp, lambda i, r: (layer, 0, 0))
    return pl.pallas_call(
        kern,
        grid=(b, lt // tm),
        in_specs=[tok, pl.BlockSpec((None, 2, 6, d), lambda i, r: (i, 0, 0, 0)),
                  pl.BlockSpec((1, d), lambda i, r: (0, 0)),
                  wspec((d, D_FF)), wspec((d, D_FF)), wspec((D_FF, d))],
        out_specs=tok,
        out_shape=jax.ShapeDtypeStruct((b, lt, d), F32),
        compiler_params=_cparams(("parallel", "parallel")),
        name="ffn_dense",
    )(x, mod, g.reshape(1, d), wg, wu, wd)


def _router_kernel(x_ref, mod_ref, g_ref, rw_ref, h_ref, rt_ref, *, lc, tm):
    r = pl.program_id(1)
    x = x_ref[...]
    row = r * tm + lax.broadcasted_iota(jnp.int32, (tm, 1), 0)
    is_ctx = row < lc
    h = (_rms(x) * g_ref[...]) * (1.0 + _mod_vec(mod_ref, 4, is_ctx)) + _mod_vec(mod_ref, 3, is_ctx)
    h_ref[...] = h
    logits = jnp.dot(h, rw_ref[...], preferred_element_type=F32, precision=lax.Precision.HIGHEST)
    lane = lax.broadcasted_iota(jnp.int32, logits.shape, 1)
    neg = jnp.float32(-jnp.inf)
    lg = jnp.where(lane < N_EXPERTS, logits, neg)
    m1 = jnp.max(lg, axis=-1, keepdims=True)
    i1 = jnp.min(jnp.where(lg == m1, lane, LANES), axis=-1, keepdims=True)
    lg2 = jnp.where(lane == i1, neg, lg)
    m2 = jnp.max(lg2, axis=-1, keepdims=True)
    i2 = jnp.min(jnp.where(lg2 == m2, lane, LANES), axis=-1, keepdims=True)
    e2 = jnp.exp(m2 - m1)
    den = 1.0 + e2
    out = jnp.where(lane == 0, i1.astype(F32), 0.0)
    out = jnp.where(lane == 1, i2.astype(F32), out)
    out = jnp.where(lane == 2, 1.0 / den, out)
    out = jnp.where(lane == 3, e2 / den, out)
    rt_ref[...] = out


def _router(x, mod, g, router_w, lc):
    b, lt, d = x.shape
    tm = _tok_tile(lc, lt - lc)
    rw = jnp.concatenate([router_w, jnp.zeros((d, LANES - N_EXPERTS), F32)], axis=1)
    kern = functools.partial(_router_kernel, lc=lc, tm=tm)
    tok = lambda w: pl.BlockSpec((None, tm, w), lambda i, r: (i, r, 0))
    return pl.pallas_call(
        kern,
        grid=(b, lt // tm),
        in_specs=[tok(d), pl.BlockSpec((None, 2, 6, d), lambda i, r: (i, 0, 0, 0)),
                  pl.BlockSpec((1, d), lambda i, r: (0, 0)),
                  pl.BlockSpec((d, LANES), lambda i, r: (0, 0))],
        out_specs=[tok(d), tok(LANES)],
        out_shape=[jax.ShapeDtypeStruct((b, lt, d), F32), jax.ShapeDtypeStruct((b, lt, LANES), F32)],
        compiler_params=_cparams(("parallel", "parallel")),
        name="moe_router",
    )(x, mod, g.reshape(1, d), rw)


def _row_copy(src_hbm, dst, idx, r, sem):
    return pltpu.make_async_copy(src_hbm.at[pl.ds(idx, 1), :], dst.at[pl.ds(r, 1), :], sem)


def _expert_kernel(te_ref, nt_ref, idx_ref, idx_next_ref, h_hbm, wg_ref, wu_ref, wd_ref, o_ref, xbuf, sems):
    t = pl.program_id(0)
    nt = nt_ref[0]
    slot = lax.rem(t, 2)

    def gather(idx, s, start):
        def body(r, c):
            cp = _row_copy(h_hbm, xbuf.at[s], idx[r] if start else 0, r, sems.at[s])
            cp.start() if start else cp.wait()
            return c
        lax.fori_loop(0, MOE_TM, body, 0, unroll=8)

    @pl.when((t == 0) & (nt > 0))
    def _():
        gather(idx_ref, 0, True)

    @pl.when(t + 1 < nt)
    def _():
        gather(idx_next_ref, 1 - slot, True)

    @pl.when(t < nt)
    def _():
        gather(idx_ref, slot, False)
        o_ref[...] = _swiglu_acc(xbuf[slot].astype(BF16), wg_ref, wu_ref, wd_ref)

    @pl.when(t >= nt)
    def _():
        o_ref[...] = jnp.zeros_like(o_ref)


def _experts(h2, src_row, tile_expert, n_tiles_used, wg, wu, wd, layer):
    rp = src_row.shape[0]
    d = h2.shape[1]
    n_tiles = rp // MOE_TM
    wspec = lambda shp: pl.BlockSpec((None, None) + shp, lambda t, te, nt: (layer, te[t], 0, 0))
    grid_spec = pltpu.PrefetchScalarGridSpec(
        num_scalar_prefetch=2,
        grid=(n_tiles,),
        in_specs=[pl.BlockSpec((MOE_TM,), lambda t, te, nt: (t,), memory_space=pltpu.SMEM),
                  pl.BlockSpec((MOE_TM,), lambda t, te, nt: (jnp.minimum(t + 1, n_tiles - 1),),
                               memory_space=pltpu.SMEM),
                  pl.BlockSpec(memory_space=pl.ANY),
                  wspec((d, D_FF)), wspec((d, D_FF)), wspec((D_FF, d))],
        out_specs=pl.BlockSpec((MOE_TM, d), lambda t, te, nt: (t, 0)),
        scratch_shapes=[pltpu.VMEM((2, MOE_TM, d), F32), pltpu.SemaphoreType.DMA((2,))],
    )
    return pl.pallas_call(
        _expert_kernel,
        grid_spec=grid_spec,
        out_shape=jax.ShapeDtypeStruct((rp, d), F32),
        compiler_params=_cparams(("arbitrary",)),
        name="moe_experts",
    )(tile_expert, n_tiles_used, src_row, src_row, h2, wg, wu, wd)


def _combine_kernel(p1_ref, p2_ref, y_hbm, x_ref, rt_ref, mod_ref, o_ref, ya, yb, sem, *, lc, tm):
    r = pl.program_id(1)

    def start(k, c):
        _row_copy(y_hbm, ya, p1_ref[k], k, sem).start(priority=0)
        _row_copy(y_hbm, yb, p2_ref[k], k, sem).start(priority=1)
        return c
    lax.fori_loop(0, tm, start, 0)

    def wait(k, c):
        _row_copy(y_hbm, ya, 0, k, sem).wait()
        _row_copy(y_hbm, yb, 0, k, sem).wait()
        return c
    lax.fori_loop(0, tm, wait, 0)

    row = r * tm + lax.broadcasted_iota(jnp.int32, (tm, 1), 0)
    rt = rt_ref[...]
    y = rt[:, 2:3] * ya[...] + rt[:, 3:4] * yb[...]
    o_ref[...] = x_ref[...] + _mod_vec(mod_ref, 5, row < lc) * y


def _combine(p1, p2, y, x, rt, mod, lc):
    b, lt, d = x.shape
    tm = _tok_tile(lc, lt - lc)
    nt = lt // tm
    kern = functools.partial(_combine_kernel, lc=lc, tm=tm)
    tok = lambda w: pl.BlockSpec((None, tm, w), lambda i, r: (i, r, 0))
    ispec = pl.BlockSpec((tm,), lambda i, r: (i * nt + r,), memory_space=pltpu.SMEM)
    return pl.pallas_call(
        kern,
        grid=(b, nt),
        in_specs=[ispec, ispec, pl.BlockSpec(memory_space=pl.ANY), tok(d), tok(LANES),
                  pl.BlockSpec((None, 2, 6, d), lambda i, r: (i, 0, 0, 0))],
        out_specs=tok(d),
        out_shape=jax.ShapeDtypeStruct((b, lt, d), F32),
        scratch_shapes=[pltpu.VMEM((tm, d), F32), pltpu.VMEM((tm, d), F32),
                        pltpu.SemaphoreType.DMA(())],
        compiler_params=_cparams(("arbitrary", "arbitrary")),
        name="moe_combine",
    )(p1, p2, y, x, rt, mod)


def _moe(x, mod, g, router_w, wg, wu, wd, layer, lc):
    b, lt, d = x.shape
    n = b * lt
    h2, rt = _router(x, mod, g, router_w, lc)
    rt2 = rt.reshape(n, LANES)
    e_flat = jnp.concatenate([rt2[:, 0], rt2[:, 1]]).astype(jnp.int32)
    onehot = (e_flat[:, None] == jnp.arange(N_EXPERTS, dtype=jnp.int32)[None, :]).astype(jnp.int32)
    counts = jnp.sum(onehot, axis=0)
    rank = jnp.sum((jnp.cumsum(onehot, axis=0) - 1) * onehot, axis=1)
    padded = ((counts + MOE_TM - 1) // MOE_TM) * MOE_TM
    ends = jnp.cumsum(padded)
    pos = (ends - padded)[e_flat] + rank
    rp = ((2 * n + MOE_TM - 1) // MOE_TM + N_EXPERTS) * MOE_TM
    tok_id = jnp.concatenate([jnp.arange(n, dtype=jnp.int32)] * 2)
    src_row = jnp.zeros((rp,), jnp.int32).at[pos].set(tok_id)
    tile_start = jnp.arange(rp // MOE_TM, dtype=jnp.int32) * MOE_TM
    tile_expert = jnp.minimum(jnp.sum(tile_start[:, None] >= ends[None, :], axis=1),
                              N_EXPERTS - 1).astype(jnp.int32)
    n_used = (ends[-1] // MOE_TM).astype(jnp.int32).reshape(1)
    ys = _experts(h2.reshape(n, d), src_row, tile_expert, n_used, wg, wu, wd, layer)
    return _combine(pos[:n].astype(jnp.int32), pos[n:].astype(jnp.int32), ys, x, rt, mod, lc)


def _final_norm_kernel(x_ref, g_ref, o_ref):
    o_ref[...] = _rms(x_ref[...]) * g_ref[...]


def _final_norm(x, g):
    b, l, d = x.shape
    tm = math.gcd(l, 512)
    tok = pl.BlockSpec((None, tm, d), lambda i, r: (i, r, 0))
    return pl.pallas_call(
        _final_norm_kernel,
        grid=(b, l // tm),
        in_specs=[tok, pl.BlockSpec((1, d), lambda i, r: (0, 0))],
        out_specs=tok,
        out_shape=jax.ShapeDtypeStruct((b, l, d), F32),
        compiler_params=_cparams(("parallel", "parallel")),
        name="final_norm",
    )(x, g.reshape(1, d))


def _grid_t(xl, rows, cols):
    b, l, ch = xl.shape
    return xl.reshape(b, rows, cols, ch).transpose(0, 2, 1, 3).reshape(b, l, ch)


def kernel(x, c, ctx, c_ctx, ada_w, ada_b, norm1_g, norm2_g, w_in, w_out,
           s5_lam_re, s5_lam_im, s5_b_re, s5_b_im, s5_c_re, s5_c_im, s5_log_dt, s5_d, s5_w_glu,
           gdn_conv_w, gdn_a_log, gdn_dt_bias, gdn_norm_g,
           m2_conv_w, m2_conv_b, m2_a_log, m2_dt_bias, m2_d, m2_norm_g,
           ffn_w_gate, ffn_w_up, ffn_w_down,
           moe_router, moe_w_gate, moe_w_up, moe_w_down, final_norm_g):
    b, l, d = x.shape
    lc = ctx.shape[1]
    lt = lc + l
    depth = ada_w.shape[0]
    rows = l // GRID_W
    assert b % SUBLANES == 0 and lc % CHUNK == 0 and l % CHUNK == 0

    cpad = jnp.zeros((b + SUBLANES, d), F32).at[:b].set(c).at[b].set(c_ctx)
    mods = _ada_all(cpad, ada_w, ada_b).reshape(depth, b + SUBLANES, 6, d)

    mod_all = jnp.stack([jnp.broadcast_to(mods[:, b][:, None], (depth, b, 6, d)), mods[:, :b]], axis=2)
    w_in_p = jax.vmap(_pack_w_in)(w_in)
    w_out_b = w_out.astype(BF16)
    s5_glu_b = s5_w_glu.astype(BF16)
    s5_par_all = jax.vmap(_s5_params)(s5_lam_re, s5_lam_im, s5_b_re, s5_b_im, s5_c_re, s5_c_im, s5_log_dt)
    ffn_b = [w.astype(BF16) for w in (ffn_w_gate, ffn_w_up, ffn_w_down)]
    moe_b = [w.astype(BF16) for w in (moe_w_gate, moe_w_up, moe_w_down)]

    xt = jnp.concatenate([ctx, x], axis=1)
    col_major = False
    for i in range(depth):
        want_cm = i % 2 == 1
        if want_cm != col_major:
            lat = xt[:, lc:]
            lat = _grid_t(lat, rows, GRID_W) if want_cm else _grid_t(lat, GRID_W, rows)
            xt = jnp.concatenate([xt[:, :lc], lat], axis=1)
            col_major = want_cm
        mod = mod_all[i]

        p = _in_proj(xt, mod, norm1_g[i], w_in_p, i, lc)
        u_tb = jnp.transpose(p[:, :, P_U:P_U + S5_WIDTH], (1, 0, 2)).reshape(lt * b, S5_WIDTH)
        ya = _s5_mixer(u_tb, lc, b, s5_par_all, s5_d[i], s5_glu_b, i)
        ya = jnp.transpose(ya.reshape(lt, b, S5_WIDTH), (1, 0, 2))
        yb = _gdn_mixer(p, lc, gdn_conv_w[i], gdn_a_log[i], gdn_dt_bias[i], gdn_norm_g[i])
        ym = _ssd_mixer(p, lc, m2_conv_w[i], m2_conv_b[i], m2_a_log[i], m2_dt_bias[i], m2_d[i],
                        m2_norm_g[i])
        xt = _out_proj(ya, yb, ym, xt, mod, w_out_b, i, lc)

        j = i // 2
        if i % 2 == 0:
            xt = _ffn_dense(xt, mod, norm2_g[i], ffn_b[0], ffn_b[1], ffn_b[2], j, lc)
        else:
            xt = _moe(xt, mod, norm2_g[i], moe_router[j], moe_b[0], moe_b[1], moe_b[2], j, lc)

    lat = xt[:, lc:]
    if col_major:
        lat = _grid_t(lat, GRID_W, rows)
    return _final_norm(lat, final_norm_g)
```

```python
import functools
import math

import jax
import jax.numpy as jnp
from jax import lax
from jax.experimental import pallas as pl
from jax.experimental.pallas import tpu as pltpu

F32 = jnp.float32
BF16 = jnp.bfloat16

D_MODEL = 1024
GRID_W = 64
EPS = 1e-6
CHUNK = 64
CONV_W = 5
S5_WIDTH = D_MODEL // 2
S5_GROUP = 16
S5_GROUPS = S5_WIDTH // S5_GROUP
S5_STATE = 64
GDN_DK = 128
GDN_DV = 128
GDN_WIDTH = D_MODEL // 2
GDN_HEADS = GDN_WIDTH // GDN_DV
GDN_CONV_CH = 2 * GDN_HEADS * GDN_DK + GDN_WIDTH
M2_WIDTH = D_MODEL
M2_HEADDIM = 64
M2_HEADS = M2_WIDTH // M2_HEADDIM
M2_GROUPS = 2
M2_HPG = M2_HEADS // M2_GROUPS
M2_STATE = 128
M2_CONV_CH = M2_WIDTH + 2 * M2_GROUPS * M2_STATE
MIX_WIDTH = S5_WIDTH + GDN_WIDTH + M2_WIDTH
IN_SIZES = (S5_WIDTH, GDN_CONV_CH, GDN_WIDTH, 2 * GDN_HEADS, 2 * GDN_HEADS,
            M2_WIDTH, M2_CONV_CH, 2 * M2_HEADS)
D_FF = 256 * ((8 * D_MODEL // 3 + 255) // 256)
N_EXPERTS = 8

LANES = 128
SUBLANES = 8
MXU_TILE = 256
VMEM_LIMIT = 56 * 1024 * 1024

P_U = 0
P_GZ = P_U + S5_WIDTH
P_QKV = P_GZ + GDN_WIDTH
P_MZ = P_QKV + GDN_CONV_CH
P_XBC = P_MZ + M2_WIDTH
P_SMALL = P_XBC + M2_CONV_CH
IN_TN = 1792
NP = 3 * IN_TN
assert P_SMALL + LANES <= NP

FF_TF = 256
FF_NF = D_FF // FF_TF
MOE_TM = 512


def _cparams(sem):
    return pltpu.CompilerParams(dimension_semantics=sem, vmem_limit_bytes=VMEM_LIMIT)


def _silu(x):
    return x * jax.nn.sigmoid(x)


def _rms(x):
    return x * lax.rsqrt(jnp.mean(x * x, axis=-1, keepdims=True) + EPS)


def _mod_vec(mod_ref, k, is_ctx):
    return jnp.where(is_ctx, mod_ref[0, k:k + 1, :], mod_ref[1, k:k + 1, :])


def _largest_divisor(n, cap):
    return max(u for u in range(1, cap + 1) if n % u == 0)


_TDOT = (((0,), (0,)), ((), ()))
_NTDOT = (((1,), (1,)), ((), ()))


def _bdot(a, b):
    return jnp.dot(a.astype(BF16), b.astype(BF16), preferred_element_type=F32)


def _bbdot(a, b):
    return lax.dot_general(a.astype(BF16), b.astype(BF16), (((2,), (1,)), ((0,), (0,))),
                           preferred_element_type=F32)


def _ada_kernel(c_ref, w_ref, b_ref, o_ref):
    o_ref[...] = _bdot(_silu(c_ref[...]), w_ref[...]) + b_ref[...]


def _ada_all(cpad, ada_w, ada_b):
    depth, d, n6 = ada_w.shape
    tn = 1536
    rows = cpad.shape[0]
    return pl.pallas_call(
        _ada_kernel,
        grid=(depth, n6 // tn),
        in_specs=[pl.BlockSpec((rows, d), lambda i, j: (0, 0)),
                  pl.BlockSpec((None, d, tn), lambda i, j: (i, 0, j)),
                  pl.BlockSpec((None, 1, tn), lambda i, j: (i, 0, j))],
        out_specs=pl.BlockSpec((None, rows, tn), lambda i, j: (i, 0, j)),
        out_shape=jax.ShapeDtypeStruct((depth, rows, n6), F32),
        compiler_params=_cparams(("parallel", "parallel")),
        name="ada_mod",
    )(cpad, ada_w, ada_b.reshape(depth, 1, n6))


def _in_proj_kernel(x_ref, mod_ref, g_ref, w_ref, o_ref, h_scr, *, lc, tm, rc):
    r = pl.program_id(1)

    @pl.when(pl.program_id(2) == 0)
    def _():
        def body(c, carry):
            r0 = pl.multiple_of(c * rc, rc)
            x = x_ref[pl.ds(r0, rc), :]
            row = r * tm + r0 + lax.broadcasted_iota(jnp.int32, (rc, 1), 0)
            is_ctx = row < lc
            h = (_rms(x) * g_ref[...]) * (1.0 + _mod_vec(mod_ref, 1, is_ctx)) + _mod_vec(mod_ref, 0, is_ctx)
            h_scr[pl.ds(r0, rc), :] = h.astype(BF16)
            return carry
        lax.fori_loop(0, tm // rc, body, 0)

    o_ref[...] = jnp.dot(h_scr[...], w_ref[...], preferred_element_type=F32)


def _in_proj(x, mod, g, w_packed, layer, lc):
    b, lt, d = x.shape
    tm = lt // 2
    rc = math.gcd(tm, 128)
    assert lc <= tm and tm % 16 == 0
    kern = functools.partial(_in_proj_kernel, lc=lc, tm=tm, rc=rc)
    return pl.pallas_call(
        kern,
        grid=(b, 2, NP // IN_TN),
        in_specs=[pl.BlockSpec((None, tm, d), lambda i, r, j: (i, r, 0)),
                  pl.BlockSpec((None, 2, 6, d), lambda i, r, j: (i, 0, 0, 0)),
                  pl.BlockSpec((1, d), lambda i, r, j: (0, 0)),
                  pl.BlockSpec((None, d, IN_TN), lambda i, r, j: (layer, 0, j))],
        out_specs=pl.BlockSpec((None, tm, IN_TN), lambda i, r, j: (i, r, j)),
        out_shape=jax.ShapeDtypeStruct((b, lt, NP), F32),
        scratch_shapes=[pltpu.VMEM((tm, d), BF16)],
        compiler_params=_cparams(("parallel", "parallel", "arbitrary")),
        name="in_proj",
    )(x, mod, g.reshape(1, d), w_packed)


def _pack_w_in(w):
    o = [0]
    for s in IN_SIZES:
        o.append(o[-1] + s)
    u, qkv, gz, al, be, mz, xbc, dt = [w[:, o[i]:o[i + 1]] for i in range(8)]
    small_pad = jnp.zeros((w.shape[0], LANES - 4 * GDN_HEADS - 2 * M2_HEADS), w.dtype)
    tail = jnp.zeros((w.shape[0], NP - P_SMALL - LANES), w.dtype)
    return jnp.concatenate([u, gz, qkv, mz, xbc, al, be, dt, small_pad, tail], axis=1).astype(BF16)


S5_HALF = S5_WIDTH // 2
S5_HS = (S5_GROUPS // 2) * S5_STATE
S5_LQ = 512


def _s5_scan_kernel(u_ref, bk_ref, ck_ref, a_ref, y_ref, s_scr, h_scr, *, tt, nb):
    d = pl.program_id(0)

    @pl.when(pl.program_id(1) == 0)
    def _():
        h_scr[...] = jnp.zeros_like(h_scr)

    u = u_ref[...].astype(BF16)
    for k in range(2):
        s_scr[:, k * 2 * S5_HS:(k + 1) * 2 * S5_HS] = jnp.dot(
            u[:, k * S5_HALF:(k + 1) * S5_HALF], bk_ref[k], preferred_element_type=F32)

    for k in range(2):
        for q in range(S5_HS // S5_LQ):
            re0 = k * 2 * S5_HS + q * S5_LQ
            im0 = re0 + S5_HS
            a_re = jnp.broadcast_to(a_ref[2 * k:2 * k + 1, q * S5_LQ:(q + 1) * S5_LQ], (nb, S5_LQ))
            a_im = jnp.broadcast_to(a_ref[2 * k + 1:2 * k + 2, q * S5_LQ:(q + 1) * S5_LQ], (nb, S5_LQ))

            def step(i, carry, re0=re0, im0=im0, a_re=a_re, a_im=a_im):
                hr, hi = carry
                t = i + d * (tt - 1 - 2 * i)
                r = pl.multiple_of(t * nb, nb)
                nr = a_re * hr - a_im * hi + s_scr[pl.ds(r, nb), re0:re0 + S5_LQ]
                ni = a_re * hi + a_im * hr + s_scr[pl.ds(r, nb), im0:im0 + S5_LQ]
                s_scr[pl.ds(r, nb), re0:re0 + S5_LQ] = nr
                s_scr[pl.ds(r, nb), im0:im0 + S5_LQ] = ni
                return nr, ni

            hr, hi = lax.fori_loop(0, tt, step, (h_scr[:, re0:re0 + S5_LQ], h_scr[:, im0:im0 + S5_LQ]),
                                   unroll=8)
            h_scr[:, re0:re0 + S5_LQ] = hr
            h_scr[:, im0:im0 + S5_LQ] = hi

    for k in range(2):
        y_ref[:, k * S5_HALF:(k + 1) * S5_HALF] = jnp.dot(
            s_scr[:, k * 2 * S5_HS:(k + 1) * 2 * S5_HS].astype(BF16), ck_ref[k],
            preferred_element_type=F32)


def _s5_fin_kernel(y_ref, u_ref, d_ref, w_ref, o_ref):
    y = y_ref[0] + y_ref[1] + d_ref[...] * u_ref[...]
    y = jax.nn.gelu(y)
    o_ref[...] = y * jax.nn.sigmoid(jnp.dot(y.astype(BF16), w_ref[...], preferred_element_type=F32))


def _s5_params(lam_re, lam_im, b_re, b_im, c_re, c_im, log_dt):
    lam = lax.complex(jnp.minimum(lam_re, -1e-4), lam_im)
    dt = jnp.exp(log_dt)[..., None]
    lam_bar = jnp.exp(lam * dt)
    gamma = (lam_bar - 1.0) / lam
    bt = lax.complex(b_re, b_im) * gamma[..., None]
    gh = S5_GROUPS // 2
    eye = jnp.eye(gh, dtype=F32)

    def bd_in(m):
        m = m.reshape(2, 2, gh, S5_STATE, S5_GROUP)
        return jnp.einsum("gh,dkgpi->dkgihp", eye, m).reshape(2, 2, gh * S5_GROUP, gh * S5_STATE)

    def bd_out(m):
        m = m.reshape(2, 2, gh, S5_GROUP, S5_STATE)
        return jnp.einsum("gh,dkgip->dkgphi", eye, m).reshape(2, 2, gh * S5_STATE, gh * S5_GROUP)

    bk = jnp.concatenate([bd_in(bt.real), bd_in(bt.imag)], axis=-1).astype(BF16)
    ck = jnp.concatenate([bd_out(c_re), -bd_out(c_im)], axis=-2).astype(BF16)
    lr = lam_bar.real.reshape(2, 2, 1, S5_HS)
    li = lam_bar.imag.reshape(2, 2, 1, S5_HS)
    a = jnp.concatenate([lr, li], axis=2).reshape(2, 4, S5_HS)
    return bk, ck, a


def _s5_mixer(u_tb, lc, nb, params, d_skip, w_glu, layer):
    n, w = u_tb.shape
    lt = n // nb
    tt = CHUNK
    n_c, n_t = lc // tt, lt // tt
    bk, ck, a = params

    def tile(dd, s):
        return jnp.where(dd == 0, s, jnp.where(s < n_c, n_c - 1 - s, n_t - 1 - s + n_c))

    kern = functools.partial(_s5_scan_kernel, tt=tt, nb=nb)
    y = pl.pallas_call(
        kern,
        grid=(2, n_t),
        in_specs=[pl.BlockSpec((tt * nb, w), lambda dd, s: (tile(dd, s), 0)),
                  pl.BlockSpec((None, None, 2, S5_HALF, 2 * S5_HS), lambda dd, s: (layer, dd, 0, 0, 0)),
                  pl.BlockSpec((None, None, 2, 2 * S5_HS, S5_HALF), lambda dd, s: (layer, dd, 0, 0, 0)),
                  pl.BlockSpec((None, None, 4, S5_HS), lambda dd, s: (layer, dd, 0, 0))],
        out_specs=pl.BlockSpec((None, tt * nb, w), lambda dd, s: (dd, tile(dd, s), 0)),
        out_shape=jax.ShapeDtypeStruct((2, n, w), F32),
        scratch_shapes=[pltpu.VMEM((tt * nb, 4 * S5_HS), F32), pltpu.VMEM((nb, 4 * S5_HS), F32)],
        compiler_params=_cparams(("arbitrary", "arbitrary")),
        name="s5_scan",
    )(u_tb, bk, ck, a)

    tm = math.gcd(n, 512)
    return pl.pallas_call(
        _s5_fin_kernel,
        grid=(n // tm,),
        in_specs=[pl.BlockSpec((2, tm, w), lambda i: (0, i, 0)),
                  pl.BlockSpec((tm, w), lambda i: (i, 0)),
                  pl.BlockSpec((1, w), lambda i: (0, 0)),
                  pl.BlockSpec((None, w, w), lambda i: (layer, 0, 0))],
        out_specs=pl.BlockSpec((tm, w), lambda i: (i, 0)),
        out_shape=jax.ShapeDtypeStruct((n, w), F32),
        compiler_params=_cparams(("parallel",)),
        name="s5_finish",
    )(y, u_tb, d_skip.reshape(1, w), w_glu)


def _conv_masks(row, lc, lt):
    pad = CONV_W // 2
    masks = {}
    for s in range(-pad, pad + 1):
        tgt = row + s
        masks[s] = (tgt >= 0) & (tgt < lt) & ((row < lc) == (tgt < lc))
    return masks


def _dwconv_silu(x, w_ref, bias, masks):
    lt = x.shape[0]
    pad = CONV_W // 2
    acc = w_ref[pad:pad + 1, :] * x
    for j in range(CONV_W):
        s = j - pad
        if s != 0:
            acc = acc + w_ref[j:j + 1, :] * jnp.where(masks[s], pltpu.roll(x, (-s) % lt, 0), 0.0)
    if bias is not None:
        acc = acc + bias
    return _silu(acc)


def _chunk_cumsum(x, row, fwd_lanes):
    lt = x.shape[0]
    pos = row & (CHUNK - 1)
    s = 1
    while s < CHUNK:
        dn = jnp.where(pos >= s, pltpu.roll(x, s, 0), 0.0)
        up = jnp.where(pos < CHUNK - s, pltpu.roll(x, lt - s, 0), 0.0)
        x = x + jnp.where(fwd_lanes, dn, up)
        s *= 2
    return x


def _col(arr, lane, idx):
    return jnp.sum(jnp.where(lane == idx, arr, 0.0), axis=-1, keepdims=True)


def _bwd_chunk(s, n_cc, n_c):
    return jnp.where(s < n_cc, n_cc - 1 - s, n_c - 1 - s + n_cc)


GDN_GROUP = MXU_TILE // CHUNK
GDN_GROUPS_PER_PASS = 3


def _gdn_kernel(q_ref, k_ref, v_ref, z_ref, sm_ref, wq_ref, wk_ref, wv_ref, al_ref, dtb_ref, ng_ref,
                o_ref, qs, ks, vs, gcf_s, gcb_s, bf_s, bb_s, pm_s, n_s, qw_s, au_s, gl_s, of_s, ob_s,
                *, lc, lt):
    h = pl.program_id(1)
    n_c, n_cc = lt // CHUNK, lc // CHUNK
    row = lax.broadcasted_iota(jnp.int32, (lt, 1), 0)
    lane = lax.broadcasted_iota(jnp.int32, (1, LANES), 1)

    def l2n(t):
        return t * lax.rsqrt(jnp.sum(t * t, axis=-1, keepdims=True) + EPS)

    masks = _conv_masks(row, lc, lt)
    qs[...] = l2n(_dwconv_silu(q_ref[...], wq_ref, None, masks)) * (GDN_DK ** -0.5)
    ks[...] = l2n(_dwconv_silu(k_ref[...], wk_ref, None, masks))
    vs[...] = _dwconv_silu(v_ref[...], wv_ref, None, masks)

    sm = sm_ref[...]
    g_all = -jnp.exp(al_ref[...]) * jax.nn.softplus(sm + dtb_ref[...])
    gc_all = _chunk_cumsum(g_all, row, lane < GDN_HEADS)
    beta_all = jax.nn.sigmoid(sm)
    full = (lt, LANES)
    gcf_s[...] = jnp.broadcast_to(_col(gc_all, lane, h), full)
    gcb_s[...] = jnp.broadcast_to(_col(gc_all, lane, GDN_HEADS + h), full)
    bf_s[...] = jnp.broadcast_to(_col(beta_all, lane, 2 * GDN_HEADS + h), full)
    bb_s[...] = jnp.broadcast_to(_col(beta_all, lane, 3 * GDN_HEADS + h), full)

    gc_n = _largest_divisor(n_c, GDN_GROUP)
    gs = gc_n * CHUNK
    ii = lax.broadcasted_iota(jnp.int32, (gs, gs), 0)
    jj = lax.broadcasted_iota(jnp.int32, (gs, gs), 1)
    same = (ii // CHUNK) == (jj // CHUNK)
    eye = (ii == jj).astype(F32)
    col_chunk = lax.broadcasted_iota(jnp.int32, (GDN_DK, gs), 1) // CHUNK

    n_g = n_c // gc_n
    gb = _largest_divisor(n_g, GDN_GROUPS_PER_PASS)

    def prep(it, carry):
        xs_, rhs_, lhs_, meta = [], [], [], []
        for u in range(gb):
            gi = it * gb + u
            rows = pl.ds(pl.multiple_of(gi * gs, gs), gs)
            k = ks[rows, :]
            q = qs[rows, :]
            v = vs[rows, :]
            gcs = (gcf_s[rows, :], gcb_s[rows, :])
            betas = (bf_s[rows, :], bb_s[rows, :])
            kbs = (k * betas[0], k * betas[1])
            prods = lax.dot_general(jnp.concatenate([kbs[0], kbs[1], q], axis=0).astype(BF16),
                                    k.astype(BF16), _NTDOT, preferred_element_type=F32)
            qk = prods[2 * gs:, :]
            for dirn in range(2):
                gc, beta, kb = gcs[dirn], betas[dirn], kbs[dirn]
                incl = same & ((ii >= jj) if dirn == 0 else (ii <= jj))
                strict = same & ((ii > jj) if dirn == 0 else (ii < jj))
                g_i = jnp.concatenate([gc] * (gs // LANES), axis=1) if gs % LANES == 0 else \
                    jnp.broadcast_to(gc[:, :1], (gs, gs))
                g_j = jnp.broadcast_to(jnp.transpose(gc)[0:1, :], (gs, gs))
                decay = jnp.where(incl, jnp.exp(jnp.where(incl, g_i - g_j, 0.0)), 0.0)
                edge = CHUNK - 1 if dirn == 0 else 0
                tots = [gc[cc * CHUNK + edge:cc * CHUNK + edge + 1, :] for cc in range(gc_n)]
                tot = jnp.concatenate([jnp.broadcast_to(t, (CHUNK, LANES)) for t in tots], axis=0)
                kdt = jnp.transpose(k * jnp.exp(tot - gc))
                kd_bd = [jnp.where(col_chunk == cc, kdt, 0.0) for cc in range(gc_n)]
                aqk = jnp.where(incl, qk * decay, 0.0)
                xs_.append(jnp.where(strict, -(prods[dirn * gs:(dirn + 1) * gs, :] * decay), 0.0))
                rhs_.append(jnp.concatenate([v * beta, kb * jnp.exp(gc)], axis=1))
                lhs_.append(jnp.concatenate(kd_bd + [aqk], axis=0).astype(BF16))
                meta.append((dirn, gi, rows, q * jnp.exp(gc), tots))
        xp = jnp.stack(xs_)
        pinv = eye[None] + xp
        xp = _bbdot(xp, xp)
        m = 2
        while 2 * m < CHUNK:
            t = _bbdot(xp, jnp.concatenate([xp, pinv], axis=2))
            xp, pinv = t[:, :, :gs], pinv + t[:, :, gs:]
            m *= 2
        pinv = pinv + _bbdot(xp, pinv)
        uw = _bbdot(pinv, jnp.stack(rhs_)).astype(BF16)
        res_all = _bbdot(jnp.stack(lhs_), uw)
        for idx, (dirn, gi, rows, qd, tots) in enumerate(meta):
            res = res_all[idx]
            aw = res[gc_n * GDN_DK:, :]
            qw_s[dirn, rows, :] = (qd - aw[:, GDN_DV:]).astype(BF16)
            au_s[dirn, rows, :] = aw[:, :GDN_DV]
            for cc in range(gc_n):
                c = gi * gc_n + cc
                blk = res[cc * GDN_DK:(cc + 1) * GDN_DK, :]
                n_s[dirn, c] = blk[:, :GDN_DV]
                pm_s[dirn, c] = blk[:, GDN_DV:].astype(BF16)
                gl_s[dirn, c] = jnp.broadcast_to(jnp.exp(tots[cc]), (SUBLANES, LANES))
        return carry

    lax.fori_loop(0, n_g // gb, prep, 0)

    def step(s, st):
        cs_ = (s, _bwd_chunk(s, n_cc, n_c))
        crows = [pl.ds(pl.multiple_of(c * CHUNK, CHUNK), CHUNK) for c in cs_]
        lhs = jnp.stack([jnp.concatenate([qw_s[dirn, crows[dirn], :], pm_s[dirn, cs_[dirn]]], axis=0)
                         for dirn in range(2)])
        r = _bbdot(lhs, st)
        of_s[crows[0], :] = r[0, :CHUNK, :] + au_s[0, crows[0], :]
        ob_s[crows[1], :] = r[1, :CHUNK, :] + au_s[1, crows[1], :]
        gl = jnp.stack([gl_s[dirn, cs_[dirn]][0:1, :] for dirn in range(2)])
        nn = jnp.stack([n_s[dirn, cs_[dirn]] for dirn in range(2)])
        return gl * st - r[:, CHUNK:, :] + nn

    lax.fori_loop(0, n_c, step, jnp.zeros((2, GDN_DK, GDN_DV), F32))

    o = of_s[...] + ob_s[...]
    o = o * lax.rsqrt(jnp.mean(o * o, axis=-1, keepdims=True) + EPS) * ng_ref[...]
    o_ref[...] = o * _silu(z_ref[...])


def _gdn_mixer(p, lc, conv_w, a_log, dt_bias, norm_g):
    b, lt, _ = p.shape
    n_c = lt // CHUNK
    nh = GDN_HEADS
    kern = functools.partial(_gdn_kernel, lc=lc, lt=lt)
    pcol = lambda base: pl.BlockSpec((None, lt, LANES), lambda i, h, base=base: (i, 0, base + h))
    wcol = lambda base: pl.BlockSpec((CONV_W, LANES), lambda i, h, base=base: (0, base + h))
    vec = pl.BlockSpec((1, LANES), lambda i, h: (0, 0))
    pad = jnp.zeros((LANES - 2 * nh,), F32)
    al = jnp.concatenate([a_log.reshape(-1), pad]).reshape(1, LANES)
    dtb = jnp.concatenate([dt_bias.reshape(-1), pad]).reshape(1, LANES)
    seq = lambda dt: pltpu.VMEM((lt, LANES), dt)
    return pl.pallas_call(
        kern,
        grid=(b, nh),
        in_specs=[pcol(P_QKV // LANES), pcol(P_QKV // LANES + nh), pcol(P_QKV // LANES + 2 * nh),
                  pcol(P_GZ // LANES),
                  pl.BlockSpec((None, lt, LANES), lambda i, h: (i, 0, P_SMALL // LANES)),
                  wcol(0), wcol(nh), wcol(2 * nh), vec, vec, vec],
        out_specs=pl.BlockSpec((None, lt, LANES), lambda i, h: (i, 0, h)),
        out_shape=jax.ShapeDtypeStruct((b, lt, GDN_WIDTH), F32),
        scratch_shapes=[seq(F32), seq(F32), seq(F32), seq(F32), seq(F32), seq(F32), seq(F32),
                        pltpu.VMEM((2, n_c, GDN_DK, GDN_DK), BF16),
                        pltpu.VMEM((2, n_c, GDN_DK, GDN_DV), F32),
                        pltpu.VMEM((2, lt, GDN_DK), BF16),
                        pltpu.VMEM((2, lt, GDN_DV), F32),
                        pltpu.VMEM((2, n_c, SUBLANES, LANES), F32),
                        seq(F32), seq(F32)],
        compiler_params=_cparams(("parallel", "arbitrary")),
        name="gdn_mixer",
    )(p, p, p, p, p, conv_w, conv_w, conv_w, al, dtb, norm_g.reshape(1, LANES))


M2_GW = M2_HPG * M2_HEADDIM
M2_DT_LANE = 4 * GDN_HEADS
M2_KPAD = MXU_TILE


def _ssd_kernel(x_ref, b_ref, c_ref, z_ref, sm_ref, wx_ref, wb_ref, wc_ref, bx_ref, bb_ref, bc_ref,
                al_ref, dtb_ref, dsk_ref, ng_ref, o_ref, xs, bs, cs, dt_s, ac_s, tr_s, lhs_s, rhs_s, h_s,
                *, lc, lt):
    g = pl.program_id(1)
    n_c, n_cc = lt // CHUNK, lc // CHUNK
    row = lax.broadcasted_iota(jnp.int32, (lt, 1), 0)
    lane = lax.broadcasted_iota(jnp.int32, (1, LANES), 1)
    head_of_lane = lax.broadcasted_iota(jnp.int32, (1, M2_GW), 1) // M2_HEADDIM

    masks = _conv_masks(row, lc, lt)
    xs[...] = _dwconv_silu(x_ref[...], wx_ref, bx_ref[...], masks)
    bs[...] = _dwconv_silu(b_ref[...], wb_ref, bb_ref[...], masks)
    cs[...] = _dwconv_silu(c_ref[...], wc_ref, bc_ref[...], masks)

    dt_all = jax.nn.softplus(sm_ref[...] + dtb_ref[...])
    dt_s[...] = dt_all
    ac_s[...] = _chunk_cumsum(dt_all * (-jnp.exp(al_ref[...])), row, lane < M2_DT_LANE + M2_HEADS)

    o_ref[...] = jnp.zeros_like(o_ref)
    h_s[...] = jnp.zeros_like(h_s)
    lhs_s[...] = jnp.zeros_like(lhs_s)
    rhs_s[...] = jnp.zeros_like(rhs_s)

    ii = lax.broadcasted_iota(jnp.int32, (CHUNK, CHUNK), 0)
    jj = lax.broadcasted_iota(jnp.int32, (CHUNK, CHUNK), 1)

    def step(s, carry):
        cs_ = (s, _bwd_chunk(s, n_cc, n_c))
        rws = [pl.ds(pl.multiple_of(c * CHUNK, CHUNK), CHUNK) for c in cs_]
        bc2 = jnp.stack([bs[rws[d], :] for d in range(2)]).astype(BF16)
        cf_ = [cs[rws[d], :] for d in range(2)]
        cb2 = lax.dot_general(jnp.stack(cf_).astype(BF16), bc2, (((2,), (2,)), ((0,), (0,))),
                              preferred_element_type=F32)
        hprev = h_s[...]
        xd_, tot_ = [], []
        for dirn in range(2):
            incl = (ii >= jj) if dirn == 0 else (ii <= jj)
            dtc = dt_s[rws[dirn], :]
            acc = ac_s[rws[dirn], :]
            tr_s[dirn] = jnp.transpose(acc)
            dte = jnp.zeros((CHUNK, M2_GW), F32)
            ace = jnp.zeros((CHUNK, M2_GW), F32)
            for j in range(M2_HPG):
                l_idx = M2_DT_LANE + dirn * M2_HEADS + g * M2_HPG + j
                a_col = _col(acc, lane, l_idx)
                hm = head_of_lane == j
                dte = jnp.where(hm, _col(dtc, lane, l_idx), dte)
                ace = jnp.where(hm, a_col, ace)
                a_row = tr_s[dirn, pl.ds(l_idx, 1), :]
                dec = jnp.where(incl, jnp.exp(jnp.where(incl, a_col - a_row, 0.0)), 0.0)
                blk = pl.ds(j * CHUNK, CHUNK)
                lhs_s[dirn, blk, 0:M2_STATE] = (jnp.exp(a_col) * cf_[dirn]).astype(BF16)
                lhs_s[dirn, blk, M2_STATE:M2_STATE + CHUNK] = (cb2[dirn] * dec).astype(BF16)
            tot = ace[CHUNK - 1:CHUNK, :] if dirn == 0 else ace[0:1, :]
            xdt = xs[rws[dirn], :] * dte
            rhs_s[dirn, 0:M2_STATE, :] = hprev[dirn].astype(BF16)
            rhs_s[dirn, M2_STATE:M2_STATE + CHUNK, :] = xdt.astype(BF16)
            xd_.append((xdt * jnp.exp(tot - ace)).astype(BF16))
            tot_.append(tot)
        r2 = lax.dot_general(lhs_s[...], rhs_s[...], (((2,), (1,)), ((0,), (0,))),
                             preferred_element_type=F32)
        for dirn in range(2):
            y = jnp.where(head_of_lane == 0, r2[dirn, 0:CHUNK, :], 0.0)
            for j in range(1, M2_HPG):
                y = y + jnp.where(head_of_lane == j, r2[dirn, j * CHUNK:(j + 1) * CHUNK, :], 0.0)
            o_ref[rws[dirn], :] += y
        st2 = lax.dot_general(bc2, jnp.stack(xd_), (((1,), (1,)), ((0,), (0,))),
                              preferred_element_type=F32)
        h_s[...] = jnp.exp(jnp.stack(tot_)) * hprev + st2
        return carry

    lax.fori_loop(0, n_c, step, 0, unroll=_largest_divisor(n_c, 2))

    def fin(c, carry):
        rows = pl.ds(pl.multiple_of(c * CHUNK, CHUNK), CHUNK)
        y = (o_ref[rows, :] + dsk_ref[...] * xs[rows, :]) * _silu(z_ref[rows, :])
        o_ref[rows, :] = y * lax.rsqrt(jnp.mean(y * y, axis=-1, keepdims=True) + EPS) * ng_ref[...]
        return carry

    lax.fori_loop(0, n_c, fin, 0)


def _ssd_mixer(p, lc, conv_w, conv_b, a_log, dt_bias, d_skip, norm_g):
    b, lt, _ = p.shape
    kern = functools.partial(_ssd_kernel, lc=lc, lt=lt)
    gw = M2_GW
    bcol = (P_XBC + M2_WIDTH) // LANES
    ccol = bcol + M2_GROUPS * M2_STATE // LANES
    wide = lambda base: pl.BlockSpec((None, lt, gw), lambda i, g, base=base: (i, 0, base + g))
    narrow = lambda base: pl.BlockSpec((None, lt, LANES), lambda i, g, base=base: (i, 0, base + g))
    wwide = lambda rows: pl.BlockSpec((rows, gw), lambda i, g: (0, g))
    wnar = lambda rows, base: pl.BlockSpec((rows, LANES), lambda i, g, base=base: (0, base + g))
    vec = pl.BlockSpec((1, LANES), lambda i, g: (0, 0))
    lead = jnp.zeros((M2_DT_LANE,), F32)
    tail = jnp.zeros((LANES - M2_DT_LANE - 2 * M2_HEADS,), F32)
    al = jnp.concatenate([lead, a_log.reshape(-1), tail]).reshape(1, LANES)
    dtb = jnp.concatenate([lead, dt_bias.reshape(-1), tail]).reshape(1, LANES)
    cbias = conv_b.reshape(1, M2_CONV_CH)
    wb0 = M2_WIDTH // LANES
    wc0 = wb0 + M2_GROUPS * M2_STATE // LANES
    return pl.pallas_call(
        kern,
        grid=(b, M2_GROUPS),
        in_specs=[wide(P_XBC // gw), narrow(bcol), narrow(ccol), wide(P_MZ // gw),
                  pl.BlockSpec((None, lt, LANES), lambda i, g: (i, 0, P_SMALL // LANES)),
                  wwide(CONV_W), wnar(CONV_W, wb0), wnar(CONV_W, wc0),
                  wwide(1), wnar(1, wb0), wnar(1, wc0),
                  vec, vec, wwide(1), wwide(1)],
        out_specs=pl.BlockSpec((None, lt, gw), lambda i, g: (i, 0, g)),
        out_shape=jax.ShapeDtypeStruct((b, lt, M2_WIDTH), F32),
        scratch_shapes=[pltpu.VMEM((lt, gw), F32), pltpu.VMEM((lt, LANES), F32), pltpu.VMEM((lt, LANES), F32),
                        pltpu.VMEM((lt, LANES), F32), pltpu.VMEM((lt, LANES), F32),
                        pltpu.VMEM((2, LANES, CHUNK), F32),
                        pltpu.VMEM((2, M2_HPG * CHUNK, M2_KPAD), BF16),
                        pltpu.VMEM((2, M2_KPAD, gw), BF16),
                        pltpu.VMEM((2, M2_STATE, gw), F32)],
        compiler_params=_cparams(("parallel", "arbitrary")),
        name="ssd_mixer",
    )(p, p, p, p, p, conv_w, conv_w, conv_w, cbias, cbias, cbias, al, dtb,
      jnp.repeat(d_skip, M2_HEADDIM).reshape(1, M2_WIDTH), norm_g.reshape(1, M2_WIDTH))


def _out_proj_kernel(a_ref, b_ref, m_ref, x_ref, mod_ref, w_ref, o_ref, *, lc, tm):
    r = pl.program_id(1)
    acc = jnp.dot(a_ref[...].astype(BF16), w_ref[0:S5_WIDTH, :], preferred_element_type=F32)
    acc += jnp.dot(b_ref[...].astype(BF16), w_ref[S5_WIDTH:S5_WIDTH + GDN_WIDTH, :],
                   preferred_element_type=F32)
    acc += jnp.dot(m_ref[...].astype(BF16), w_ref[S5_WIDTH + GDN_WIDTH:, :],
                   preferred_element_type=F32)
    row = r * tm + lax.broadcasted_iota(jnp.int32, (tm, 1), 0)
    o_ref[...] = x_ref[...] + _mod_vec(mod_ref, 2, row < lc) * acc


def _out_proj(a, bm, m, x, mod, w_out_bf, layer, lc):
    b, lt, d = x.shape
    tm = lt // 4 if (lt // 4) % 8 == 0 and lc <= lt // 4 else lt // 2
    kern = functools.partial(_out_proj_kernel, lc=lc, tm=tm)
    tok = lambda w: pl.BlockSpec((None, tm, w), lambda i, r: (i, r, 0))
    return pl.pallas_call(
        kern,
        grid=(b, lt // tm),
        in_specs=[tok(S5_WIDTH), tok(GDN_WIDTH), tok(M2_WIDTH), tok(d),
                  pl.BlockSpec((None, 2, 6, d), lambda i, r: (i, 0, 0, 0)),
                  pl.BlockSpec((None, MIX_WIDTH, d), lambda i, r: (layer, 0, 0))],
        out_specs=tok(d),
        out_shape=jax.ShapeDtypeStruct((b, lt, d), F32),
        compiler_params=_cparams(("parallel", "parallel")),
        name="out_proj",
    )(a, bm, m, x, mod, w_out_bf)


def _swiglu_acc(h, wg_ref, wu_ref, wd_ref):
    acc = jnp.zeros((h.shape[0], D_MODEL), F32)
    for f in range(FF_NF):
        cols = slice(f * FF_TF, (f + 1) * FF_TF)
        g = jnp.dot(h, wg_ref[:, cols], preferred_element_type=F32)
        u = jnp.dot(h, wu_ref[:, cols], preferred_element_type=F32)
        acc = acc + jnp.dot((_silu(g) * u).astype(BF16), wd_ref[cols, :], preferred_element_type=F32)
    return acc


def _ffn_dense_kernel(x_ref, mod_ref, g_ref, wg_ref, wu_ref, wd_ref, o_ref, *, lc, tm):
    r = pl.program_id(1)
    x = x_ref[...]
    row = r * tm + lax.broadcasted_iota(jnp.int32, (tm, 1), 0)
    is_ctx = row < lc
    h = (_rms(x) * g_ref[...]) * (1.0 + _mod_vec(mod_ref, 4, is_ctx)) + _mod_vec(mod_ref, 3, is_ctx)
    acc = _swiglu_acc(h.astype(BF16), wg_ref, wu_ref, wd_ref)
    o_ref[...] = x + _mod_vec(mod_ref, 5, is_ctx) * acc


def _tok_tile(lc, l):
    return math.gcd(math.gcd(lc, l), 256)


def _ffn_dense(x, mod, g, wg, wu, wd, layer, lc):
    b, lt, d = x.shape
    tm = lt // 3 if lt % (3 * SUBLANES) == 0 else _tok_tile(lc, lt - lc)
    kern = functools.partial(_ffn_dense_kernel, lc=lc, tm=tm)
    tok = pl.BlockSpec((None, tm, d), lambda i, r: (i, r, 0))
    wspec = lambda shp: pl.BlockSpec((None,) + shp, lambda i, r: (layer, 0, 0))
    return pl.pallas_call(
        kern,
        grid=(b, lt // tm),
        in_specs=[tok, pl.BlockSpec((None, 2, 6, d), lambda i, r: (i, 0, 0, 0)),
                  pl.BlockSpec((1, d), lambda i, r: (0, 0)),
                  wspec((d, D_FF)), wspec((d, D_FF)), wspec((D_FF, d))],
        out_specs=tok,
        out_shape=jax.ShapeDtypeStruct((b, lt, d), F32),
        compiler_params=_cparams(("parallel", "parallel")),
        name="ffn_dense",
    )(x, mod, g.reshape(1, d), wg, wu, wd)


def _router_kernel(x_ref, mod_ref, g_ref, rw_ref, h_ref, rt_ref, *, lc, tm):
    r = pl.program_id(1)
    x = x_ref[...]
    row = r * tm + lax.broadcasted_iota(jnp.int32, (tm, 1), 0)
    is_ctx = row < lc
    h = (_rms(x) * g_ref[...]) * (1.0 + _mod_vec(mod_ref, 4, is_ctx)) + _mod_vec(mod_ref, 3, is_ctx)
    h_ref[...] = h
    logits = jnp.dot(h, rw_ref[...], preferred_element_type=F32, precision=lax.Precision.HIGHEST)
    lane = lax.broadcasted_iota(jnp.int32, logits.shape, 1)
    neg = jnp.float32(-jnp.inf)
    lg = jnp.where(lane < N_EXPERTS, logits, neg)
    m1 = jnp.max(lg, axis=-1, keepdims=True)
    i1 = jnp.min(jnp.where(lg == m1, lane, LANES), axis=-1, keepdims=True)
    lg2 = jnp.where(lane == i1, neg, lg)
    m2 = jnp.max(lg2, axis=-1, keepdims=True)
    i2 = jnp.min(jnp.where(lg2 == m2, lane, LANES), axis=-1, keepdims=True)
    e2 = jnp.exp(m2 - m1)
    den = 1.0 + e2
    out = jnp.where(lane == 0, i1.astype(F32), 0.0)
    out = jnp.where(lane == 1, i2.astype(F32), out)
    out = jnp.where(lane == 2, 1.0 / den, out)
    out = jnp.where(lane == 3, e2 / den, out)
    rt_ref[...] = out


def _router(x, mod, g, router_w, lc):
    b, lt, d = x.shape
    tm = _tok_tile(lc, lt - lc)
    rw = jnp.concatenate([router_w, jnp.zeros((d, LANES - N_EXPERTS), F32)], axis=1)
    kern = functools.partial(_router_kernel, lc=lc, tm=tm)
    tok = lambda w: pl.BlockSpec((None, tm, w), lambda i, r: (i, r, 0))
    return pl.pallas_call(
        kern,
        grid=(b, lt // tm),
        in_specs=[tok(d), pl.BlockSpec((None, 2, 6, d), lambda i, r: (i, 0, 0, 0)),
                  pl.BlockSpec((1, d), lambda i, r: (0, 0)),
                  pl.BlockSpec((d, LANES), lambda i, r: (0, 0))],
        out_specs=[tok(d), tok(LANES)],
        out_shape=[jax.ShapeDtypeStruct((b, lt, d), F32), jax.ShapeDtypeStruct((b, lt, LANES), F32)],
        compiler_params=_cparams(("parallel", "parallel")),
        name="moe_router",
    )(x, mod, g.reshape(1, d), rw)


def _row_copy(src_hbm, dst, idx, r, sem):
    return pltpu.make_async_copy(src_hbm.at[pl.ds(idx, 1), :], dst.at[pl.ds(r, 1), :], sem)


def _expert_kernel(te_ref, nt_ref, idx_ref, idx_next_ref, h_hbm, wg_ref, wu_ref, wd_ref, o_ref, xbuf, sems):
    t = pl.program_id(0)
    nt = nt_ref[0]
    slot = lax.rem(t, 2)

    def gather(idx, s, start):
        def body(r, c):
            cp = _row_copy(h_hbm, xbuf.at[s], idx[r] if start else 0, r, sems.at[s])
            cp.start() if start else cp.wait()
            return c
        lax.fori_loop(0, MOE_TM, body, 0, unroll=8)

    @pl.when((t == 0) & (nt > 0))
    def _():
        gather(idx_ref, 0, True)

    @pl.when(t + 1 < nt)
    def _():
        gather(idx_next_ref, 1 - slot, True)

    @pl.when(t < nt)
    def _():
        gather(idx_ref, slot, False)
        o_ref[...] = _swiglu_acc(xbuf[slot].astype(BF16), wg_ref, wu_ref, wd_ref)

    @pl.when(t >= nt)
    def _():
        o_ref[...] = jnp.zeros_like(o_ref)


def _experts(h2, src_row, tile_expert, n_tiles_used, wg, wu, wd, layer):
    rp = src_row.shape[0]
    d = h2.shape[1]
    n_tiles = rp // MOE_TM
    wspec = lambda shp: pl.BlockSpec((None, None) + shp, lambda t, te, nt: (layer, te[t], 0, 0))
    grid_spec = pltpu.PrefetchScalarGridSpec(
        num_scalar_prefetch=2,
        grid=(n_tiles,),
        in_specs=[pl.BlockSpec((MOE_TM,), lambda t, te, nt: (t,), memory_space=pltpu.SMEM),
                  pl.BlockSpec((MOE_TM,), lambda t, te, nt: (jnp.minimum(t + 1, n_tiles - 1),),
                               memory_space=pltpu.SMEM),
                  pl.BlockSpec(memory_space=pl.ANY),
                  wspec((d, D_FF)), wspec((d, D_FF)), wspec((D_FF, d))],
        out_specs=pl.BlockSpec((MOE_TM, d), lambda t, te, nt: (t, 0)),
        scratch_shapes=[pltpu.VMEM((2, MOE_TM, d), F32), pltpu.SemaphoreType.DMA((2,))],
    )
    return pl.pallas_call(
        _expert_kernel,
        grid_spec=grid_spec,
        out_shape=jax.ShapeDtypeStruct((rp, d), F32),
        compiler_params=_cparams(("arbitrary",)),
        name="moe_experts",
    )(tile_expert, n_tiles_used, src_row, src_row, h2, wg, wu, wd)


def _combine_kernel(p1_ref, p2_ref, y_hbm, x_ref, rt_ref, mod_ref, o_ref, ya, yb, sem, *, lc, tm):
    r = pl.program_id(1)

    def start(k, c):
        _row_copy(y_hbm, ya, p1_ref[k], k, sem).start(priority=0)
        _row_copy(y_hbm, yb, p2_ref[k], k, sem).start(priority=1)
        return c
    lax.fori_loop(0, tm, start, 0)

    def wait(k, c):
        _row_copy(y_hbm, ya, 0, k, sem).wait()
        _row_copy(y_hbm, yb, 0, k, sem).wait()
        return c
    lax.fori_loop(0, tm, wait, 0)

    row = r * tm + lax.broadcasted_iota(jnp.int32, (tm, 1), 0)
    rt = rt_ref[...]
    y = rt[:, 2:3] * ya[...] + rt[:, 3:4] * yb[...]
    o_ref[...] = x_ref[...] + _mod_vec(mod_ref, 5, row < lc) * y


def _combine(p1, p2, y, x, rt, mod, lc):
    b, lt, d = x.shape
    tm = _tok_tile(lc, lt - lc)
    nt = lt // tm
    kern = functools.partial(_combine_kernel, lc=lc, tm=tm)
    tok = lambda w: pl.BlockSpec((None, tm, w), lambda i, r: (i, r, 0))
    ispec = pl.BlockSpec((tm,), lambda i, r: (i * nt + r,), memory_space=pltpu.SMEM)
    return pl.pallas_call(
        kern,
        grid=(b, nt),
        in_specs=[ispec, ispec, pl.BlockSpec(memory_space=pl.ANY), tok(d), tok(LANES),
                  pl.BlockSpec((None, 2, 6, d), lambda i, r: (i, 0, 0, 0))],
        out_specs=tok(d),
        out_shape=jax.ShapeDtypeStruct((b, lt, d), F32),
        scratch_shapes=[pltpu.VMEM((tm, d), F32), pltpu.VMEM((tm, d), F32),
                        pltpu.SemaphoreType.DMA(())],
        compiler_params=_cparams(("arbitrary", "arbitrary")),
        name="moe_combine",
    )(p1, p2, y, x, rt, mod)


def _moe(x, mod, g, router_w, wg, wu, wd, layer, lc):
    b, lt, d = x.shape
    n = b * lt
    h2, rt = _router(x, mod, g, router_w, lc)
    rt2 = rt.reshape(n, LANES)
    e_flat = jnp.concatenate([rt2[:, 0], rt2[:, 1]]).astype(jnp.int32)
    onehot = (e_flat[:, None] == jnp.arange(N_EXPERTS, dtype=jnp.int32)[None, :]).astype(jnp.int32)
    counts = jnp.sum(onehot, axis=0)
    rank = jnp.sum((jnp.cumsum(onehot, axis=0) - 1) * onehot, axis=1)
    padded = ((counts + MOE_TM - 1) // MOE_TM) * MOE_TM
    ends = jnp.cumsum(padded)
    pos = (ends - padded)[e_flat] + rank
    rp = ((2 * n + MOE_TM - 1) // MOE_TM + N_EXPERTS) * MOE_TM
    tok_id = jnp.concatenate([jnp.arange(n, dtype=jnp.int32)] * 2)
    src_row = jnp.zeros((rp,), jnp.int32).at[pos].set(tok_id)
    tile_start = jnp.arange(rp // MOE_TM, dtype=jnp.int32) * MOE_TM
    tile_expert = jnp.minimum(jnp.sum(tile_start[:, None] >= ends[None, :], axis=1),
                              N_EXPERTS - 1).astype(jnp.int32)
    n_used = (ends[-1] // MOE_TM).astype(jnp.int32).reshape(1)
    ys = _experts(h2.reshape(n, d), src_row, tile_expert, n_used, wg, wu, wd, layer)
    return _combine(pos[:n].astype(jnp.int32), pos[n:].astype(jnp.int32), ys, x, rt, mod, lc)


def _final_norm_kernel(x_ref, g_ref, o_ref):
    o_ref[...] = _rms(x_ref[...]) * g_ref[...]


def _final_norm(x, g):
    b, l, d = x.shape
    tm = math.gcd(l, 512)
    tok = pl.BlockSpec((None, tm, d), lambda i, r: (i, r, 0))
    return pl.pallas_call(
        _final_norm_kernel,
        grid=(b, l // tm),
        in_specs=[tok, pl.BlockSpec((1, d), lambda i, r: (0, 0))],
        out_specs=tok,
        out_shape=jax.ShapeDtypeStruct((b, l, d), F32),
        compiler_params=_cparams(("parallel", "parallel")),
        name="final_norm",
    )(x, g.reshape(1, d))


def _grid_t(xl, rows, cols):
    b, l, ch = xl.shape
    return xl.reshape(b, rows, cols, ch).transpose(0, 2, 1, 3).reshape(b, l, ch)


def kernel(x, c, ctx, c_ctx, ada_w, ada_b, norm1_g, norm2_g, w_in, w_out,
           s5_lam_re, s5_lam_im, s5_b_re, s5_b_im, s5_c_re, s5_c_im, s5_log_dt, s5_d, s5_w_glu,
           gdn_conv_w, gdn_a_log, gdn_dt_bias, gdn_norm_g,
           m2_conv_w, m2_conv_b, m2_a_log, m2_dt_bias, m2_d, m2_norm_g,
           ffn_w_gate, ffn_w_up, ffn_w_down,
           moe_router, moe_w_gate, moe_w_up, moe_w_down, final_norm_g):
    b, l, d = x.shape
    lc = ctx.shape[1]
    lt = lc + l
    depth = ada_w.shape[0]
    rows = l // GRID_W
    assert b % SUBLANES == 0 and lc % CHUNK == 0 and l % CHUNK == 0

    cpad = jnp.zeros((b + SUBLANES, d), F32).at[:b].set(c).at[b].set(c_ctx)
    mods = _ada_all(cpad, ada_w, ada_b).reshape(depth, b + SUBLANES, 6, d)

    mod_all = jnp.stack([jnp.broadcast_to(mods[:, b][:, None], (depth, b, 6, d)), mods[:, :b]], axis=2)
    w_in_p = jax.vmap(_pack_w_in)(w_in)
    w_out_b = w_out.astype(BF16)
    s5_glu_b = s5_w_glu.astype(BF16)
    s5_par_all = jax.vmap(_s5_params)(s5_lam_re, s5_lam_im, s5_b_re, s5_b_im, s5_c_re, s5_c_im, s5_log_dt)
    ffn_b = [w.astype(BF16) for w in (ffn_w_gate, ffn_w_up, ffn_w_down)]
    moe_b = [w.astype(BF16) for w in (moe_w_gate, moe_w_up, moe_w_down)]

    xt = jnp.concatenate([ctx, x], axis=1)
    col_major = False
    for i in range(depth):
        want_cm = i % 2 == 1
        if want_cm != col_major:
            lat = xt[:, lc:]
            lat = _grid_t(lat, rows, GRID_W) if want_cm else _grid_t(lat, GRID_W, rows)
            xt = jnp.concatenate([xt[:, :lc], lat], axis=1)
            col_major = want_cm
        mod = mod_all[i]

        p = _in_proj(xt, mod, norm1_g[i], w_in_p, i, lc)
        u_tb = jnp.transpose(p[:, :, P_U:P_U + S5_WIDTH], (1, 0, 2)).reshape(lt * b, S5_WIDTH)
        ya = _s5_mixer(u_tb, lc, b, s5_par_all, s5_d[i], s5_glu_b, i)
        ya = jnp.transpose(ya.reshape(lt, b, S5_WIDTH), (1, 0, 2))
        yb = _gdn_mixer(p, lc, gdn_conv_w[i], gdn_a_log[i], gdn_dt_bias[i], gdn_norm_g[i])
        ym = _ssd_mixer(p, lc, m2_conv_w[i], m2_conv_b[i], m2_a_log[i], m2_dt_bias[i], m2_d[i],
                        m2_norm_g[i])
        xt = _out_proj(ya, yb, ym, xt, mod, w_out_b, i, lc)

        j = i // 2
        if i % 2 == 0:
            xt = _ffn_dense(xt, mod, norm2_g[i], ffn_b[0], ffn_b[1], ffn_b[2], j, lc)
        else:
            xt = _moe(xt, mod, norm2_g[i], moe_router[j], moe_b[0], moe_b[1], moe_b[2], j, lc)

    lat = xt[:, lc:]
    if col_major:
        lat = _grid_t(lat, GRID_W, rows)
    return _final_norm(lat, final_norm_g)
```
